```python
import jax
import jax.numpy as jnp
from jax import lax
import numpy as np

D_MODEL = 1024
BATCH = 16
SEQ = 2048
DEPTH = 1
DEC_BATCH = 128
DEC_SEQ = 1
PAST_LEN = 8192
PAGE_SIZE = 128

NSA_HEADS = 8
NSA_KV_HEADS = 2
NSA_HD = 64
NSA_GROUP = NSA_HEADS // NSA_KV_HEADS
NSA_SCALE = NSA_HD ** -0.5
CMP_BLOCK = 32
CMP_STRIDE = 16
CMP_HID = 128
SEL_BLOCK = 64
SEL_TOP = 16
WINDOW = 512
SEL_Q_BLOCK = 64
WIN_Q_BLOCK = 128
FORCE_SCORE = 1e9
GDN_HEADS = 4
GDN_DK = 128
GDN_DV = 128
CONV_W = 4
GDN_CHUNK = 64
MEM_TOKENS = 256
MEM_HEADS = 4
MEM_HD = 128
MEM_SCALE = MEM_HD ** -0.5
D_FF = 4 * D_MODEL
ROPE_THETA = 10000.0
EPS = 1e-6

Q_NSA = NSA_HEADS * NSA_HD
KV_NSA = NSA_KV_HEADS * NSA_HD
GDN_QK = GDN_HEADS * GDN_DK
GDN_V = GDN_HEADS * GDN_DV
CONV_DIM = 2 * GDN_QK + GDN_V
MEM_Q = MEM_HEADS * MEM_HD
N_BRANCH = 3
IN_SPLITS = (Q_NSA, 6 * KV_NSA, 3 * NSA_HEADS, CONV_DIM, GDN_HEADS, GDN_HEADS, GDN_V, MEM_Q, N_BRANCH * D_MODEL)
IN_DIM = sum(IN_SPLITS)

kernel_name = 'hybrid_nsa_gdn_memory_decoder_step'


def split_offsets(sizes):
    offs, acc = [], 0
    for s in sizes[:-1]:
        acc += s
        offs.append(acc)
    return offs


def rmsnorm(x, g):
    xf = x.astype(jnp.float32)
    y = xf * lax.rsqrt(jnp.mean(xf * xf, axis=-1, keepdims=True) + EPS)
    return (y * g.astype(jnp.float32)).astype(x.dtype)


def l2norm(x):
    return x * lax.rsqrt(jnp.sum(x * x, axis=-1, keepdims=True) + EPS)


def rope(x, pos):
    half = x.shape[-1] // 2
    inv = ROPE_THETA ** (-jnp.arange(half, dtype=jnp.float32) / half)
    ang = pos.astype(jnp.float32)[:, None] * inv[None, :]
    bshape = (1, pos.shape[0]) + (1,) * (x.ndim - 3) + (half,)
    cos = jnp.cos(ang).reshape(bshape)
    sin = jnp.sin(ang).reshape(bshape)
    xf = x.astype(jnp.float32)
    x1, x2 = xf[..., :half], xf[..., half:]
    return jnp.concatenate([x1 * cos - x2 * sin, x2 * cos + x1 * sin], axis=-1).astype(x.dtype)


def masked_softmax(s, mask):
    s = jnp.where(mask, s.astype(jnp.float32), -1e30)
    p = jax.nn.softmax(s, axis=-1)
    return jnp.where(mask, p, 0.0)


def compress_blocks(x, pe, w1, w2):
    B, N = x.shape[0], x.shape[1]
    R = CMP_BLOCK // CMP_STRIDE
    n_chunks = N // CMP_STRIDE
    n_cmp = n_chunks - R + 1
    chunks = x[:, :n_chunks * CMP_STRIDE].reshape(B, n_chunks, CMP_STRIDE, NSA_KV_HEADS, NSA_HD)
    w1r = w1.reshape(R, CMP_STRIDE, NSA_HD, CMP_HID)
    h = jnp.einsum('ld,ldf->f', pe, w1.reshape(CMP_BLOCK, NSA_HD, CMP_HID))
    for r in range(R):
        h = h + jnp.einsum('bnsgd,sdf->bngf', chunks[:, r:r + n_cmp], w1r[r])
    return jnp.einsum('bngf,fd->bngd', jax.nn.gelu(h), w2)


def cmp_to_sel(n_cmp, n_sel):
    i = jnp.arange(n_cmp)[:, None]
    j = jnp.arange(n_sel)[None, :]
    hit = (i * CMP_STRIDE < (j + 1) * SEL_BLOCK) & (i * CMP_STRIDE + CMP_BLOCK > j * SEL_BLOCK)
    return hit.astype(jnp.float32)


def sel_attend(qg, ksb, vsb, idx, q_pos):
    B, Tq = qg.shape[0], qg.shape[1]
    n_top = idx.shape[-1]
    qb = SEL_Q_BLOCK if Tq % SEL_Q_BLOCK == 0 else Tq
    nqb = Tq // qb
    bi = jnp.arange(B)[:, None, None, None]
    gi = jnp.arange(NSA_KV_HEADS)[None, :, None, None]
    offs = jnp.arange(SEL_BLOCK, dtype=jnp.int32)

    def one_block(args):
        q_blk, idx_blk, pos_blk = args
        kb = ksb[bi, gi, idx_blk]
        vb = vsb[bi, gi, idx_blk]
        s = jnp.einsum('bqgjd,bgqnld->bgjqnl', q_blk, kb) * NSA_SCALE
        kpos = idx_blk[..., None] * SEL_BLOCK + offs
        mask = kpos <= pos_blk[None, None, :, None, None]
        sh = s.shape
        p = masked_softmax(s.reshape(sh[0], sh[1], sh[2], sh[3], -1), mask.reshape(B, NSA_KV_HEADS, 1, qb, -1))
        p = p.reshape(sh).astype(q_blk.dtype)
        return jnp.einsum('bgjqnl,bgqnld->bqgjd', p, vb)

    q_blocks = qg.reshape(B, nqb, qb, NSA_KV_HEADS, NSA_GROUP, NSA_HD).transpose(1, 0, 2, 3, 4, 5)
    idx_blocks = idx.reshape(B, NSA_KV_HEADS, nqb, qb, n_top).transpose(2, 0, 1, 3, 4)
    pos_blocks = q_pos.reshape(nqb, qb)
    out = lax.map(one_block, (q_blocks, idx_blocks, pos_blocks))
    return out.transpose(1, 0, 2, 3, 4, 5).reshape(B, Tq, NSA_KV_HEADS, NSA_GROUP, NSA_HD)


def window_attend(qg, kw, vw, kw_pos, q_pos):
    B, Tq = qg.shape[0], qg.shape[1]
    Nk = kw.shape[1]
    qb = WIN_Q_BLOCK if Tq % WIN_Q_BLOCK == 0 else Tq
    nqb = Tq // qb
    span = WINDOW + qb
    padw = ((0, 0), (WINDOW, 0), (0, 0), (0, 0))
    kp = jnp.pad(kw, padw)
    vp = jnp.pad(vw, padw)
    pos_p = jnp.concatenate([jnp.zeros((WINDOW,), jnp.int32), kw_pos])
    val_p = jnp.concatenate([jnp.zeros((WINDOW,), bool), jnp.ones((Nk,), bool)])
    start = jnp.arange(nqb, dtype=jnp.int32) * qb + (Nk - Tq)
    idx = start[:, None] + jnp.arange(span, dtype=jnp.int32)[None, :]
    kb = kp[:, idx]
    vb = vp[:, idx]
    kpos = pos_p[idx][:, None, :]
    qpos = q_pos.reshape(nqb, qb)[:, :, None]
    mask = val_p[idx][:, None, :] & (kpos <= qpos) & (kpos > qpos - WINDOW)
    qblk = qg.reshape(B, nqb, qb, NSA_KV_HEADS, NSA_GROUP, NSA_HD)
    s = jnp.einsum('bnqgjd,bnkgd->bgjnqk', qblk, kb) * NSA_SCALE
    p = masked_softmax(s, mask).astype(qg.dtype)
    o = jnp.einsum('bgjnqk,bnkgd->bnqgjd', p, vb)
    return o.reshape(B, Tq, NSA_KV_HEADS, NSA_GROUP, NSA_HD)


def nsa_core(q, q_pos, cmp_kv, sel_kv, win_kv, kw_pos, gates, cmp_pe, cmp_w1, cmp_w2):
    B, Tq = q.shape[0], q.shape[1]
    N = cmp_kv.shape[1]
    qg = q.reshape(B, Tq, NSA_KV_HEADS, NSA_GROUP, NSA_HD)
    kcmp = compress_blocks(cmp_kv[:, :, 0], cmp_pe[0], cmp_w1[0], cmp_w2[0])
    vcmp = compress_blocks(cmp_kv[:, :, 1], cmp_pe[1], cmp_w1[1], cmp_w2[1])
    n_cmp = kcmp.shape[1]
    blk_end = jnp.arange(n_cmp, dtype=jnp.int32) * CMP_STRIDE + CMP_BLOCK - 1
    cmask = blk_end[None, :] <= q_pos[:, None]
    s = jnp.einsum('btgjd,bigd->bgjti', qg, kcmp) * NSA_SCALE
    p_cmp = masked_softmax(s, cmask)
    o_cmp = jnp.einsum('bgjti,bigd->btgjd', p_cmp.astype(q.dtype), vcmp)
    n_sel = -(-N // SEL_BLOCK)
    imp = jnp.einsum('bgjti,is->bgts', p_cmp, cmp_to_sel(n_cmp, n_sel))
    jj = jnp.arange(n_sel, dtype=jnp.int32)[None, :]
    imp = jnp.where(jj * SEL_BLOCK <= q_pos[:, None], imp, -jnp.inf)
    imp = jnp.where((jj == 0) | (jj == q_pos[:, None] // SEL_BLOCK), FORCE_SCORE, imp)
    _, idx = lax.top_k(imp, min(SEL_TOP, n_sel))
    pad = ((0, 0), (0, n_sel * SEL_BLOCK - N), (0, 0), (0, 0))
    ksb = jnp.pad(sel_kv[:, :, 0], pad).reshape(B, n_sel, SEL_BLOCK, NSA_KV_HEADS, NSA_HD).transpose(0, 3, 1, 2, 4)
    vsb = jnp.pad(sel_kv[:, :, 1], pad).reshape(B, n_sel, SEL_BLOCK, NSA_KV_HEADS, NSA_HD).transpose(0, 3, 1, 2, 4)
    o_sel = sel_attend(qg, ksb, vsb, idx.astype(jnp.int32), q_pos)
    o_win = window_attend(qg, win_kv[:, :, 0], win_kv[:, :, 1], kw_pos, q_pos)
    g = jax.nn.sigmoid(gates.astype(jnp.float32)).astype(q.dtype).reshape(B, Tq, NSA_KV_HEADS, NSA_GROUP, 3)
    o = g[..., 0:1] * o_cmp + g[..., 1:2] * o_sel + g[..., 2:3] * o_win
    return o.reshape(B, Tq, NSA_HEADS * NSA_HD)


def short_conv(x, buf, w):
    T = x.shape[1]
    xx = jnp.concatenate([buf.astype(x.dtype), x], axis=1)
    y = xx[:, 0:T] * w[0]
    for j in range(1, CONV_W):
        y = y + xx[:, j:j + T] * w[j]
    return jax.nn.silu(y), xx[:, -(CONV_W - 1):]


def gated_delta_chunked(q, k, v, g, beta, s0):
    B, T, H, DK = q.shape
    C = GDN_CHUNK if T % GDN_CHUNK == 0 else T
    nc = T // C

    def to_chunks(a):
        a = a.reshape((B, nc, C, H) + a.shape[3:])
        return jnp.moveaxis(a, (1, 3), (0, 2))

    qc = to_chunks(q * (DK ** -0.5))
    kc = to_chunks(k)
    vc = to_chunks(v)
    bc = to_chunks(beta)
    dec = jnp.cumsum(to_chunks(g), axis=-1)
    tril = jnp.tril(jnp.ones((C, C), bool))
    strict = jnp.tril(jnp.ones((C, C), bool), -1)
    diff = dec[..., :, None] - dec[..., None, :]
    lmask = jnp.where(tril, jnp.exp(jnp.where(tril, diff, 0.0)), 0.0)
    kb = kc * bc[..., None]
    a_mat = jnp.where(strict, jnp.einsum('nbhid,nbhjd->nbhij', kb, kc) * lmask, 0.0)
    t_mat = a_mat + jnp.eye(C, dtype=jnp.float32)
    u = jax.lax.linalg.triangular_solve(t_mat, vc * bc[..., None], left_side=True, lower=True, unit_diagonal=True)
    w = jax.lax.linalg.triangular_solve(t_mat, kb * jnp.exp(dec)[..., None], left_side=True, lower=True, unit_diagonal=True)
    attn_in = jnp.einsum('nbhid,nbhjd->nbhij', qc, kc) * lmask

    def step(S, xs):
        q_i, k_i, u_i, w_i, a_i, d_i = xs
        v_new = u_i - jnp.einsum('bhcd,bhde->bhce', w_i, S)
        o = jnp.einsum('bhcd,bhde->bhce', q_i * jnp.exp(d_i)[..., None], S) + jnp.einsum('bhij,bhje->bhie', a_i, v_new)
        S = S * jnp.exp(d_i[..., -1])[..., None, None] + jnp.einsum('bhcd,bhce->bhde', k_i * jnp.exp(d_i[..., -1:] - d_i)[..., None], v_new)
        return S, o

    s_fin, o = lax.scan(step, s0, (qc, kc, u, w, attn_in, dec))
    o = jnp.moveaxis(o, (0, 2), (1, 3)).reshape(B, T, H, v.shape[-1])
    return o, s_fin


def gdn_mixer(qkv, b_raw, a_raw, z, conv_buf, s0, conv_w, A_log, dt_bias, o_norm):
    B, T, _ = qkv.shape
    c, conv_new = short_conv(qkv, conv_buf, conv_w)
    c = c.astype(jnp.float32)
    q, k, v = jnp.split(c, [GDN_QK, 2 * GDN_QK], axis=-1)
    q = l2norm(q.reshape(B, T, GDN_HEADS, GDN_DK))
    k = l2norm(k.reshape(B, T, GDN_HEADS, GDN_DK))
    v = v.reshape(B, T, GDN_HEADS, GDN_DV)
    beta = jax.nn.sigmoid(b_raw.astype(jnp.float32))
    g = -jnp.exp(A_log.astype(jnp.float32)) * jax.nn.softplus(a_raw.astype(jnp.float32) + dt_bias.astype(jnp.float32))
    o, s_new = gated_delta_chunked(q, k, v, g, beta, s0.astype(jnp.float32))
    o = rmsnorm(o, o_norm) * jax.nn.silu(z.astype(jnp.float32).reshape(B, T, GDN_HEADS, GDN_DV))
    return o.reshape(B, T, GDN_V).astype(qkv.dtype), s_new.astype(s0.dtype), conv_new


def memory_kv(mem, g_mem, w_mem_kv, mem_k_norm):
    B, M, _ = mem.shape
    kv = jnp.einsum('bmd,df->bmf', rmsnorm(mem, g_mem), w_mem_kv).reshape(B, M, 2, MEM_HEADS, MEM_HD)
    return jnp.stack([rmsnorm(kv[:, :, 0], mem_k_norm), kv[:, :, 1]], axis=2)


def mem_attend(q, kv):
    B, T = q.shape[0], q.shape[1]
    kv = kv.astype(q.dtype)
    s = jnp.einsum('bthd,bmhd->bhtm', q, kv[:, :, 0]) * MEM_SCALE
    p = jax.nn.softmax(s.astype(jnp.float32), axis=-1).astype(q.dtype)
    return jnp.einsum('bhtm,bmhd->bthd', p, kv[:, :, 1]).reshape(B, T, MEM_Q)


def gather_pages(pool, page_table):
    g = pool[page_table]
    return g.reshape((page_table.shape[0], -1) + pool.shape[2:])


def hybrid_layer(x, q_off, past_cmp, past_sel, win_buf, win_keep, gdn_s0, conv_buf, mem_kv, params):
    (g_mix, w_in, nsa_q_norm, nsa_k_norm, cmp_pe, cmp_w1, cmp_w2, gdn_conv_w, gdn_A_log,
     gdn_dt_bias, gdn_o_norm, mem_q_norm, w_br_nsa, w_br_gdn, w_br_mem, w_out, g_ffn, w_ff1, w_ff2) = params
    B, T, _ = x.shape
    pos = q_off + jnp.arange(T, dtype=jnp.int32)
    u = rmsnorm(x, g_mix)
    proj = jnp.einsum('btd,df->btf', u, w_in)
    (q_nsa, kv_nsa, g_nsa, qkv_gdn, b_gdn, a_gdn, z_gdn, q_mem, g_br) = jnp.split(proj, split_offsets(IN_SPLITS), axis=-1)
    q = rope(rmsnorm(q_nsa.reshape(B, T, NSA_HEADS, NSA_HD), nsa_q_norm), pos)
    kv = kv_nsa.reshape(B, T, 3, 2, NSA_KV_HEADS, NSA_HD)
    k = rope(rmsnorm(kv[:, :, :, 0], nsa_k_norm[:, None, :]), pos)
    rows = jnp.stack([k, kv[:, :, :, 1]], axis=3)
    new_cmp, new_sel, new_win = rows[:, :, 0], rows[:, :, 1], rows[:, :, 2]
    cmp_kv = jnp.concatenate([past_cmp.astype(x.dtype), new_cmp], axis=1)
    sel_kv = jnp.concatenate([past_sel.astype(x.dtype), new_sel], axis=1)
    win_kv = jnp.concatenate([win_buf.astype(x.dtype), new_win], axis=1)
    win_len = win_buf.shape[1]
    kw_pos = q_off - win_len + jnp.arange(win_len + T, dtype=jnp.int32)
    o_nsa = nsa_core(q, pos, cmp_kv, sel_kv, win_kv, kw_pos, g_nsa.reshape(B, T, NSA_HEADS, 3), cmp_pe, cmp_w1, cmp_w2)
    o_gdn, s_new, conv_new = gdn_mixer(qkv_gdn, b_gdn, a_gdn, z_gdn, conv_buf, gdn_s0, gdn_conv_w, gdn_A_log, gdn_dt_bias, gdn_o_norm)
    qm = rmsnorm(q_mem.reshape(B, T, MEM_HEADS, MEM_HD), mem_q_norm)
    o_mem = mem_attend(qm, mem_kv)
    gates = jax.nn.sigmoid(g_br.astype(jnp.float32)).astype(x.dtype).reshape(B, T, N_BRANCH, D_MODEL)
    merged = (gates[:, :, 0] * jnp.einsum('btc,cd->btd', o_nsa, w_br_nsa)
              + gates[:, :, 1] * jnp.einsum('btc,cd->btd', o_gdn, w_br_gdn)
              + gates[:, :, 2] * jnp.einsum('btc,cd->btd', o_mem, w_br_mem))
    h = x + jnp.einsum('btd,de->bte', merged, w_out)
    f = jnp.square(jax.nn.relu(jnp.einsum('btd,df->btf', rmsnorm(h, g_ffn), w_ff1)))
    y = h + jnp.einsum('btf,fd->btd', f, w_ff2)
    return y, new_cmp, new_sel, win_kv[:, -win_keep:], s_new, conv_new


def setup_inputs(seed: int = 0) -> dict:
    key = jax.random.key(seed)
    ks = jax.random.split(key, 40)
    f32 = jnp.float32

    def nrm(k, shape, scale):
        return jax.random.normal(k, shape, f32) * scale

    n_pages = PAST_LEN // PAGE_SIZE
    n_pool = (DEC_BATCH * n_pages * 5) // 4
    win_len = min(WINDOW, PAST_LEN)
    perm = jax.random.permutation(ks[0], n_pool)
    page_table = perm[:DEC_BATCH * n_pages].reshape(DEC_BATCH, n_pages).astype(jnp.int32)
    dt = jnp.exp(jax.random.uniform(ks[1], (GDN_HEADS,), f32, np.log(1e-3), np.log(1e-1)))
    return {
        'x_prompt': nrm(ks[2], (BATCH, SEQ, D_MODEL), 1.0),
        'x_sample': nrm(ks[3], (DEC_BATCH, DEC_SEQ, D_MODEL), 1.0),
        'cache_cmp_kv': nrm(ks[4], (n_pool, PAGE_SIZE, 2, NSA_KV_HEADS, NSA_HD), 1.0),
        'cache_sel_kv': nrm(ks[5], (n_pool, PAGE_SIZE, 2, NSA_KV_HEADS, NSA_HD), 1.0),
        'cache_win_kv': nrm(ks[6], (DEC_BATCH, win_len, 2, NSA_KV_HEADS, NSA_HD), 1.0),
        'state_gdn': nrm(ks[7], (DEC_BATCH, GDN_HEADS, GDN_DK, GDN_DV), 0.3),
        'state_gdn_conv': nrm(ks[8], (DEC_BATCH, CONV_W - 1, CONV_DIM), 1.0),
        'cache_mem_kv': nrm(ks[9], (DEC_BATCH, MEM_TOKENS, 2, MEM_HEADS, MEM_HD), 1.0),
        'page_table': page_table,
        'mem_prompt': nrm(ks[10], (BATCH, MEM_TOKENS, D_MODEL), 1.0),
        'g_mix': 1.0 + nrm(ks[11], (D_MODEL,), 0.1),
        'w_in': nrm(ks[12], (D_MODEL, IN_DIM), D_MODEL ** -0.5),
        'nsa_q_norm': 1.0 + nrm(ks[13], (NSA_HD,), 0.1),
        'nsa_k_norm': 1.0 + nrm(ks[14], (3, NSA_HD), 0.1),
        'cmp_pe': nrm(ks[15], (2, CMP_BLOCK, NSA_HD), 0.5),
        'cmp_w1': nrm(ks[16], (2, CMP_BLOCK * NSA_HD, CMP_HID), (CMP_BLOCK * NSA_HD) ** -0.5),
        'cmp_w2': nrm(ks[17], (2, CMP_HID, NSA_HD), CMP_HID ** -0.5),
        'gdn_conv_w': nrm(ks[18], (CONV_W, CONV_DIM), CONV_W ** -0.5),
        'gdn_A_log': jnp.log(jax.random.uniform(ks[19], (GDN_HEADS,), f32, 1.0, 16.0)),
        'gdn_dt_bias': jnp.log(jnp.expm1(dt)),
        'gdn_o_norm': 1.0 + nrm(ks[20], (GDN_DV,), 0.1),
        'g_mem': 1.0 + nrm(ks[21], (D_MODEL,), 0.1),
        'w_mem_kv': nrm(ks[22], (D_MODEL, 2 * MEM_Q), D_MODEL ** -0.5),
        'mem_q_norm': 1.0 + nrm(ks[23], (MEM_HD,), 0.1),
        'mem_k_norm': 1.0 + nrm(ks[24], (MEM_HD,), 0.1),
        'w_br_nsa': nrm(ks[25], (Q_NSA, D_MODEL), Q_NSA ** -0.5),
        'w_br_gdn': nrm(ks[26], (GDN_V, D_MODEL), GDN_V ** -0.5),
        'w_br_mem': nrm(ks[27], (MEM_Q, D_MODEL), MEM_Q ** -0.5),
        'w_out': nrm(ks[28], (D_MODEL, D_MODEL), D_MODEL ** -0.5),
        'g_ffn': 1.0 + nrm(ks[29], (D_MODEL,), 0.1),
        'w_ff1': nrm(ks[30], (D_MODEL, D_FF), D_MODEL ** -0.5),
        'w_ff2': nrm(ks[31], (D_FF, D_MODEL), D_FF ** -0.5),
    }


def reference(x_prompt, x_sample, cache_cmp_kv, cache_sel_kv, cache_win_kv, state_gdn, state_gdn_conv,
              cache_mem_kv, page_table, mem_prompt, g_mix, w_in, nsa_q_norm, nsa_k_norm, cmp_pe, cmp_w1,
              cmp_w2, gdn_conv_w, gdn_A_log, gdn_dt_bias, gdn_o_norm, g_mem, w_mem_kv, mem_q_norm,
              mem_k_norm, w_br_nsa, w_br_gdn, w_br_mem, w_out, g_ffn, w_ff1, w_ff2):
    params = (g_mix, w_in, nsa_q_norm, nsa_k_norm, cmp_pe, cmp_w1, cmp_w2, gdn_conv_w, gdn_A_log,
              gdn_dt_bias, gdn_o_norm, mem_q_norm, w_br_nsa, w_br_gdn, w_br_mem, w_out, g_ffn, w_ff1, w_ff2)
    bp, sp = x_prompt.shape[0], x_prompt.shape[1]
    dtp = x_prompt.dtype
    empty = jnp.zeros((bp, 0, 2, NSA_KV_HEADS, NSA_HD), dtp)
    p_mem_kv = memory_kv(mem_prompt, g_mem, w_mem_kv, mem_k_norm)
    y_prompt, p_cmp, p_sel, p_win, p_state, p_conv = hybrid_layer(
        x_prompt, 0, empty, empty, empty, min(WINDOW, sp),
        jnp.zeros((bp, GDN_HEADS, GDN_DK, GDN_DV), dtp), jnp.zeros((bp, CONV_W - 1, CONV_DIM), dtp),
        p_mem_kv, params)
    past_cmp = gather_pages(cache_cmp_kv, page_table)
    past_sel = gather_pages(cache_sel_kv, page_table)
    y_sample, s_cmp, s_sel, s_win, s_state, s_conv = hybrid_layer(
        x_sample, past_cmp.shape[1], past_cmp, past_sel, cache_win_kv, cache_win_kv.shape[1],
        state_gdn, state_gdn_conv, cache_mem_kv, params)
    return (y_prompt, y_sample, p_cmp, p_sel, p_win, p_state, p_conv, p_mem_kv, s_cmp, s_sel, s_win, s_state, s_conv)
```

```python
import functools

import jax
import jax.numpy as jnp
from jax import lax
from jax.experimental import pallas as pl
from jax.experimental.pallas import tpu as pltpu

F32 = jnp.float32
BF16 = jnp.bfloat16
HI = lax.Precision.HIGHEST

D_MODEL = 1024
PAGE_SIZE = 128
NSA_HEADS = 8
NSA_KV_HEADS = 2
NSA_HD = 64
NSA_GROUP = NSA_HEADS // NSA_KV_HEADS
NSA_SCALE = NSA_HD ** -0.5
CMP_BLOCK = 32
CMP_STRIDE = 16
CMP_HID = 128
SEL_BLOCK = 64
SEL_TOP = 16
WINDOW = 512
FORCE_SCORE = 1e9
GDN_HEADS = 4
GDN_DK = 128
GDN_DV = 128
CONV_W = 4
GDN_CHUNK = 64
MEM_HEADS = 4
MEM_HD = 128
MEM_SCALE = MEM_HD ** -0.5
D_FF = 4 * D_MODEL
ROPE_THETA = 10000.0
EPS = 1e-6

Q_NSA = NSA_HEADS * NSA_HD
KV_ROW = 2 * NSA_KV_HEADS * NSA_HD
GDN_QK = GDN_HEADS * GDN_DK
GDN_V = GDN_HEADS * GDN_DV
CONV_DIM = 2 * GDN_QK + GDN_V
MEM_Q = MEM_HEADS * MEM_HD
N_BRANCH = 3
IN_SPLITS = (Q_NSA, 3 * KV_ROW, 3 * NSA_HEADS, CONV_DIM, GDN_HEADS, GDN_HEADS, GDN_V, MEM_Q, N_BRANCH * D_MODEL)

LANES = 128
SMALL_W = LANES
G_NSA_OFF, B_OFF, A_OFF = 0, 3 * NSA_HEADS, 3 * NSA_HEADS + GDN_HEADS
TOK_Q, TOK_QKV, TOK_Z, TOK_QM, TOK_GBR, TOK_SMALL = 0, 512, 2048, 2560, 3072, 6144
TOK_W = TOK_SMALL + SMALL_W
V7X_VMEM_LIMIT = 56 * 1024 * 1024


def _cparams(sem):
    return pltpu.CompilerParams(dimension_semantics=sem, vmem_limit_bytes=V7X_VMEM_LIMIT)


def _dot(a, b):
    return jnp.dot(a.astype(BF16), b.astype(BF16), preferred_element_type=F32)


def _dot_nt(a, b):
    return lax.dot_general(a.astype(BF16), b.astype(BF16), (((1,), (1,)), ((), ())), preferred_element_type=F32)


def _dot_hi(a, b):
    return jnp.dot(a, b, precision=HI, preferred_element_type=F32)


def _rms(x, axis=-1):
    return x * lax.rsqrt(jnp.mean(x * x, axis=axis, keepdims=True) + EPS)


def _const_spec(shape):
    nd = len(shape)
    return pl.BlockSpec(shape, lambda *_: (0,) * nd, pipeline_mode=pl.Buffered(1))


def _inproj_kernel(x_ref, gmix_ref, wtok_ref, wkvt_ref, qn_ref, kn_ref, cq_ref, sq_ref, ck_ref, sk_ref, mqn_ref,
                   q_ref, qkv_ref, z_ref, qm_ref, gate_ref, small_ref, kc_ref, ks_ref, kw_ref):
    x = x_ref[0]
    ub = (_rms(x) * gmix_ref[...]).astype(BF16)
    tm = x.shape[0]
    lane = lax.broadcasted_iota(jnp.int32, (tm, LANES), 1)
    lo = lane < NSA_HD
    first_half = (lane % NSA_HD) < (NSA_HD // 2)
    for i in range(Q_NSA // LANES):
        qs = jnp.dot(ub, wtok_ref[:, TOK_Q + i * LANES:TOK_Q + (i + 1) * LANES], preferred_element_type=F32)
        sq = qs * qs
        ss_lo = jnp.sum(jnp.where(lo, sq, 0.0), axis=-1, keepdims=True)
        ss_hi = jnp.sum(jnp.where(lo, 0.0, sq), axis=-1, keepdims=True)
        r = jnp.where(lo, lax.rsqrt(ss_lo / NSA_HD + EPS), lax.rsqrt(ss_hi / NSA_HD + EPS))
        qs = qs * r * qn_ref[...]
        rot = jnp.where(first_half, -pltpu.roll(qs, LANES - NSA_HD // 2, axis=1), pltpu.roll(qs, NSA_HD // 2, axis=1))
        qs = qs * cq_ref[...] + rot * sq_ref[...]
        q_ref[0, :, i * LANES:(i + 1) * LANES] = (qs * NSA_SCALE).astype(BF16)
    qkv_ref[0] = jnp.dot(ub, wtok_ref[:, TOK_QKV:TOK_Z], preferred_element_type=F32)
    z_ref[0] = jnp.dot(ub, wtok_ref[:, TOK_Z:TOK_QM], preferred_element_type=F32)
    for h in range(MEM_HEADS):
        qm = jnp.dot(ub, wtok_ref[:, TOK_QM + h * MEM_HD:TOK_QM + (h + 1) * MEM_HD], preferred_element_type=F32)
        qm_ref[0, :, h * MEM_HD:(h + 1) * MEM_HD] = (_rms(qm) * mqn_ref[...]).astype(BF16)
    for i in range(N_BRANCH):
        gb = jnp.dot(ub, wtok_ref[:, TOK_GBR + i * D_MODEL:TOK_GBR + (i + 1) * D_MODEL], preferred_element_type=F32)
        gate_ref[0, :, i * D_MODEL:(i + 1) * D_MODEL] = jax.nn.sigmoid(gb)
    small_ref[0] = jnp.dot(ub, wtok_ref[:, TOK_SMALL:TOK_W], preferred_element_type=F32)
    kvt = lax.dot_general(wkvt_ref[...], ub, (((1,), (1,)), ((), ())), preferred_element_type=F32)
    cos = ck_ref[...]
    sin = sk_ref[...]
    half = NSA_HD // 2
    for c, out_ref in enumerate((kc_ref, ks_ref, kw_ref)):
        base = c * KV_ROW
        for g in range(NSA_KV_HEADS):
            kh = kvt[base + g * NSA_HD:base + (g + 1) * NSA_HD, :]
            kh = _rms(kh, axis=0) * kn_ref[c]
            x1, x2 = kh[:half], kh[half:]
            out_ref[0, g * NSA_HD:g * NSA_HD + half, :] = x1 * cos - x2 * sin
            out_ref[0, g * NSA_HD + half:(g + 1) * NSA_HD, :] = x2 * cos + x1 * sin
        out_ref[0, KV_ROW // 2:, :] = kvt[base + KV_ROW // 2:base + KV_ROW, :]


def _inproj(x, pos, wts, tm):
    B, T, _ = x.shape
    half = NSA_HD // 2
    inv = ROPE_THETA ** (-jnp.arange(half, dtype=F32) / half)
    ang = pos.astype(F32)[:, None] * inv[None, :]
    cos, sin = jnp.cos(ang), jnp.sin(ang)
    cq, sq = jnp.tile(cos, (1, LANES // half)), jnp.tile(sin, (1, LANES // half))
    ck, sk = cos.T, sin.T
    tok = lambda w: pl.BlockSpec((1, tm, w), lambda b, t: (b, t, 0))
    kvs = pl.BlockSpec((1, KV_ROW, tm), lambda b, t: (b, 0, t))
    outs = pl.pallas_call(
        _inproj_kernel,
        grid=(B, T // tm),
        in_specs=[tok(D_MODEL), _const_spec((1, D_MODEL)), _const_spec((D_MODEL, TOK_W)), _const_spec((3 * KV_ROW, D_MODEL)),
                  _const_spec((1, LANES)), _const_spec((3, NSA_HD, 1)),
                  pl.BlockSpec((tm, LANES), lambda b, t: (t, 0)), pl.BlockSpec((tm, LANES), lambda b, t: (t, 0)),
                  pl.BlockSpec((half, tm), lambda b, t: (0, t)), pl.BlockSpec((half, tm), lambda b, t: (0, t)),
                  _const_spec((1, MEM_HD))],
        out_specs=[tok(Q_NSA), tok(CONV_DIM), tok(GDN_V), tok(MEM_Q), tok(N_BRANCH * D_MODEL), tok(SMALL_W), kvs, kvs, kvs],
        out_shape=[jax.ShapeDtypeStruct((B, T, Q_NSA), BF16), jax.ShapeDtypeStruct((B, T, CONV_DIM), F32),
                   jax.ShapeDtypeStruct((B, T, GDN_V), F32), jax.ShapeDtypeStruct((B, T, MEM_Q), BF16),
                   jax.ShapeDtypeStruct((B, T, N_BRANCH * D_MODEL), F32), jax.ShapeDtypeStruct((B, T, SMALL_W), F32)]
                  + [jax.ShapeDtypeStruct((B, KV_ROW, T), F32)] * 3,
        compiler_params=_cparams(("parallel", "parallel")),
        name="inproj",
    )(x, wts["g_mix"], wts["w_tok"], wts["w_kvt"], wts["qn"], wts["kn"], cq, sq, ck, sk, wts["mqn"])
    return outs


def _memkv_kernel(m_ref, g_ref, w_ref, kn_ref, o_ref):
    u = _rms(m_ref[...]) * g_ref[...]
    kv = _dot(u, w_ref[...])
    for h in range(MEM_HEADS):
        sl = slice(h * MEM_HD, (h + 1) * MEM_HD)
        o_ref[:, sl] = _rms(kv[:, sl]) * kn_ref[...]
    o_ref[:, MEM_Q:] = kv[:, MEM_Q:]


def _memkv(mem2d, wts, tm=512):
    n = mem2d.shape[0]
    tm = min(tm, n)
    return pl.pallas_call(
        _memkv_kernel, grid=(n // tm,),
        in_specs=[pl.BlockSpec((tm, D_MODEL), lambda i: (i, 0)), _const_spec((1, D_MODEL)),
                  _const_spec((D_MODEL, 2 * MEM_Q)), _const_spec((1, MEM_HD))],
        out_specs=pl.BlockSpec((tm, 2 * MEM_Q), lambda i: (i, 0)),
        out_shape=jax.ShapeDtypeStruct((n, 2 * MEM_Q), F32),
        compiler_params=_cparams(("parallel",)), name="memkv",
    )(mem2d, wts["g_mem"], wts["w_mem_kv"], wts["mkn"])


def _compress(get_page, n_pages, xs_ref, hs_ref, wbd_ref, pe_ref, w1_ref, w2_ref):
    n_chunks = n_pages * PAGE_SIZE // CMP_STRIDE
    for p in range(n_pages):
        xt = get_page(p).T
        xs_ref[0, p * PAGE_SIZE:(p + 1) * PAGE_SIZE, :] = xt[:, :LANES]
        xs_ref[1, p * PAGE_SIZE:(p + 1) * PAGE_SIZE, :] = xt[:, LANES:]
    hs_ref[n_chunks:, :] = jnp.zeros((8, CMP_HID), F32)
    parts = []
    for kv in range(2):
        h = jnp.zeros((n_chunks, 2 * NSA_KV_HEADS * CMP_HID), F32)
        for s2 in range(CMP_STRIDE // 2):
            lhs = jnp.concatenate([xs_ref[kv, pl.ds(2 * s2, n_chunks, stride=CMP_STRIDE), :],
                                   xs_ref[kv, pl.ds(2 * s2 + 1, n_chunks, stride=CMP_STRIDE), :]], axis=1)
            h = h + jnp.dot(lhs.astype(BF16), wbd_ref[kv, s2 * 2 * LANES:(s2 + 1) * 2 * LANES, :], preferred_element_type=F32)
        pe_h = jnp.dot(pe_ref[kv].astype(BF16), w1_ref[kv], preferred_element_type=F32)[0:1]
        for g in range(NSA_KV_HEADS):
            h0 = h[:, g * 2 * CMP_HID:g * 2 * CMP_HID + CMP_HID]
            hs_ref[0:n_chunks, :] = h[:, g * 2 * CMP_HID + CMP_HID:(g + 1) * 2 * CMP_HID]
            hh = h0 + hs_ref[pl.ds(1, n_chunks), :] + pe_h
            parts.append(jnp.dot(jax.nn.gelu(hh).astype(BF16), w2_ref[kv], preferred_element_type=F32))
    return jnp.concatenate(parts, axis=1)


def _masked_softmax(s, allow):
    s = jnp.where(allow, s, -1e30)
    e = jnp.exp(s - jnp.max(s, axis=-1, keepdims=True))
    p = e / jnp.sum(e, axis=-1, keepdims=True)
    return jnp.where(allow, p, 0.0)


def _cmp_probs(qg, kc, tpos, n_cmp):
    n_chunks = kc.shape[0]
    s = _dot_nt(qg, kc)
    i = lax.broadcasted_iota(jnp.int32, (1, n_chunks), 1)
    allow = jnp.where(i < n_cmp, i * CMP_STRIDE + CMP_BLOCK - 1, jnp.int32(2 ** 30)) <= tpos
    return _masked_softmax(s, allow)


def _select_blocks(psum, tpos, n_cmp, n_sel, ns_pad):
    n_chunks = psum.shape[1]
    ci = lax.broadcasted_iota(jnp.int32, (n_chunks, ns_pad), 0)
    sj = lax.broadcasted_iota(jnp.int32, (n_chunks, ns_pad), 1)
    hit = (ci * CMP_STRIDE < (sj + 1) * SEL_BLOCK) & (ci * CMP_STRIDE + CMP_BLOCK > sj * SEL_BLOCK) & (ci < n_cmp) & (sj < n_sel)
    imp = _dot(psum, jnp.where(hit, 1.0, 0.0))
    jj = lax.broadcasted_iota(jnp.int32, (1, ns_pad), 1)
    imp = jnp.where((jj * SEL_BLOCK <= tpos) & (jj < n_sel), imp, -jnp.inf)
    imp = jnp.where((jj == 0) | (jj == tpos // SEL_BLOCK), FORCE_SCORE, imp)
    cnt = jnp.zeros(imp.shape, F32)
    for j in range(n_sel):
        col = imp[:, j:j + 1]
        later = jnp.where(jj > j, 1.0, 0.0)
        cnt = cnt + jnp.where(col > imp, 1.0, jnp.where(col == imp, later, 0.0))
    return jnp.where(cnt < min(SEL_TOP, n_sel), 1.0, 0.0)


def _head_rows(q, g):
    return jnp.concatenate([q[:, (g * NSA_GROUP + j) * NSA_HD:(g * NSA_GROUP + j + 1) * NSA_HD] for j in range(NSA_GROUP)], axis=0)


def _gate_merge(gsig, o_cmp, o_sel, o_win, tq):
    cols = []
    for h in range(NSA_HEADS):
        g, j = divmod(h, NSA_GROUP)
        rows = slice(j * tq, (j + 1) * tq)
        cols.append(gsig[:, 3 * h:3 * h + 1] * o_cmp[g][rows] + gsig[:, 3 * h + 1:3 * h + 2] * o_sel[g][rows]
                    + gsig[:, 3 * h + 2:3 * h + 3] * o_win[g][rows])
    return jnp.concatenate(cols, axis=1)


def _pcompress_kernel(kc_ref, wbd_ref, pe_ref, w1_ref, w2_ref, o_ref, xs_ref, hs_ref):
    n_pages = kc_ref.shape[2] // PAGE_SIZE
    o_ref[0] = _compress(lambda p: kc_ref[0, :, p * PAGE_SIZE:(p + 1) * PAGE_SIZE], n_pages, xs_ref, hs_ref,
                         wbd_ref, pe_ref, w1_ref, w2_ref)


def _pcompress(kct, wts):
    B, _, T = kct.shape
    n_chunks = T // CMP_STRIDE
    return pl.pallas_call(
        _pcompress_kernel, grid=(B,),
        in_specs=[pl.BlockSpec((1, KV_ROW, T), lambda b: (b, 0, 0)), _const_spec(wts["wbd"].shape), _const_spec(wts["pe"].shape),
                  _const_spec(wts["w1"].shape), _const_spec(wts["w2"].shape)],
        out_specs=pl.BlockSpec((1, n_chunks, KV_ROW), lambda b: (b, 0, 0)),
        out_shape=jax.ShapeDtypeStruct((B, n_chunks, KV_ROW), F32),
        scratch_shapes=[pltpu.VMEM((2, T, LANES), F32), pltpu.VMEM((n_chunks + 8, CMP_HID), F32)],
        compiler_params=_cparams(("parallel",)), name="prompt_compress",
    )(kct, wts["wbd"], wts["pe"], wts["w1"], wts["w2"])


def _pattn_kernel(q_ref, ckv_ref, ks_ref, kw_ref, small_ref, o_ref, *, T, tq):
    n_chunks = T // CMP_STRIDE
    n_cmp = n_chunks - CMP_BLOCK // CMP_STRIDE + 1
    n_sel = -(-T // SEL_BLOCK)
    ns_pad = -(-n_sel // LANES) * LANES
    span = WINDOW + tq
    q0 = pl.program_id(1) * tq
    q = q_ref[0]
    ckv = ckv_ref[0]
    tcol = q0 + lax.broadcasted_iota(jnp.int32, (tq, 1), 0)
    trow = jnp.concatenate([tcol] * NSA_GROUP, axis=0)
    keyi = lax.broadcasted_iota(jnp.int32, (1, T), 1)
    es = lax.broadcasted_iota(jnp.int32, (ns_pad, T), 0)
    ek = lax.broadcasted_iota(jnp.int32, (ns_pad, T), 1)
    expand = jnp.where(ek // SEL_BLOCK == es, 1.0, 0.0).astype(BF16)
    wstart = pl.multiple_of(jnp.maximum(q0 - WINDOW, 0), LANES)
    keyw = wstart + lax.broadcasted_iota(jnp.int32, (1, span), 1)
    allow_w = (keyw <= trow) & (keyw > trow - WINDOW)
    o_cmp, o_sel, o_win = [], [], []
    for g in range(NSA_KV_HEADS):
        qg = _head_rows(q, g)
        kc = ckv[:, g * NSA_HD:(g + 1) * NSA_HD]
        vc = ckv[:, KV_ROW // 2 + g * NSA_HD:KV_ROW // 2 + (g + 1) * NSA_HD]
        p = _cmp_probs(qg, kc, trow, n_cmp)
        o_cmp.append(_dot(p, vc))
        psum = p[0:tq]
        for j in range(1, NSA_GROUP):
            psum = psum + p[j * tq:(j + 1) * tq]
        sel = _select_blocks(psum, tcol, n_cmp, n_sel, ns_pad)
        keysel = jnp.dot(sel.astype(BF16), expand, preferred_element_type=F32)
        allow = jnp.where(keyi <= tcol, keysel, 0.0) > 0.5
        allow = jnp.concatenate([allow] * NSA_GROUP, axis=0)
        ksg = ks_ref[0, g * NSA_HD:(g + 1) * NSA_HD, :]
        vsg = ks_ref[0, KV_ROW // 2 + g * NSA_HD:KV_ROW // 2 + (g + 1) * NSA_HD, :]
        ps = _masked_softmax(_dot(qg, ksg), allow)
        o_sel.append(_dot_nt(ps, vsg))
        kwg = kw_ref[0, g * NSA_HD:(g + 1) * NSA_HD, pl.ds(wstart, span)]
        vwg = kw_ref[0, KV_ROW // 2 + g * NSA_HD:KV_ROW // 2 + (g + 1) * NSA_HD, pl.ds(wstart, span)]
        pw = _masked_softmax(_dot(qg, kwg), allow_w)
        o_win.append(_dot_nt(pw, vwg))
    gsig = jax.nn.sigmoid(small_ref[0][:, G_NSA_OFF:G_NSA_OFF + 3 * NSA_HEADS])
    o_ref[0] = _gate_merge(gsig, o_cmp, o_sel, o_win, tq)


def _pattn(q, ckv, kst, kwt, small, tq=128):
    B, T, _ = q.shape
    n_chunks = T // CMP_STRIDE
    full = lambda b, t: (b, 0, 0)
    return pl.pallas_call(
        functools.partial(_pattn_kernel, T=T, tq=tq), grid=(B, T // tq),
        in_specs=[pl.BlockSpec((1, tq, Q_NSA), lambda b, t: (b, t, 0)), pl.BlockSpec((1, n_chunks, KV_ROW), full),
                  pl.BlockSpec((1, KV_ROW, T), full), pl.BlockSpec((1, KV_ROW, T), full),
                  pl.BlockSpec((1, tq, SMALL_W), lambda b, t: (b, t, 0))],
        out_specs=pl.BlockSpec((1, tq, Q_NSA), lambda b, t: (b, t, 0)),
        out_shape=jax.ShapeDtypeStruct((B, T, Q_NSA), F32),
        compiler_params=_cparams(("parallel", "parallel")), name="prompt_nsa_attn",
    )(q, ckv, kst, kwt, small)


def _s1_kernel(pt_ref, *refs, n_pages, q_pos):
    pages = refs[:n_pages]
    q_ref, wbd_ref, pe_ref, w1_ref, w2_ref, ocmp_ref, idx_ref, xs_ref, hs_ref = refs[n_pages:]
    n_chunks = n_pages * PAGE_SIZE // CMP_STRIDE
    n_cmp = n_chunks - CMP_BLOCK // CMP_STRIDE + 1
    n_sel = -(-(q_pos + 1) // SEL_BLOCK)
    ns_pad = -(-n_sel // LANES) * LANES
    ckv = _compress(lambda p: pages[p][0], n_pages, xs_ref, hs_ref, wbd_ref, pe_ref, w1_ref, w2_ref)
    qrow = q_ref[0].astype(F32)
    q8 = jnp.concatenate([qrow[:, h * NSA_HD:(h + 1) * NSA_HD] for h in range(NSA_HEADS)], axis=0)
    row = lax.broadcasted_iota(jnp.int32, (NSA_HEADS, 1), 0)
    tpos = jnp.full((NSA_HEADS, 1), q_pos, jnp.int32)
    o_all = jnp.zeros((NSA_HEADS, NSA_HD), F32)
    psum = jnp.zeros((NSA_HEADS, n_chunks), F32)
    for g in range(NSA_KV_HEADS):
        kc = ckv[:, g * NSA_HD:(g + 1) * NSA_HD]
        vc = ckv[:, KV_ROW // 2 + g * NSA_HD:KV_ROW // 2 + (g + 1) * NSA_HD]
        p = _cmp_probs(q8, kc, tpos, n_cmp)
        mine = (row // NSA_GROUP) == g
        o_all = jnp.where(mine, _dot(p, vc), o_all)
        pg = jnp.sum(jnp.where(mine, p, 0.0), axis=0, keepdims=True)
        psum = jnp.where(row == g, pg, psum)
    sel = _select_blocks(psum, tpos, n_cmp, n_sel, ns_pad)
    a = lax.broadcasted_iota(jnp.int32, (ns_pad, ns_pad), 0)
    b = lax.broadcasted_iota(jnp.int32, (ns_pad, ns_pad), 1)
    before = jnp.dot(sel.astype(BF16), jnp.where(a < b, 1.0, 0.0).astype(BF16), preferred_element_type=F32)
    jj = lax.broadcasted_iota(jnp.int32, (1, ns_pad), 1).astype(F32)
    lane = lax.broadcasted_iota(jnp.int32, (1, LANES), 1)
    idx = jnp.zeros((NSA_HEADS, LANES), F32)
    for k in range(min(SEL_TOP, n_sel)):
        ik = jnp.sum(jnp.where((sel > 0.5) & (before == k), jj, 0.0), axis=1, keepdims=True)
        idx = jnp.where(lane == k, ik, idx)
    idx_ref[0] = idx.astype(jnp.int32)
    ocmp_ref[0] = jnp.concatenate([o_all, jnp.zeros((NSA_HEADS, LANES - NSA_HD), F32)], axis=1)


def _s1(pool_t, page_table, q3, wts, q_pos):
    B, n_pages = page_table.shape
    n_chunks = n_pages * PAGE_SIZE // CMP_STRIDE

    def page_spec(j):
        return pl.BlockSpec((1, KV_ROW, PAGE_SIZE), lambda b, pt: (pt[b, j], 0, 0))

    cst = lambda shape: pl.BlockSpec(shape, lambda b, pt: (0,) * len(shape), pipeline_mode=pl.Buffered(1))
    grid_spec = pltpu.PrefetchScalarGridSpec(
        num_scalar_prefetch=1, grid=(B,),
        in_specs=[page_spec(j) for j in range(n_pages)]
        + [pl.BlockSpec((1, 1, Q_NSA), lambda b, pt: (b, 0, 0)), cst(wts["wbd"].shape), cst(wts["pe"].shape),
           cst(wts["w1"].shape), cst(wts["w2"].shape)],
        out_specs=[pl.BlockSpec((1, NSA_HEADS, LANES), lambda b, pt: (b, 0, 0)),
                   pl.BlockSpec((1, NSA_HEADS, LANES), lambda b, pt: (b, 0, 0))],
        scratch_shapes=[pltpu.VMEM((2, n_pages * PAGE_SIZE, LANES), F32), pltpu.VMEM((n_chunks + 8, CMP_HID), F32)],
    )
    return pl.pallas_call(
        functools.partial(_s1_kernel, n_pages=n_pages, q_pos=q_pos), grid_spec=grid_spec,
        out_shape=[jax.ShapeDtypeStruct((B, NSA_HEADS, LANES), F32), jax.ShapeDtypeStruct((B, NSA_HEADS, LANES), jnp.int32)],
        compiler_params=_cparams(("parallel",)), name="sample_compress_select",
    )(page_table, *([pool_t] * n_pages), q3, wts["wbd"], wts["pe"], wts["w1"], wts["w2"])


def _s2_kernel(phys_ref, meta_ref, *refs, n_top, win_len):
    n_blk = NSA_KV_HEADS * n_top
    pages = refs[:n_blk]
    q_ref, ocmp_ref, small_ref, nsel_ref, nwin_ref, nwint_ref, cwin_ref, o_ref, swin_ref = refs[n_blk:]
    b = pl.program_id(0)
    qrow = q_ref[0].astype(F32)
    q8 = jnp.concatenate([qrow[:, h * NSA_HD:(h + 1) * NSA_HD] for h in range(NSA_HEADS)], axis=0)
    row = lax.broadcasted_iota(jnp.int32, (NSA_HEADS, 1), 0)
    lane = lax.broadcasted_iota(jnp.int32, (1, PAGE_SIZE), 1)
    nsel = nsel_ref[0]
    nwin = nwin_ref[0]
    cwin = cwin_ref[0]
    r = lax.broadcasted_iota(jnp.int32, (1, win_len), 1)
    allow_w = (r > win_len - WINDOW) & (r <= win_len)
    o_sel = jnp.zeros((NSA_HEADS, NSA_HD), F32)
    o_win = jnp.zeros((NSA_HEADS, NSA_HD), F32)
    for g in range(NSA_KV_HEADS):
        mine = (row // NSA_GROUP) == g
        ksl = slice(g * NSA_HD, (g + 1) * NSA_HD)
        vsl = slice(KV_ROW // 2 + g * NSA_HD, KV_ROW // 2 + (g + 1) * NSA_HD)
        kts, vts, masks = [], [], []
        has_new = jnp.zeros((1, 1), F32)
        for k in range(n_top):
            m = meta_ref[b, g * n_top + k]
            kts.append(pages[g * n_top + k][0, ksl, :])
            vts.append(pages[g * n_top + k][0, vsl, :])
            masks.append((lane // SEL_BLOCK) == m)
            has_new = has_new + jnp.where(m == 2, 1.0, 0.0)
        kt = jnp.concatenate(kts, axis=1)
        vt = jnp.concatenate(vts, axis=1)
        allow = jnp.concatenate(masks, axis=1)
        s = jnp.where(allow, _dot(q8, kt), -1e30)
        s_new = jnp.where(has_new > 0.5, jnp.sum(q8 * nsel[:, ksl], axis=1, keepdims=True), -1e30)
        mx = jnp.maximum(jnp.max(s, axis=1, keepdims=True), s_new)
        e = jnp.where(allow, jnp.exp(s - mx), 0.0)
        e_new = jnp.where(has_new > 0.5, jnp.exp(s_new - mx), 0.0)
        den = jnp.sum(e, axis=1, keepdims=True) + e_new
        og = (_dot_nt(e, vt) + e_new * nsel[:, vsl]) / den
        o_sel = jnp.where(mine, og, o_sel)
        s = jnp.where(allow_w, _dot(q8, cwin[ksl, :]), -1e30)
        s_new = jnp.sum(q8 * nwin[:, ksl], axis=1, keepdims=True)
        mx = jnp.maximum(jnp.max(s, axis=1, keepdims=True), s_new)
        e = jnp.where(allow_w, jnp.exp(s - mx), 0.0)
        e_new = jnp.exp(s_new - mx)
        den = jnp.sum(e, axis=1, keepdims=True) + e_new
        og = (_dot_nt(e, cwin[vsl, :]) + e_new * nwin[:, vsl]) / den
        o_win = jnp.where(mine, og, o_win)
    o_cmp = ocmp_ref[0][:, :NSA_HD]
    gsig = jax.nn.sigmoid(small_ref[0][:, G_NSA_OFF:G_NSA_OFF + 3 * NSA_HEADS])
    cols = []
    for h in range(NSA_HEADS):
        cols.append(gsig[:, 3 * h:3 * h + 1] * o_cmp[h:h + 1] + gsig[:, 3 * h + 1:3 * h + 2] * o_sel[h:h + 1]
                    + gsig[:, 3 * h + 2:3 * h + 3] * o_win[h:h + 1])
    o_ref[0] = jnp.concatenate(cols, axis=1)
    blane = lax.broadcasted_iota(jnp.int32, (1, nwint_ref.shape[1]), 1)
    newcol = jnp.sum(jnp.where(blane == b, nwint_ref[...], 0.0), axis=1, keepdims=True)
    swin_ref[0] = jnp.concatenate([cwin[:, 1:], newcol], axis=1)


def _s2(pool_t, phys, meta, q3, ocmp, small3, nsel3, nwin3, nwint, cwint, n_top):
    B = q3.shape[0]
    win_len = cwint.shape[2]
    n_blk = NSA_KV_HEADS * n_top

    def page_spec(j):
        return pl.BlockSpec((1, KV_ROW, PAGE_SIZE), lambda b, ph, me: (ph[b, j], 0, 0))

    per_b = lambda shape: pl.BlockSpec(shape, lambda b, ph, me: (b,) + (0,) * (len(shape) - 1))
    grid_spec = pltpu.PrefetchScalarGridSpec(
        num_scalar_prefetch=2, grid=(B,),
        in_specs=[page_spec(j) for j in range(n_blk)]
        + [per_b((1, 1, Q_NSA)), per_b((1, NSA_HEADS, LANES)), per_b((1, 1, SMALL_W)), per_b((1, 1, KV_ROW)), per_b((1, 1, KV_ROW)),
           pl.BlockSpec(nwint.shape, lambda b, ph, me: (0, 0)), per_b((1, KV_ROW, win_len))],
        out_specs=[per_b((1, 1, Q_NSA)), per_b((1, KV_ROW, win_len))],
    )
    return pl.pallas_call(
        functools.partial(_s2_kernel, n_top=n_top, win_len=win_len), grid_spec=grid_spec,
        out_shape=[jax.ShapeDtypeStruct((B, 1, Q_NSA), F32), jax.ShapeDtypeStruct((B, KV_ROW, win_len), F32)],
        compiler_params=_cparams(("parallel",)), name="sample_sel_win_attn",
    )(phys, meta, *([pool_t] * n_blk), q3, ocmp, small3, nsel3, nwin3, nwint, cwint)


def _gdn_gates(b_raw, a_raw, alog, dtb):
    beta = jax.nn.sigmoid(b_raw)
    g = -jnp.exp(alog) * jax.nn.softplus(a_raw + dtb)
    return beta, g


def _l2n(x):
    return x * lax.rsqrt(jnp.sum(x * x, axis=-1, keepdims=True) + EPS)


def _gdn_prompt_kernel(qkv_ref, z_ref, small_ref, smallt_ref, cw_ref, alog_ref, dtb_ref, alogt_ref, dtbt_ref, on_ref,
                       o_ref, sfin_ref, conv_ref, s_ref, xx_ref):
    ci = pl.program_id(1)
    tc = qkv_ref.shape[1]
    C = GDN_CHUNK

    @pl.when(ci == 0)
    def _():
        s_ref[...] = jnp.zeros(s_ref.shape, F32)
        xx_ref[0:8, :] = jnp.zeros((8, CONV_DIM), F32)

    xx_ref[8:8 + tc, :] = qkv_ref[0]
    y = xx_ref[pl.ds(8 - (CONV_W - 1), tc), :] * cw_ref[0:1, :]
    for j in range(1, CONV_W):
        y = y + xx_ref[pl.ds(8 - (CONV_W - 1) + j, tc), :] * cw_ref[j:j + 1, :]
    c = jax.nn.silu(y)
    tail = xx_ref[tc:tc + 8, :]
    conv_ref[0] = tail
    xx_ref[0:8, :] = tail
    small = small_ref[0]
    beta, gcol = _gdn_gates(small[:, B_OFF:B_OFF + GDN_HEADS], small[:, A_OFF:A_OFF + GDN_HEADS], alog_ref[...], dtb_ref[...])
    _, grow = _gdn_gates(smallt_ref[0][0:GDN_HEADS], smallt_ref[0][GDN_HEADS:2 * GDN_HEADS], alogt_ref[...], dtbt_ref[...])
    z = z_ref[0]
    ii = lax.broadcasted_iota(jnp.int32, (C, C), 0)
    jj = lax.broadcasted_iota(jnp.int32, (C, C), 1)
    tril = ii >= jj
    strict = ii > jj
    eye = jnp.where(ii == jj, 1.0, 0.0)
    for cc in range(tc // C):
        rs = slice(cc * C, (cc + 1) * C)
        for h in range(GDN_HEADS):
            qh = _l2n(c[rs, h * GDN_DK:(h + 1) * GDN_DK]) * (GDN_DK ** -0.5)
            kh = _l2n(c[rs, GDN_QK + h * GDN_DK:GDN_QK + (h + 1) * GDN_DK])
            vh = c[rs, 2 * GDN_QK + h * GDN_DV:2 * GDN_QK + (h + 1) * GDN_DV]
            bcol = beta[rs, h:h + 1]
            g_c = gcol[rs, h:h + 1]
            g_r = grow[h:h + 1, rs]
            dec_c = jnp.sum(jnp.where(tril, g_r, 0.0), axis=1, keepdims=True)
            dec_r = jnp.sum(jnp.where(ii <= jj, g_c, 0.0), axis=0, keepdims=True)
            lmask = jnp.where(tril, jnp.exp(jnp.where(tril, dec_c - dec_r, 0.0)), 0.0)
            kb = kh * bcol
            a_mat = jnp.where(strict, _dot_nt(kb, kh) * lmask, 0.0)
            m = -a_mat
            tinv = eye + m
            for _ in range(5):
                m = _dot_hi(m, m)
                tinv = tinv + _dot_hi(tinv, m)
            u = _dot_hi(tinv, vh * bcol)
            w = _dot_hi(tinv, kb * jnp.exp(dec_c))
            attn = _dot_nt(qh, kh) * lmask
            s_old = s_ref[h]
            v_new = u - _dot(w, s_old)
            o = _dot(qh * jnp.exp(dec_c), s_old) + _dot(attn, v_new)
            dlast = dec_c[C - 1:C, :]
            kdt = (kh * jnp.exp(dlast - dec_c)).T
            s_ref[h] = s_old * jnp.exp(dlast) + _dot(kdt, v_new)
            o = _rms(o) * on_ref[...] * jax.nn.silu(z[rs, h * GDN_DV:(h + 1) * GDN_DV])
            o_ref[0, rs, h * GDN_DV:(h + 1) * GDN_DV] = o

    @pl.when(ci == pl.num_programs(1) - 1)
    def _():
        sfin_ref[0] = s_ref[...]


def _gdn_prompt(qkv, z, small, wts, tc=128):
    B, T, _ = qkv.shape
    smallt = jnp.transpose(small[:, :, B_OFF:B_OFF + 2 * GDN_HEADS], (0, 2, 1))
    tokb = lambda w: pl.BlockSpec((1, tc, w), lambda b, c: (b, c, 0))
    return pl.pallas_call(
        _gdn_prompt_kernel, grid=(B, T // tc),
        in_specs=[tokb(CONV_DIM), tokb(GDN_V), tokb(SMALL_W), pl.BlockSpec((1, 2 * GDN_HEADS, tc), lambda b, c: (b, 0, c)),
                  _const_spec((CONV_W, CONV_DIM)), _const_spec((1, GDN_HEADS)), _const_spec((1, GDN_HEADS)),
                  _const_spec((GDN_HEADS, 1)), _const_spec((GDN_HEADS, 1)), _const_spec((1, GDN_DV))],
        out_specs=[tokb(GDN_V), pl.BlockSpec((1, GDN_HEADS, GDN_DK, GDN_DV), lambda b, c: (b, 0, 0, 0)),
                   pl.BlockSpec((1, 8, CONV_DIM), lambda b, c: (b, 0, 0))],
        out_shape=[jax.ShapeDtypeStruct((B, T, GDN_V), F32), jax.ShapeDtypeStruct((B, GDN_HEADS, GDN_DK, GDN_DV), F32),
                   jax.ShapeDtypeStruct((B, 8, CONV_DIM), F32)],
        scratch_shapes=[pltpu.VMEM((GDN_HEADS, GDN_DK, GDN_DV), F32), pltpu.VMEM((tc + 8, CONV_DIM), F32)],
        compiler_params=_cparams(("parallel", "arbitrary")), name="gdn_prompt",
    )(qkv, z, small, smallt, wts["conv_w"], wts["alog"], wts["dtb"], wts["alog_t"], wts["dtb_t"], wts["onorm"])


def _gdn_sample_kernel(xx_ref, z_ref, small_ref, cw_ref, alog_ref, dtb_ref, on_ref, s_ref,
                       o_ref, sout_ref, qt_ref, kt_ref, wt_ref, u_ref, sc_ref):
    b = pl.program_id(0)
    nb = z_ref.shape[0]

    @pl.when(b == 0)
    def _():
        y = xx_ref[0] * cw_ref[0:1, :]
        for j in range(1, CONV_W):
            y = y + xx_ref[j] * cw_ref[j:j + 1, :]
        c = jax.nn.silu(y)
        small = small_ref[...]
        beta, g = _gdn_gates(small[:, B_OFF:B_OFF + GDN_HEADS], small[:, A_OFF:A_OFF + GDN_HEADS], alog_ref[...], dtb_ref[...])
        a = jnp.exp(g)
        attn = []
        for h in range(GDN_HEADS):
            qh = _l2n(c[:, h * GDN_DK:(h + 1) * GDN_DK]) * (GDN_DK ** -0.5)
            kh = _l2n(c[:, GDN_QK + h * GDN_DK:GDN_QK + (h + 1) * GDN_DK])
            vh = c[:, 2 * GDN_QK + h * GDN_DV:2 * GDN_QK + (h + 1) * GDN_DV]
            bh, ah = beta[:, h:h + 1], a[:, h:h + 1]
            qt_ref[h] = (qh * ah).T
            kt_ref[h] = kh.T
            wt_ref[h] = (kh * bh * ah).T
            u_ref[h] = vh * bh
            attn.append(jnp.sum(qh * kh, axis=1, keepdims=True))
        sc_ref[...] = jnp.concatenate([a] + attn + [jnp.zeros((nb, LANES - 2 * GDN_HEADS), F32)], axis=1)

    lane = lax.broadcasted_iota(jnp.int32, (1, nb), 1)
    pick = lane == b
    sc = sc_ref[pl.ds(b, 1), :]
    zrow = z_ref[pl.ds(b, 1), :]
    for h in range(GDN_HEADS):
        wcol = jnp.sum(jnp.where(pick, wt_ref[h], 0.0), axis=1, keepdims=True)
        qcol = jnp.sum(jnp.where(pick, qt_ref[h], 0.0), axis=1, keepdims=True)
        kcol = jnp.sum(jnp.where(pick, kt_ref[h], 0.0), axis=1, keepdims=True)
        s_old = s_ref[0, h]
        v_new = u_ref[h, pl.ds(b, 1), :] - jnp.sum(s_old * wcol, axis=0, keepdims=True)
        o = jnp.sum(s_old * qcol, axis=0, keepdims=True) + sc[:, GDN_HEADS + h:GDN_HEADS + h + 1] * v_new
        sout_ref[0, h] = s_old * sc[:, h:h + 1] + kcol * v_new
        o = _rms(o) * on_ref[...] * jax.nn.silu(zrow[:, h * GDN_DV:(h + 1) * GDN_DV])
        o_ref[0, :, h * GDN_DV:(h + 1) * GDN_DV] = o


def _gdn_sample(xx4, z2, small2, state, wts):
    nb = z2.shape[0]
    cst = lambda shape: pl.BlockSpec(shape, lambda b: (0,) * len(shape))
    sspec = pl.BlockSpec((1, GDN_HEADS, GDN_DK, GDN_DV), lambda b: (b, 0, 0, 0))
    return pl.pallas_call(
        _gdn_sample_kernel, grid=(nb,),
        in_specs=[cst(xx4.shape), cst(z2.shape), cst(small2.shape), cst((CONV_W, CONV_DIM)), cst((1, GDN_HEADS)),
                  cst((1, GDN_HEADS)), cst((1, GDN_DV)), sspec],
        out_specs=[pl.BlockSpec((1, 1, GDN_V), lambda b: (b, 0, 0)), sspec],
        out_shape=[jax.ShapeDtypeStruct((nb, 1, GDN_V), F32), jax.ShapeDtypeStruct(state.shape, F32)],
        scratch_shapes=[pltpu.VMEM((GDN_HEADS, GDN_DK, nb), F32)] * 3
        + [pltpu.VMEM((GDN_HEADS, nb, GDN_DV), F32), pltpu.VMEM((nb, LANES), F32)],
        compiler_params=_cparams(("arbitrary",)), name="gdn_sample",
    )(xx4, z2, small2, wts["conv_w"], wts["alog"], wts["dtb"], wts["onorm"], state)


def _mem_prompt_kernel(qm_ref, kv_ref, o_ref):
    qm = qm_ref[0]
    kv = kv_ref[0]
    for h in range(MEM_HEADS):
        sl = slice(h * MEM_HD, (h + 1) * MEM_HD)
        s = _dot_nt(qm[:, sl], kv[:, sl]) * MEM_SCALE
        e = jnp.exp(s - jnp.max(s, axis=-1, keepdims=True))
        p = e / jnp.sum(e, axis=-1, keepdims=True)
        o_ref[0, :, sl] = _dot(p, kv[:, MEM_Q + h * MEM_HD:MEM_Q + (h + 1) * MEM_HD])


def _mem_prompt(qm, mkv, tq=256):
    B, T, _ = qm.shape
    M = mkv.shape[1]
    return pl.pallas_call(
        _mem_prompt_kernel, grid=(B, T // tq),
        in_specs=[pl.BlockSpec((1, tq, MEM_Q), lambda b, t: (b, t, 0)), pl.BlockSpec((1, M, 2 * MEM_Q), lambda b, t: (b, 0, 0))],
        out_specs=pl.BlockSpec((1, tq, MEM_Q), lambda b, t: (b, t, 0)),
        out_shape=jax.ShapeDtypeStruct((B, T, MEM_Q), F32),
        compiler_params=_cparams(("parallel", "parallel")), name="mem_attn_prompt",
    )(qm, mkv)


def _mem_sample_kernel(qm_ref, kv_ref, o_ref):
    q = qm_ref[0].astype(F32)
    for h in range(MEM_HEADS):
        sl = slice(h * MEM_HD, (h + 1) * MEM_HD)
        k = kv_ref[0, :, 0, h, :]
        v = kv_ref[0, :, 1, h, :]
        s = jnp.sum(k * q[:, sl], axis=1, keepdims=True) * MEM_SCALE
        e = jnp.exp(s - jnp.max(s, axis=0, keepdims=True))
        p = e / jnp.sum(e, axis=0, keepdims=True)
        o_ref[0, :, sl] = jnp.sum(v * p, axis=0, keepdims=True)


def _mem_sample(qm3, cache):
    B, M = cache.shape[0], cache.shape[1]
    return pl.pallas_call(
        _mem_sample_kernel, grid=(B,),
        in_specs=[pl.BlockSpec((1, 1, MEM_Q), lambda b: (b, 0, 0)),
                  pl.BlockSpec((1, M, 2, MEM_HEADS, MEM_HD), lambda b: (b, 0, 0, 0, 0))],
        out_specs=pl.BlockSpec((1, 1, MEM_Q), lambda b: (b, 0, 0)),
        out_shape=jax.ShapeDtypeStruct((B, 1, MEM_Q), F32),
        compiler_params=_cparams(("parallel",)), name="mem_attn_sample",
    )(qm3, cache)


def _merge_ffn_kernel(on_ref, og_ref, om_ref, gate_ref, x_ref, wbn_ref, wbg_ref, wbm_ref, wout_ref, gffn_ref, w1_ref, w2_ref, y_ref):
    merged = (gate_ref[:, 0:D_MODEL] * _dot(on_ref[...], wbn_ref[...])
              + gate_ref[:, D_MODEL:2 * D_MODEL] * _dot(og_ref[...], wbg_ref[...])
              + gate_ref[:, 2 * D_MODEL:3 * D_MODEL] * _dot(om_ref[...], wbm_ref[...]))
    h = x_ref[...] + _dot(merged, wout_ref[...])
    f = _dot(_rms(h) * gffn_ref[...], w1_ref[...])
    f = jnp.square(jnp.maximum(f, 0.0))
    y_ref[...] = h + _dot(f, w2_ref[...])


def _merge_ffn(o_nsa, o_gdn, o_mem, gates, x2d, wts, tm):
    n = x2d.shape[0]
    tok = lambda w: pl.BlockSpec((tm, w), lambda i: (i, 0))
    return pl.pallas_call(
        _merge_ffn_kernel, grid=(n // tm,),
        in_specs=[tok(Q_NSA), tok(GDN_V), tok(MEM_Q), tok(N_BRANCH * D_MODEL), tok(D_MODEL),
                  _const_spec((Q_NSA, D_MODEL)), _const_spec((GDN_V, D_MODEL)), _const_spec((MEM_Q, D_MODEL)),
                  _const_spec((D_MODEL, D_MODEL)), _const_spec((1, D_MODEL)), _const_spec((D_MODEL, D_FF)), _const_spec((D_FF, D_MODEL))],
        out_specs=tok(D_MODEL),
        out_shape=jax.ShapeDtypeStruct((n, D_MODEL), F32),
        compiler_params=_cparams(("parallel",)), name="merge_ffn",
    )(o_nsa, o_gdn, o_mem, gates, x2d, wts["w_br_nsa"], wts["w_br_gdn"], wts["w_br_mem"], wts["w_out"], wts["g_ffn"],
      wts["w_ff1"], wts["w_ff2"])


def _prep_weights(g_mix, w_in, nsa_q_norm, nsa_k_norm, cmp_pe, cmp_w1, cmp_w2, gdn_conv_w, gdn_A_log, gdn_dt_bias, gdn_o_norm,
                  g_mem, w_mem_kv, mem_q_norm, mem_k_norm, w_br_nsa, w_br_gdn, w_br_mem, w_out, g_ffn, w_ff1, w_ff2):
    offs = [0]
    for s in IN_SPLITS:
        offs.append(offs[-1] + s)
    wt = w_in.T
    seg = lambda i: wt[offs[i]:offs[i + 1]]
    small = jnp.concatenate([seg(2), seg(4), seg(5)], axis=0)
    small = jnp.pad(small, ((0, SMALL_W - small.shape[0]), (0, 0)))
    w_tok = jnp.concatenate([seg(0), seg(3), seg(6), seg(7), seg(8), small], axis=0).T.astype(BF16)
    R = CMP_BLOCK // CMP_STRIDE
    w1r = cmp_w1.reshape(2, R, CMP_STRIDE, NSA_HD, CMP_HID)
    wbd = jnp.einsum("krsdf,gh->ksgdhrf", w1r, jnp.eye(NSA_KV_HEADS, dtype=F32))
    wbd = wbd.reshape(2, CMP_STRIDE * NSA_KV_HEADS * NSA_HD, NSA_KV_HEADS * R * CMP_HID).astype(BF16)
    row = lambda v: v.reshape(1, -1)
    return dict(
        g_mix=row(g_mix), w_tok=w_tok, w_kvt=seg(1).astype(BF16), qn=row(jnp.tile(nsa_q_norm, LANES // NSA_HD)),
        kn=nsa_k_norm.reshape(3, NSA_HD, 1), mqn=row(mem_q_norm), mkn=row(mem_k_norm), g_mem=row(g_mem),
        w_mem_kv=w_mem_kv.astype(BF16), wbd=wbd,
        pe=jnp.broadcast_to(cmp_pe.reshape(2, 1, CMP_BLOCK * NSA_HD), (2, 8, CMP_BLOCK * NSA_HD)),
        w1=cmp_w1.astype(BF16), w2=cmp_w2.astype(BF16), conv_w=gdn_conv_w,
        alog=row(gdn_A_log), dtb=row(gdn_dt_bias), alog_t=gdn_A_log.reshape(-1, 1), dtb_t=gdn_dt_bias.reshape(-1, 1),
        onorm=row(gdn_o_norm), w_br_nsa=w_br_nsa.astype(BF16), w_br_gdn=w_br_gdn.astype(BF16), w_br_mem=w_br_mem.astype(BF16),
        w_out=w_out.astype(BF16), g_ffn=row(g_ffn), w_ff1=w_ff1.astype(BF16), w_ff2=w_ff2.astype(BF16))


def _rows_5d(kt):
    B, _, N = kt.shape
    return jnp.transpose(kt.reshape(B, 2, NSA_KV_HEADS, NSA_HD, N), (0, 4, 1, 2, 3))


def _feature_major(rows):
    B, N = rows.shape[0], rows.shape[1]
    return jnp.transpose(rows, (0, 2, 3, 4, 1)).reshape(B, KV_ROW, N)


def kernel(x_prompt, x_sample, cache_cmp_kv, cache_sel_kv, cache_win_kv, state_gdn, state_gdn_conv, cache_mem_kv, page_table, mem_prompt, g_mix, w_in, nsa_q_norm, nsa_k_norm, cmp_pe, cmp_w1, cmp_w2, gdn_conv_w, gdn_A_log, gdn_dt_bias, gdn_o_norm, g_mem, w_mem_kv, mem_q_norm, mem_k_norm, w_br_nsa, w_br_gdn, w_br_mem, w_out, g_ffn, w_ff1, w_ff2):
    wts = _prep_weights(g_mix, w_in, nsa_q_norm, nsa_k_norm, cmp_pe, cmp_w1, cmp_w2, gdn_conv_w, gdn_A_log, gdn_dt_bias,
                        gdn_o_norm, g_mem, w_mem_kv, mem_q_norm, mem_k_norm, w_br_nsa, w_br_gdn, w_br_mem, w_out, g_ffn, w_ff1, w_ff2)
    B, T, D = x_prompt.shape
    nb = x_sample.shape[0]
    assert x_sample.shape[1] == 1 and T % 256 == 0 and T >= WINDOW + 128
    n_pages = page_table.shape[1]
    past = n_pages * PAGE_SIZE

    q, qkv, z, qm, gates, small, kct, kst, kwt = _inproj(x_prompt, jnp.arange(T, dtype=jnp.int32), wts, tm=256)
    mkv = _memkv(mem_prompt.reshape(-1, D), wts).reshape(B, -1, 2 * MEM_Q)
    ckv = _pcompress(kct, wts)
    o_nsa = _pattn(q, ckv, kst, kwt, small)
    o_gdn, p_state, conv_tail = _gdn_prompt(qkv, z, small, wts)
    o_mem = _mem_prompt(qm, mkv)
    y_prompt = _merge_ffn(o_nsa.reshape(-1, Q_NSA), o_gdn.reshape(-1, GDN_V), o_mem.reshape(-1, MEM_Q),
                          gates.reshape(-1, N_BRANCH * D), x_prompt.reshape(-1, D), wts, tm=256).reshape(B, T, D)
    p_cmp, p_sel = _rows_5d(kct), _rows_5d(kst)
    p_win = _rows_5d(kwt[:, :, T - min(WINDOW, T):])
    p_conv = conv_tail[:, 8 - (CONV_W - 1):, :]
    p_mem_kv = mkv.reshape(B, -1, 2, MEM_HEADS, MEM_HD)

    xs = x_sample.reshape(1, nb, D)
    sq, sqkv, sz, sqm, sgates, ssmall, skct, skst, skwt = _inproj(xs, jnp.full((nb,), past, jnp.int32), wts, tm=nb)
    pool_cmp = _feature_major(cache_cmp_kv)
    pool_sel = _feature_major(cache_sel_kv)
    q3 = sq.reshape(nb, 1, Q_NSA)
    ocmp, idx = _s1(pool_cmp, page_table, q3, wts, q_pos=past)
    n_sel = -(-(past + 1) // SEL_BLOCK)
    n_top = min(SEL_TOP, n_sel)
    idx = idx[:, :NSA_KV_HEADS, :n_top].reshape(nb, NSA_KV_HEADS * n_top)
    blk_per_page = PAGE_SIZE // SEL_BLOCK
    is_new = idx * SEL_BLOCK >= past
    page = jnp.take_along_axis(page_table, jnp.where(is_new, 0, idx // blk_per_page), axis=1)
    meta = jnp.where(is_new, blk_per_page, idx % blk_per_page).astype(jnp.int32)
    tokrow = lambda kt: jnp.transpose(kt[0], (1, 0)).reshape(nb, 1, KV_ROW)
    cwint = _feature_major(cache_win_kv)
    o_nsa_s, swin = _s2(pool_sel, page.astype(jnp.int32), meta, q3, ocmp, ssmall.reshape(nb, 1, SMALL_W), tokrow(skst), tokrow(skwt),
                        skwt[0], cwint, n_top)
    xx4 = jnp.concatenate([jnp.transpose(state_gdn_conv, (1, 0, 2)), sqkv], axis=0)
    o_gdn_s, s_state = _gdn_sample(xx4, sz[0], ssmall[0], state_gdn, wts)
    o_mem_s = _mem_sample(sqm.reshape(nb, 1, MEM_Q), cache_mem_kv)
    y_sample = _merge_ffn(o_nsa_s.reshape(nb, Q_NSA), o_gdn_s.reshape(nb, GDN_V), o_mem_s.reshape(nb, MEM_Q), sgates[0], x_sample.reshape(nb, D),
                          wts, tm=nb).reshape(nb, 1, D)
    s_cmp = jnp.transpose(skct[0], (1, 0)).reshape(nb, 1, 2, NSA_KV_HEADS, NSA_HD)
    s_sel = jnp.transpose(skst[0], (1, 0)).reshape(nb, 1, 2, NSA_KV_HEADS, NSA_HD)
    s_win = _rows_5d(swin)
    s_conv = jnp.transpose(xx4[1:], (1, 0, 2))
    return (y_prompt, y_sample, p_cmp, p_sel, p_win, p_state, p_conv, p_mem_kv, s_cmp, s_sel, s_win, s_state, s_conv)
```

```python
import functools

import jax
import jax.numpy as jnp
from jax import lax
from jax.experimental import pallas as pl
from jax.experimental.pallas import tpu as pltpu

F32 = jnp.float32
BF16 = jnp.bfloat16
HI = lax.Precision.HIGHEST

D_MODEL = 1024
PAGE_SIZE = 128
NSA_HEADS = 8
NSA_KV_HEADS = 2
NSA_HD = 64
NSA_GROUP = NSA_HEADS // NSA_KV_HEADS
NSA_SCALE = NSA_HD ** -0.5
CMP_BLOCK = 32
CMP_STRIDE = 16
CMP_HID = 128
SEL_BLOCK = 64
SEL_TOP = 16
WINDOW = 512
FORCE_SCORE = 1e9
GDN_HEADS = 4
GDN_DK = 128
GDN_DV = 128
CONV_W = 4
GDN_CHUNK = 64
MEM_HEADS = 4
MEM_HD = 128
MEM_SCALE = MEM_HD ** -0.5
D_FF = 4 * D_MODEL
ROPE_THETA = 10000.0
EPS = 1e-6

Q_NSA = NSA_HEADS * NSA_HD
KV_ROW = 2 * NSA_KV_HEADS * NSA_HD
GDN_QK = GDN_HEADS * GDN_DK
GDN_V = GDN_HEADS * GDN_DV
CONV_DIM = 2 * GDN_QK + GDN_V
MEM_Q = MEM_HEADS * MEM_HD
N_BRANCH = 3
IN_SPLITS = (Q_NSA, 3 * KV_ROW, 3 * NSA_HEADS, CONV_DIM, GDN_HEADS, GDN_HEADS, GDN_V, MEM_Q, N_BRANCH * D_MODEL)

LANES = 128
SMALL_W = LANES
G_NSA_OFF, B_OFF, A_OFF = 0, 3 * NSA_HEADS, 3 * NSA_HEADS + GDN_HEADS
TOK_Q, TOK_QKV, TOK_Z, TOK_QM, TOK_GBR, TOK_SMALL, TOK_KTOK = 0, 512, 2048, 2560, 3072, 6144, 6272
TOK_W = TOK_KTOK + 2 * LANES
FT_KV, FT_Q, FT_G = 0, 3 * KV_ROW, 3 * KV_ROW + Q_NSA
G_ROWS = 32
FT_W = FT_G + G_ROWS
V7X_VMEM_LIMIT = 56 * 1024 * 1024


def _cparams(sem):
    return pltpu.CompilerParams(dimension_semantics=sem, vmem_limit_bytes=V7X_VMEM_LIMIT)


def _dot(a, b):
    return jnp.dot(a.astype(BF16), b.astype(BF16), preferred_element_type=F32)


def _dot_nt(a, b):
    return lax.dot_general(a.astype(BF16), b.astype(BF16), (((1,), (1,)), ((), ())), preferred_element_type=F32)


def _dot_hi(a, b):
    return jnp.dot(a, b, precision=HI, preferred_element_type=F32)


def _split_bf16(a):
    hi = a.astype(BF16)
    return hi, (a - hi.astype(F32)).astype(BF16)


def _dot3(a, b):
    ah, al = _split_bf16(a)
    bh, bl = _split_bf16(b)
    d = lambda x, y: jnp.dot(x, y, preferred_element_type=F32)
    return d(ah, bh) + (d(ah, bl) + d(al, bh))


def _rms(x, axis=-1):
    return x * lax.rsqrt(jnp.mean(x * x, axis=axis, keepdims=True) + EPS)


def _const_spec(shape):
    nd = len(shape)
    return pl.BlockSpec(shape, lambda *_: (0,) * nd, pipeline_mode=pl.Buffered(1))


def _inproj_kernel(x_ref, gmix_ref, wtok_ref, wft_ref, qn_ref, qnc_ref, kn_ref, knr_ref, cq_ref, sq_ref, ck_ref, sk_ref, mqn_ref,
                   q_ref, qkv_ref, z_ref, qm_ref, gate_ref, small_ref, kstok_ref, kwtok_ref,
                   kc_ref, ks_ref, kw_ref, qt_ref, gt_ref):
    x = x_ref[0]
    ub = (_rms(x) * gmix_ref[...]).astype(BF16)
    tm = x.shape[0]
    lane = lax.broadcasted_iota(jnp.int32, (tm, LANES), 1)
    lo = lane < NSA_HD
    first_half = (lane % NSA_HD) < (NSA_HD // 2)

    def norm_rope_slab(col0, gain):
        qs = jnp.dot(ub, wtok_ref[:, col0:col0 + LANES], preferred_element_type=F32)
        sq = qs * qs
        ss_lo = jnp.sum(jnp.where(lo, sq, 0.0), axis=-1, keepdims=True)
        ss_hi = jnp.sum(jnp.where(lo, 0.0, sq), axis=-1, keepdims=True)
        r = jnp.where(lo, lax.rsqrt(ss_lo / NSA_HD + EPS), lax.rsqrt(ss_hi / NSA_HD + EPS))
        qs = qs * r * gain
        rot = jnp.where(first_half, -pltpu.roll(qs, LANES - NSA_HD // 2, axis=1), pltpu.roll(qs, NSA_HD // 2, axis=1))
        return qs * cq_ref[...] + rot * sq_ref[...]

    for i in range(Q_NSA // LANES):
        q_ref[0, :, i * LANES:(i + 1) * LANES] = (norm_rope_slab(TOK_Q + i * LANES, qn_ref[...]) * NSA_SCALE).astype(BF16)
    kstok_ref[0] = norm_rope_slab(TOK_KTOK, knr_ref[1:2, :]).astype(BF16)
    kwtok_ref[0] = norm_rope_slab(TOK_KTOK + LANES, knr_ref[2:3, :]).astype(BF16)
    qkv_ref[0] = jnp.dot(ub, wtok_ref[:, TOK_QKV:TOK_Z], preferred_element_type=F32)
    z_ref[0] = jnp.dot(ub, wtok_ref[:, TOK_Z:TOK_QM], preferred_element_type=F32)
    for h in range(MEM_HEADS):
        qm = jnp.dot(ub, wtok_ref[:, TOK_QM + h * MEM_HD:TOK_QM + (h + 1) * MEM_HD], preferred_element_type=F32)
        qm_ref[0, :, h * MEM_HD:(h + 1) * MEM_HD] = (_rms(qm) * mqn_ref[...]).astype(BF16)
    for i in range(N_BRANCH):
        gb = jnp.dot(ub, wtok_ref[:, TOK_GBR + i * D_MODEL:TOK_GBR + (i + 1) * D_MODEL], preferred_element_type=F32)
        gate_ref[0, :, i * D_MODEL:(i + 1) * D_MODEL] = jax.nn.sigmoid(gb)
    small_ref[0] = jnp.dot(ub, wtok_ref[:, TOK_SMALL:TOK_SMALL + SMALL_W], preferred_element_type=F32)
    ft = lax.dot_general(wft_ref[...], ub, (((1,), (1,)), ((), ())), preferred_element_type=F32)
    cos = ck_ref[...]
    sin = sk_ref[...]
    half = NSA_HD // 2

    def norm_rope_rows(row0, gain_col):
        kh = _rms(ft[row0:row0 + NSA_HD, :], axis=0) * gain_col
        x1, x2 = kh[:half], kh[half:]
        return x1 * cos - x2 * sin, x2 * cos + x1 * sin

    for c, out_ref in enumerate((kc_ref, ks_ref, kw_ref)):
        base = FT_KV + c * KV_ROW
        for g in range(NSA_KV_HEADS):
            r1, r2 = norm_rope_rows(base + g * NSA_HD, kn_ref[c])
            out_ref[0, g * NSA_HD:g * NSA_HD + half, :] = r1
            out_ref[0, g * NSA_HD + half:(g + 1) * NSA_HD, :] = r2
        out_ref[0, KV_ROW // 2:, :] = ft[base + KV_ROW // 2:base + KV_ROW, :]
    for h in range(NSA_HEADS):
        r1, r2 = norm_rope_rows(FT_Q + h * NSA_HD, qnc_ref[...])
        qt_ref[0, h * NSA_HD:(h + 1) * NSA_HD, :] = (jnp.concatenate([r1, r2], axis=0) * NSA_SCALE).astype(BF16)
    gt_ref[0] = jax.nn.sigmoid(ft[FT_G:FT_G + G_ROWS, :])


def _inproj(x, pos, wts, tm):
    B, T, _ = x.shape
    half = NSA_HD // 2
    inv = ROPE_THETA ** (-jnp.arange(half, dtype=F32) / half)
    ang = pos.astype(F32)[:, None] * inv[None, :]
    cos, sin = jnp.cos(ang), jnp.sin(ang)
    cq, sq = jnp.tile(cos, (1, LANES // half)), jnp.tile(sin, (1, LANES // half))
    ck, sk = cos.T, sin.T
    tok = lambda w: pl.BlockSpec((1, tm, w), lambda b, t: (b, t, 0))
    ftm = lambda r: pl.BlockSpec((1, r, tm), lambda b, t: (b, 0, t))
    tok_shape = lambda w, dt: jax.ShapeDtypeStruct((B, T, w), dt)
    ft_shape = lambda r, dt: jax.ShapeDtypeStruct((B, r, T), dt)
    names = ("q", "qkv", "z", "qm", "gates", "small", "ks_tok", "kw_tok", "kct", "kst", "kwt", "qt", "gt")
    outs = pl.pallas_call(
        _inproj_kernel,
        grid=(B, T // tm),
        in_specs=[tok(D_MODEL), _const_spec((1, D_MODEL)), _const_spec((D_MODEL, TOK_W)), _const_spec((FT_W, D_MODEL)),
                  _const_spec((1, LANES)), _const_spec((NSA_HD, 1)), _const_spec((3, NSA_HD, 1)), _const_spec((3, LANES)),
                  pl.BlockSpec((tm, LANES), lambda b, t: (t, 0)), pl.BlockSpec((tm, LANES), lambda b, t: (t, 0)),
                  pl.BlockSpec((half, tm), lambda b, t: (0, t)), pl.BlockSpec((half, tm), lambda b, t: (0, t)),
                  _const_spec((1, MEM_HD))],
        out_specs=[tok(Q_NSA), tok(CONV_DIM), tok(GDN_V), tok(MEM_Q), tok(N_BRANCH * D_MODEL), tok(SMALL_W), tok(LANES), tok(LANES),
                   ftm(KV_ROW), ftm(KV_ROW), ftm(KV_ROW), ftm(Q_NSA), ftm(G_ROWS)],
        out_shape=[tok_shape(Q_NSA, BF16), tok_shape(CONV_DIM, F32), tok_shape(GDN_V, F32), tok_shape(MEM_Q, BF16),
                   tok_shape(N_BRANCH * D_MODEL, F32), tok_shape(SMALL_W, F32), tok_shape(LANES, BF16), tok_shape(LANES, BF16),
                   ft_shape(KV_ROW, F32), ft_shape(KV_ROW, F32), ft_shape(KV_ROW, F32), ft_shape(Q_NSA, BF16), ft_shape(G_ROWS, F32)],
        compiler_params=_cparams(("parallel", "parallel")),
        name="inproj",
    )(x, wts["g_mix"], wts["w_tok"], wts["w_ft"], wts["qn"], wts["qn_col"], wts["kn"], wts["kn_row"], cq, sq, ck, sk, wts["mqn"])
    return dict(zip(names, outs))


def _memkv_kernel(m_ref, g_ref, w_ref, kn_ref, o_ref):
    u = _rms(m_ref[...]) * g_ref[...]
    kv = _dot(u, w_ref[...])
    for h in range(MEM_HEADS):
        sl = slice(h * MEM_HD, (h + 1) * MEM_HD)
        o_ref[:, sl] = _rms(kv[:, sl]) * kn_ref[...]
    o_ref[:, MEM_Q:] = kv[:, MEM_Q:]


def _memkv(mem2d, wts, tm=512):
    n = mem2d.shape[0]
    tm = min(tm, n)
    return pl.pallas_call(
        _memkv_kernel, grid=(n // tm,),
        in_specs=[pl.BlockSpec((tm, D_MODEL), lambda i: (i, 0)), _const_spec((1, D_MODEL)),
                  _const_spec((D_MODEL, 2 * MEM_Q)), _const_spec((1, MEM_HD))],
        out_specs=pl.BlockSpec((tm, 2 * MEM_Q), lambda i: (i, 0)),
        out_shape=jax.ShapeDtypeStruct((n, 2 * MEM_Q), F32),
        compiler_params=_cparams(("parallel",)), name="memkv",
    )(mem2d, wts["g_mem"], wts["w_mem_kv"], wts["mkn"])


def _compress(get_page, n_pages, xs_ref, hs_ref, wbd_ref, pe_ref, w1_ref, w2_ref):
    n_chunks = n_pages * PAGE_SIZE // CMP_STRIDE
    for p in range(n_pages):
        xt = get_page(p).T
        xs_ref[0, p * PAGE_SIZE:(p + 1) * PAGE_SIZE, :] = xt[:, :LANES]
        xs_ref[1, p * PAGE_SIZE:(p + 1) * PAGE_SIZE, :] = xt[:, LANES:]
    hs_ref[n_chunks:, :] = jnp.zeros((8, CMP_HID), F32)
    parts = []
    for kv in range(2):
        h = jnp.zeros((n_chunks, 2 * NSA_KV_HEADS * CMP_HID), F32)
        for s2 in range(CMP_STRIDE // 2):
            lhs = jnp.concatenate([xs_ref[kv, pl.ds(2 * s2, n_chunks, stride=CMP_STRIDE), :],
                                   xs_ref[kv, pl.ds(2 * s2 + 1, n_chunks, stride=CMP_STRIDE), :]], axis=1)
            h = h + jnp.dot(lhs.astype(BF16), wbd_ref[kv, s2 * 2 * LANES:(s2 + 1) * 2 * LANES, :], preferred_element_type=F32)
        pe_h = jnp.dot(pe_ref[kv].astype(BF16), w1_ref[kv], preferred_element_type=F32)[0:1]
        for g in range(NSA_KV_HEADS):
            h0 = h[:, g * 2 * CMP_HID:g * 2 * CMP_HID + CMP_HID]
            hs_ref[0:n_chunks, :] = h[:, g * 2 * CMP_HID + CMP_HID:(g + 1) * 2 * CMP_HID]
            hh = h0 + hs_ref[pl.ds(1, n_chunks), :] + pe_h
            parts.append(jnp.dot(jax.nn.gelu(hh).astype(BF16), w2_ref[kv], preferred_element_type=F32))
    return jnp.concatenate(parts, axis=1)


def _masked_softmax(s, allow):
    s = jnp.where(allow, s, -1e30)
    e = jnp.exp(s - jnp.max(s, axis=-1, keepdims=True))
    p = e / jnp.sum(e, axis=-1, keepdims=True)
    return jnp.where(allow, p, 0.0)


def _cmp_probs(qg, kc, tpos, n_cmp):
    n_chunks = kc.shape[0]
    s = _dot_nt(qg, kc)
    i = lax.broadcasted_iota(jnp.int32, (1, n_chunks), 1)
    allow = jnp.where(i < n_cmp, i * CMP_STRIDE + CMP_BLOCK - 1, jnp.int32(2 ** 30)) <= tpos
    return _masked_softmax(s, allow)


def _select_blocks(psum, tpos, n_cmp, n_sel, ns_pad):
    n_chunks = psum.shape[1]
    ci = lax.broadcasted_iota(jnp.int32, (n_chunks, ns_pad), 0)
    sj = lax.broadcasted_iota(jnp.int32, (n_chunks, ns_pad), 1)
    hit = (ci * CMP_STRIDE < (sj + 1) * SEL_BLOCK) & (ci * CMP_STRIDE + CMP_BLOCK > sj * SEL_BLOCK) & (ci < n_cmp) & (sj < n_sel)
    imp = _dot(psum, jnp.where(hit, 1.0, 0.0))
    jj = lax.broadcasted_iota(jnp.int32, (1, ns_pad), 1)
    imp = jnp.where((jj * SEL_BLOCK <= tpos) & (jj < n_sel), imp, -jnp.inf)
    imp = jnp.where((jj == 0) | (jj == tpos // SEL_BLOCK), FORCE_SCORE, imp)
    cnt = jnp.zeros(imp.shape, F32)
    for j in range(n_sel):
        col = imp[:, j:j + 1]
        later = jnp.where(jj > j, 1.0, 0.0)
        cnt = cnt + jnp.where(col > imp, 1.0, jnp.where(col == imp, later, 0.0))
    return jnp.where(cnt < min(SEL_TOP, n_sel), 1.0, 0.0)


def _head_rows(q, g):
    return jnp.concatenate([q[:, (g * NSA_GROUP + j) * NSA_HD:(g * NSA_GROUP + j + 1) * NSA_HD] for j in range(NSA_GROUP)], axis=0)


def _gate_merge(gsig, o_cmp, o_sel, o_win, tq):
    cols = []
    for h in range(NSA_HEADS):
        g, j = divmod(h, NSA_GROUP)
        rows = slice(j * tq, (j + 1) * tq)
        cols.append(gsig[:, 3 * h:3 * h + 1] * o_cmp[g][rows] + gsig[:, 3 * h + 1:3 * h + 2] * o_sel[g][rows]
                    + gsig[:, 3 * h + 2:3 * h + 3] * o_win[g][rows])
    return jnp.concatenate(cols, axis=1)


def _pcompress_kernel(kc_ref, wbd_ref, pe_ref, w1_ref, w2_ref, o_ref, xs_ref, hs_ref):
    n_pages = kc_ref.shape[2] // PAGE_SIZE
    o_ref[0] = _compress(lambda p: kc_ref[0, :, p * PAGE_SIZE:(p + 1) * PAGE_SIZE], n_pages, xs_ref, hs_ref,
                         wbd_ref, pe_ref, w1_ref, w2_ref)


def _pcompress(kct, wts):
    B, _, T = kct.shape
    n_chunks = T // CMP_STRIDE
    return pl.pallas_call(
        _pcompress_kernel, grid=(B,),
        in_specs=[pl.BlockSpec((1, KV_ROW, T), lambda b: (b, 0, 0)), _const_spec(wts["wbd"].shape), _const_spec(wts["pe"].shape),
                  _const_spec(wts["w1"].shape), _const_spec(wts["w2"].shape)],
        out_specs=pl.BlockSpec((1, n_chunks, KV_ROW), lambda b: (b, 0, 0)),
        out_shape=jax.ShapeDtypeStruct((B, n_chunks, KV_ROW), F32),
        scratch_shapes=[pltpu.VMEM((2, T, LANES), F32), pltpu.VMEM((n_chunks + 8, CMP_HID), F32)],
        compiler_params=_cparams(("parallel",)), name="prompt_compress",
    )(kct, wts["wbd"], wts["pe"], wts["w1"], wts["w2"])


def _softmax_pv(q, kt, vt, bias):
    s = _dot(q, kt) + bias
    e = jnp.exp(s - jnp.max(s, axis=-1, keepdims=True))
    return _dot_nt(e, vt) / jnp.sum(e, axis=-1, keepdims=True)


def _pattn_kernel(q_ref, ckv_ref, ks_ref, kw_ref, small_ref, o_ref, osel_ref, *, T, tq, n_ext):
    n_chunks = T // CMP_STRIDE
    n_cmp = n_chunks - CMP_BLOCK // CMP_STRIDE + 1
    n_sel = -(-T // SEL_BLOCK)
    ns_pad = -(-n_sel // LANES) * LANES
    span = WINDOW + tq
    qi = pl.program_id(1)
    q0 = qi * tq
    q = q_ref[0]
    ckv = ckv_ref[0]
    tcol = q0 + lax.broadcasted_iota(jnp.int32, (tq, 1), 0)
    wstart = pl.multiple_of(jnp.maximum(q0 - WINDOW, 0), LANES)
    keyw = wstart + lax.broadcasted_iota(jnp.int32, (1, span), 1)
    bias_w = jnp.where(keyw <= tcol, jnp.where(keyw > tcol - WINDOW, 0.0, -1e30), -1e30)
    gsig = jax.nn.sigmoid(small_ref[0][:, G_NSA_OFF:G_NSA_OFF + 3 * NSA_HEADS])
    tiles_per_ext = T // tq // n_ext
    o_cmp, o_win = [], []
    for g in range(NSA_KV_HEADS):
        ksl = slice(g * NSA_HD, (g + 1) * NSA_HD)
        vsl = slice(KV_ROW // 2 + g * NSA_HD, KV_ROW // 2 + (g + 1) * NSA_HD)
        qs = [q[:, (g * NSA_GROUP + j) * NSA_HD:(g * NSA_GROUP + j + 1) * NSA_HD] for j in range(NSA_GROUP)]
        psum = None
        for j in range(NSA_GROUP):
            p = _cmp_probs(qs[j], ckv[:, ksl], tcol, n_cmp)
            o_cmp.append(_dot(p, ckv[:, vsl]))
            psum = p if psum is None else psum + p
        sel = _select_blocks(psum, tcol, n_cmp, n_sel, ns_pad).astype(BF16)
        kwg = kw_ref[0, ksl, pl.ds(wstart, span)]
        vwg = kw_ref[0, vsl, pl.ds(wstart, span)]
        for j in range(NSA_GROUP):
            o_win.append(_softmax_pv(qs[j], kwg, vwg, bias_w))
        for v in range(n_ext):
            kext = (v + 1) * (T // n_ext)

            @pl.when((qi >= v * tiles_per_ext) & (qi < (v + 1) * tiles_per_ext))
            def _():
                es = lax.broadcasted_iota(jnp.int32, (ns_pad, kext), 0)
                ek = lax.broadcasted_iota(jnp.int32, (ns_pad, kext), 1)
                expand = jnp.where(ek // SEL_BLOCK == es, 1.0, 0.0).astype(BF16)
                keysel = jnp.dot(sel, expand, preferred_element_type=F32)
                keyi = lax.broadcasted_iota(jnp.int32, (1, kext), 1)
                bias = jnp.where(keyi <= tcol, jnp.where(keysel > 0.5, 0.0, -1e30), -1e30)
                for j in range(NSA_GROUP):
                    h = g * NSA_GROUP + j
                    osel_ref[:, h * NSA_HD:(h + 1) * NSA_HD] = _softmax_pv(qs[j], ks_ref[0, ksl, 0:kext], ks_ref[0, vsl, 0:kext], bias)
    cols = []
    for h in range(NSA_HEADS):
        cols.append(gsig[:, 3 * h:3 * h + 1] * o_cmp[h] + gsig[:, 3 * h + 1:3 * h + 2] * osel_ref[:, h * NSA_HD:(h + 1) * NSA_HD]
                    + gsig[:, 3 * h + 2:3 * h + 3] * o_win[h])
    o_ref[0] = jnp.concatenate(cols, axis=1)


def _pattn(q, ckv, kst, kwt, small, tq=128, n_ext=4):
    B, T, _ = q.shape
    n_chunks = T // CMP_STRIDE
    assert T % (n_ext * tq) == 0
    full = lambda b, t: (b, 0, 0)
    return pl.pallas_call(
        functools.partial(_pattn_kernel, T=T, tq=tq, n_ext=n_ext), grid=(B, T // tq),
        in_specs=[pl.BlockSpec((1, tq, Q_NSA), lambda b, t: (b, t, 0)), pl.BlockSpec((1, n_chunks, KV_ROW), full),
                  pl.BlockSpec((1, KV_ROW, T), full), pl.BlockSpec((1, KV_ROW, T), full),
                  pl.BlockSpec((1, tq, SMALL_W), lambda b, t: (b, t, 0))],
        out_specs=pl.BlockSpec((1, tq, Q_NSA), lambda b, t: (b, t, 0)),
        out_shape=jax.ShapeDtypeStruct((B, T, Q_NSA), F32),
        scratch_shapes=[pltpu.VMEM((tq, Q_NSA), F32)],
        compiler_params=_cparams(("parallel", "parallel")), name="prompt_nsa_attn",
    )(q, ckv, kst, kwt, small)


KEY_TILE = 256


def _select_blocks_t(psum_t, tl, n_cmp, n_sel, ns_rows):
    n_chunks = psum_t.shape[0]
    sj = lax.broadcasted_iota(jnp.int32, (ns_rows, n_chunks), 0)
    ci = lax.broadcasted_iota(jnp.int32, (ns_rows, n_chunks), 1)
    hit = (ci * CMP_STRIDE < (sj + 1) * SEL_BLOCK) & (ci * CMP_STRIDE + CMP_BLOCK > sj * SEL_BLOCK) & (ci < n_cmp) & (sj < n_sel)
    imp = _dot(jnp.where(hit, 1.0, 0.0), psum_t)
    jj = lax.broadcasted_iota(jnp.int32, (ns_rows, 1), 0)
    imp = jnp.where((jj * SEL_BLOCK <= tl) & (jj < n_sel), imp, -jnp.inf)
    imp = jnp.where((jj == 0) | (jj == tl // SEL_BLOCK), FORCE_SCORE, imp)
    cnt = jnp.zeros(imp.shape, F32)
    for j in range(n_sel):
        row = imp[j:j + 1, :]
        later = jnp.where(jj > j, 1.0, 0.0)
        cnt = cnt + jnp.where(row > imp, 1.0, jnp.where(row == imp, later, 0.0))
    return jnp.where(cnt < min(SEL_TOP, n_sel), 1.0, 0.0)


V_AUG = NSA_HD + 8


def _col_attend(n_tiles, tile_w, k_tile, q_list, v_aug, bias_tile, s_ref, e_ref):
    outs = []
    n_keys = n_tiles * tile_w
    for qh in q_list:
        m8 = None
        for i in range(n_tiles):
            s = jnp.dot(k_tile(i), qh, preferred_element_type=F32)
            bt = bias_tile(i)
            if bt is not None:
                s = s + bt
            s_ref[i * tile_w:(i + 1) * tile_w, :] = s
            for r in range(tile_w // 8):
                m8 = s[8 * r:8 * r + 8] if m8 is None else jnp.maximum(m8, s[8 * r:8 * r + 8])
        m = jnp.max(m8, axis=0, keepdims=True)
        for i in range(n_tiles):
            rows = slice(i * tile_w, (i + 1) * tile_w)
            e_ref[rows, :] = jnp.exp(s_ref[rows, :] - m).astype(BF16)
        a = jnp.dot(v_aug, e_ref[0:n_keys, :], preferred_element_type=F32)
        outs.append(a[:NSA_HD] / a[NSA_HD:NSA_HD + 1])
    return outs


def _ones_rows(v_t):
    n = v_t.shape[1]
    r = lax.broadcasted_iota(jnp.int32, (V_AUG - NSA_HD, n), 0)
    return jnp.concatenate([v_t, jnp.where(r == 0, 1.0, 0.0)], axis=0).astype(BF16)


def _pattn_t_kernel(qt_ref, gt_ref, ckv_ref, kstok_ref, vst_ref, kwtok_ref, vwt_ref, blk_ref, o_ref, osel_ref, s_ref, e_ref,
                    *, T, tq, n_ext):
    n_chunks = T // CMP_STRIDE
    n_cmp = n_chunks - CMP_BLOCK // CMP_STRIDE + 1
    n_sel = -(-T // SEL_BLOCK)
    ns_rows = -(-n_sel // 8) * 8
    span = WINDOW + tq
    qi = pl.program_id(1)
    q0 = qi * tq
    qt = qt_ref[0]
    gt = gt_ref[0]
    ckv = ckv_ref[0]
    tl = q0 + lax.broadcasted_iota(jnp.int32, (1, tq), 1)
    wstart = pl.multiple_of(jnp.maximum(q0 - WINDOW, 0), LANES)
    ci = lax.broadcasted_iota(jnp.int32, (n_chunks, 1), 0)
    bias_c = jnp.where(jnp.where(ci < n_cmp, ci * CMP_STRIDE + CMP_BLOCK - 1, jnp.int32(2 ** 30)) <= tl, 0.0, -1e30)
    zeros_half = jnp.zeros((NSA_HD, tq), BF16)
    tiles_per_ext = T // tq // n_ext
    kw = wstart + lax.broadcasted_iota(jnp.int32, (span, 1), 0)
    bias_w = jnp.where(kw <= tl, jnp.where(kw > tl - WINDOW, 0.0, -1e30), -1e30)
    o_cmp, o_win, sel_bias, q_pads = [], [], [], []
    for g in range(NSA_KV_HEADS):
        ksl = slice(g * NSA_HD, (g + 1) * NSA_HD)
        vsl = slice(KV_ROW // 2 + g * NSA_HD, KV_ROW // 2 + (g + 1) * NSA_HD)
        q_h = [qt[(g * NSA_GROUP + j) * NSA_HD:(g * NSA_GROUP + j + 1) * NSA_HD, :] for j in range(NSA_GROUP)]
        q_pad = [jnp.concatenate([qh, zeros_half] if g == 0 else [zeros_half, qh], axis=0) for qh in q_h]
        q_pads.append(q_pad)
        kc = ckv[:, ksl].astype(BF16)
        vc_t = ckv[:, vsl].T.astype(BF16)
        psum = None
        for j in range(NSA_GROUP):
            s = jnp.dot(kc, q_h[j], preferred_element_type=F32) + bias_c
            e = jnp.exp(s - jnp.max(s, axis=0, keepdims=True))
            p = jnp.where(bias_c == 0.0, e / jnp.sum(e, axis=0, keepdims=True), 0.0)
            o_cmp.append(jnp.dot(vc_t, p.astype(BF16), preferred_element_type=F32))
            psum = p if psum is None else psum + p
        sel = _select_blocks_t(psum, tl, n_cmp, n_sel, ns_rows)
        sel_bias.append(jnp.concatenate([(sel - 1.0) * 1e30, jnp.zeros((LANES - ns_rows, tq), F32)], axis=0).astype(BF16))
        vw_aug = _ones_rows(vwt_ref[0, ksl, pl.ds(wstart, span)])
        o_win += _col_attend(
            span // LANES, LANES, lambda i: kwtok_ref[0, pl.ds(wstart + i * LANES, LANES), :], q_pad, vw_aug,
            lambda i: bias_w[i * LANES:(i + 1) * LANES, :], s_ref, e_ref)

    ext_w = T // n_ext
    for v in range(n_ext):
        kext = (v + 1) * ext_w
        n_tiles = kext // KEY_TILE
        first_diag = n_tiles - ext_w // KEY_TILE

        @pl.when((qi >= v * tiles_per_ext) & (qi < (v + 1) * tiles_per_ext))
        def _():
            keyd = (kext - ext_w) + lax.broadcasted_iota(jnp.int32, (ext_w, 1), 0)
            causal = jnp.where(keyd <= tl, 0.0, -1e30)
            for g in range(NSA_KV_HEADS):
                ksl = slice(g * NSA_HD, (g + 1) * NSA_HD)
                vs_aug = _ones_rows(vst_ref[0, ksl, 0:kext])
                q_aug = [jnp.concatenate([qp, sel_bias[g]], axis=0) for qp in q_pads[g]]
                outs = _col_attend(
                    n_tiles, KEY_TILE,
                    lambda i: jnp.concatenate([kstok_ref[0, i * KEY_TILE:(i + 1) * KEY_TILE, :],
                                               blk_ref[i * KEY_TILE:(i + 1) * KEY_TILE, :]], axis=1),
                    q_aug, vs_aug,
                    lambda i: causal[(i - first_diag) * KEY_TILE:(i - first_diag + 1) * KEY_TILE, :] if i >= first_diag else None,
                    s_ref, e_ref)
                for j in range(NSA_GROUP):
                    h = g * NSA_GROUP + j
                    osel_ref[h * NSA_HD:(h + 1) * NSA_HD, :] = outs[j]

    heads = [gt[3 * h:3 * h + 1, :] * o_cmp[h] + gt[3 * h + 1:3 * h + 2, :] * osel_ref[h * NSA_HD:(h + 1) * NSA_HD, :]
             + gt[3 * h + 2:3 * h + 3, :] * o_win[h] for h in range(NSA_HEADS)]
    o_ref[0] = jnp.concatenate(heads, axis=0).T


def _pattn_t(p, ckv, tq=128, n_ext=4):
    B, _, T = p["qt"].shape
    n_chunks = T // CMP_STRIDE
    assert T % (n_ext * tq) == 0 and (T // n_ext) % KEY_TILE == 0
    full = lambda b, t: (b, 0, 0)
    vhalf = pl.BlockSpec((1, KV_ROW // 2, T), lambda b, t: (b, 1, 0))
    blk = (jnp.arange(T)[:, None] // SEL_BLOCK == jnp.arange(LANES)[None, :]).astype(BF16)
    return pl.pallas_call(
        functools.partial(_pattn_t_kernel, T=T, tq=tq, n_ext=n_ext), grid=(B, T // tq),
        in_specs=[pl.BlockSpec((1, Q_NSA, tq), lambda b, t: (b, 0, t)), pl.BlockSpec((1, G_ROWS, tq), lambda b, t: (b, 0, t)),
                  pl.BlockSpec((1, n_chunks, KV_ROW), full), pl.BlockSpec((1, T, LANES), full), vhalf,
                  pl.BlockSpec((1, T, LANES), full), vhalf, pl.BlockSpec((T, LANES), lambda b, t: (0, 0))],
        out_specs=pl.BlockSpec((1, tq, Q_NSA), lambda b, t: (b, t, 0)),
        out_shape=jax.ShapeDtypeStruct((B, T, Q_NSA), F32),
        scratch_shapes=[pltpu.VMEM((Q_NSA, tq), F32), pltpu.VMEM((T, tq), F32), pltpu.VMEM((T, tq), BF16)],
        compiler_params=_cparams(("parallel", "parallel")), name="prompt_nsa_attn",
    )(p["qt"], p["gt"], ckv, p["ks_tok"], p["kst"], p["kw_tok"], p["kwt"], blk)


def _s1_kernel(pt_ref, *refs, n_pages, q_pos):
    pages = refs[:n_pages]
    q_ref, wbd_ref, pe_ref, w1_ref, w2_ref, ocmp_ref, idx_ref, xs_ref, hs_ref = refs[n_pages:]
    n_chunks = n_pages * PAGE_SIZE // CMP_STRIDE
    n_cmp = n_chunks - CMP_BLOCK // CMP_STRIDE + 1
    n_sel = -(-(q_pos + 1) // SEL_BLOCK)
    ns_pad = -(-n_sel // LANES) * LANES
    ckv = _compress(lambda p: pages[p][0], n_pages, xs_ref, hs_ref, wbd_ref, pe_ref, w1_ref, w2_ref)
    qrow = q_ref[0].astype(F32)
    q8 = jnp.concatenate([qrow[:, h * NSA_HD:(h + 1) * NSA_HD] for h in range(NSA_HEADS)], axis=0)
    row = lax.broadcasted_iota(jnp.int32, (NSA_HEADS, 1), 0)
    tpos = jnp.full((NSA_HEADS, 1), q_pos, jnp.int32)
    o_all = jnp.zeros((NSA_HEADS, NSA_HD), F32)
    psum = jnp.zeros((NSA_HEADS, n_chunks), F32)
    for g in range(NSA_KV_HEADS):
        kc = ckv[:, g * NSA_HD:(g + 1) * NSA_HD]
        vc = ckv[:, KV_ROW // 2 + g * NSA_HD:KV_ROW // 2 + (g + 1) * NSA_HD]
        p = _cmp_probs(q8, kc, tpos, n_cmp)
        mine = (row // NSA_GROUP) == g
        o_all = jnp.where(mine, _dot(p, vc), o_all)
        pg = jnp.sum(jnp.where(mine, p, 0.0), axis=0, keepdims=True)
        psum = jnp.where(row == g, pg, psum)
    sel = _select_blocks(psum, tpos, n_cmp, n_sel, ns_pad)
    a = lax.broadcasted_iota(jnp.int32, (ns_pad, ns_pad), 0)
    b = lax.broadcasted_iota(jnp.int32, (ns_pad, ns_pad), 1)
    before = jnp.dot(sel.astype(BF16), jnp.where(a < b, 1.0, 0.0).astype(BF16), preferred_element_type=F32)
    jj = lax.broadcasted_iota(jnp.int32, (1, ns_pad), 1).astype(F32)
    lane = lax.broadcasted_iota(jnp.int32, (1, LANES), 1)
    idx = jnp.zeros((NSA_HEADS, LANES), F32)
    for k in range(min(SEL_TOP, n_sel)):
        ik = jnp.sum(jnp.where((sel > 0.5) & (before == k), jj, 0.0), axis=1, keepdims=True)
        idx = jnp.where(lane == k, ik, idx)
    idx_ref[0] = idx.astype(jnp.int32)
    ocmp_ref[0] = jnp.concatenate([o_all, jnp.zeros((NSA_HEADS, LANES - NSA_HD), F32)], axis=1)


def _s1(pool_t, page_table, q3, wts, q_pos):
    B, n_pages = page_table.shape
    n_chunks = n_pages * PAGE_SIZE // CMP_STRIDE

    def page_spec(j):
        return pl.BlockSpec((1, KV_ROW, PAGE_SIZE), lambda b, pt: (pt[b, j], 0, 0))

    cst = lambda shape: pl.BlockSpec(shape, lambda b, pt: (0,) * len(shape), pipeline_mode=pl.Buffered(1))
    grid_spec = pltpu.PrefetchScalarGridSpec(
        num_scalar_prefetch=1, grid=(B,),
        in_specs=[page_spec(j) for j in range(n_pages)]
        + [pl.BlockSpec((1, 1, Q_NSA), lambda b, pt: (b, 0, 0)), cst(wts["wbd"].shape), cst(wts["pe"].shape),
           cst(wts["w1"].shape), cst(wts["w2"].shape)],
        out_specs=[pl.BlockSpec((1, NSA_HEADS, LANES), lambda b, pt: (b, 0, 0)),
                   pl.BlockSpec((1, NSA_HEADS, LANES), lambda b, pt: (b, 0, 0))],
        scratch_shapes=[pltpu.VMEM((2, n_pages * PAGE_SIZE, LANES), F32), pltpu.VMEM((n_chunks + 8, CMP_HID), F32)],
    )
    return pl.pallas_call(
        functools.partial(_s1_kernel, n_pages=n_pages, q_pos=q_pos), grid_spec=grid_spec,
        out_shape=[jax.ShapeDtypeStruct((B, NSA_HEADS, LANES), F32), jax.ShapeDtypeStruct((B, NSA_HEADS, LANES), jnp.int32)],
        compiler_params=_cparams(("parallel",)), name="sample_compress_select",
    )(page_table, *([pool_t] * n_pages), q3, wts["wbd"], wts["pe"], wts["w1"], wts["w2"])


def _s2_kernel(phys_ref, meta_ref, *refs, n_top, win_len):
    n_blk = NSA_KV_HEADS * n_top
    pages = refs[:n_blk]
    q_ref, ocmp_ref, small_ref, nsel_ref, nwin_ref, nwint_ref, cwin_ref, o_ref, swin_ref = refs[n_blk:]
    b = pl.program_id(0)
    qrow = q_ref[0].astype(F32)
    q8 = jnp.concatenate([qrow[:, h * NSA_HD:(h + 1) * NSA_HD] for h in range(NSA_HEADS)], axis=0)
    row = lax.broadcasted_iota(jnp.int32, (NSA_HEADS, 1), 0)
    lane = lax.broadcasted_iota(jnp.int32, (1, PAGE_SIZE), 1)
    nsel = nsel_ref[0]
    nwin = nwin_ref[0]
    cwin = cwin_ref[0]
    r = lax.broadcasted_iota(jnp.int32, (1, win_len), 1)
    allow_w = (r > win_len - WINDOW) & (r <= win_len)
    o_sel = jnp.zeros((NSA_HEADS, NSA_HD), F32)
    o_win = jnp.zeros((NSA_HEADS, NSA_HD), F32)
    for g in range(NSA_KV_HEADS):
        mine = (row // NSA_GROUP) == g
        ksl = slice(g * NSA_HD, (g + 1) * NSA_HD)
        vsl = slice(KV_ROW // 2 + g * NSA_HD, KV_ROW // 2 + (g + 1) * NSA_HD)
        kts, vts, masks = [], [], []
        has_new = jnp.zeros((1, 1), F32)
        for k in range(n_top):
            m = meta_ref[b, g * n_top + k]
            kts.append(pages[g * n_top + k][0, ksl, :])
            vts.append(pages[g * n_top + k][0, vsl, :])
            masks.append((lane // SEL_BLOCK) == m)
            has_new = has_new + jnp.where(m == 2, 1.0, 0.0)
        kt = jnp.concatenate(kts, axis=1)
        vt = jnp.concatenate(vts, axis=1)
        allow = jnp.concatenate(masks, axis=1)
        s = jnp.where(allow, _dot(q8, kt), -1e30)
        s_new = jnp.where(has_new > 0.5, jnp.sum(q8 * nsel[:, ksl], axis=1, keepdims=True), -1e30)
        mx = jnp.maximum(jnp.max(s, axis=1, keepdims=True), s_new)
        e = jnp.where(allow, jnp.exp(s - mx), 0.0)
        e_new = jnp.where(has_new > 0.5, jnp.exp(s_new - mx), 0.0)
        den = jnp.sum(e, axis=1, keepdims=True) + e_new
        og = (_dot_nt(e, vt) + e_new * nsel[:, vsl]) / den
        o_sel = jnp.where(mine, og, o_sel)
        s = jnp.where(allow_w, _dot(q8, cwin[ksl, :]), -1e30)
        s_new = jnp.sum(q8 * nwin[:, ksl], axis=1, keepdims=True)
        mx = jnp.maximum(jnp.max(s, axis=1, keepdims=True), s_new)
        e = jnp.where(allow_w, jnp.exp(s - mx), 0.0)
        e_new = jnp.exp(s_new - mx)
        den = jnp.sum(e, axis=1, keepdims=True) + e_new
        og = (_dot_nt(e, cwin[vsl, :]) + e_new * nwin[:, vsl]) / den
        o_win = jnp.where(mine, og, o_win)
    o_cmp = ocmp_ref[0][:, :NSA_HD]
    gsig = jax.nn.sigmoid(small_ref[0][:, G_NSA_OFF:G_NSA_OFF + 3 * NSA_HEADS])
    cols = []
    for h in range(NSA_HEADS):
        cols.append(gsig[:, 3 * h:3 * h + 1] * o_cmp[h:h + 1] + gsig[:, 3 * h + 1:3 * h + 2] * o_sel[h:h + 1]
                    + gsig[:, 3 * h + 2:3 * h + 3] * o_win[h:h + 1])
    o_ref[0] = jnp.concatenate(cols, axis=1)
    blane = lax.broadcasted_iota(jnp.int32, (1, nwint_ref.shape[1]), 1)
    newcol = jnp.sum(jnp.where(blane == b, nwint_ref[...], 0.0), axis=1, keepdims=True)
    swin_ref[0] = jnp.concatenate([cwin[:, 1:], newcol], axis=1)


def _s2(pool_t, phys, meta, q3, ocmp, small3, nsel3, nwin3, nwint, cwint, n_top):
    B = q3.shape[0]
    win_len = cwint.shape[2]
    n_blk = NSA_KV_HEADS * n_top

    def page_spec(j):
        return pl.BlockSpec((1, KV_ROW, PAGE_SIZE), lambda b, ph, me: (ph[b, j], 0, 0))

    per_b = lambda shape: pl.BlockSpec(shape, lambda b, ph, me: (b,) + (0,) * (len(shape) - 1))
    grid_spec = pltpu.PrefetchScalarGridSpec(
        num_scalar_prefetch=2, grid=(B,),
        in_specs=[page_spec(j) for j in range(n_blk)]
        + [per_b((1, 1, Q_NSA)), per_b((1, NSA_HEADS, LANES)), per_b((1, 1, SMALL_W)), per_b((1, 1, KV_ROW)), per_b((1, 1, KV_ROW)),
           pl.BlockSpec(nwint.shape, lambda b, ph, me: (0, 0)), per_b((1, KV_ROW, win_len))],
        out_specs=[per_b((1, 1, Q_NSA)), per_b((1, KV_ROW, win_len))],
    )
    return pl.pallas_call(
        functools.partial(_s2_kernel, n_top=n_top, win_len=win_len), grid_spec=grid_spec,
        out_shape=[jax.ShapeDtypeStruct((B, 1, Q_NSA), F32), jax.ShapeDtypeStruct((B, KV_ROW, win_len), F32)],
        compiler_params=_cparams(("parallel",)), name="sample_sel_win_attn",
    )(phys, meta, *([pool_t] * n_blk), q3, ocmp, small3, nsel3, nwin3, nwint, cwint)


def _gdn_gates(b_raw, a_raw, alog, dtb):
    beta = jax.nn.sigmoid(b_raw)
    g = -jnp.exp(alog) * jax.nn.softplus(a_raw + dtb)
    return beta, g


def _l2n(x):
    return x * lax.rsqrt(jnp.sum(x * x, axis=-1, keepdims=True) + EPS)


def _gdn_prompt_kernel(qkv_ref, z_ref, small_ref, smallt_ref, cw_ref, alog_ref, dtb_ref, alogt_ref, dtbt_ref, on_ref,
                       o_ref, sfin_ref, conv_ref, s_ref, xx_ref):
    ci = pl.program_id(1)
    tc = qkv_ref.shape[1]
    C = GDN_CHUNK

    @pl.when(ci == 0)
    def _():
        s_ref[...] = jnp.zeros(s_ref.shape, F32)
        xx_ref[0:8, :] = jnp.zeros((8, CONV_DIM), F32)

    xx_ref[8:8 + tc, :] = qkv_ref[0]
    y = xx_ref[pl.ds(8 - (CONV_W - 1), tc), :] * cw_ref[0:1, :]
    for j in range(1, CONV_W):
        y = y + xx_ref[pl.ds(8 - (CONV_W - 1) + j, tc), :] * cw_ref[j:j + 1, :]
    c = jax.nn.silu(y)
    tail = xx_ref[tc:tc + 8, :]
    conv_ref[0] = tail
    xx_ref[0:8, :] = tail
    small = small_ref[0]
    beta, gcol = _gdn_gates(small[:, B_OFF:B_OFF + GDN_HEADS], small[:, A_OFF:A_OFF + GDN_HEADS], alog_ref[...], dtb_ref[...])
    _, grow = _gdn_gates(smallt_ref[0][0:GDN_HEADS], smallt_ref[0][GDN_HEADS:2 * GDN_HEADS], alogt_ref[...], dtbt_ref[...])
    z = z_ref[0]
    ii = lax.broadcasted_iota(jnp.int32, (C, C), 0)
    jj = lax.broadcasted_iota(jnp.int32, (C, C), 1)
    tril = ii >= jj
    strict = ii > jj
    eye = jnp.where(ii == jj, 1.0, 0.0)
    hcs = [(cc, h) for cc in range(tc // C) for h in range(GDN_HEADS)]
    loc = {}
    for cc, h in hcs:
        rs = slice(cc * C, (cc + 1) * C)
        qh = _l2n(c[rs, h * GDN_DK:(h + 1) * GDN_DK]) * (GDN_DK ** -0.5)
        kh = _l2n(c[rs, GDN_QK + h * GDN_DK:GDN_QK + (h + 1) * GDN_DK])
        vh = c[rs, 2 * GDN_QK + h * GDN_DV:2 * GDN_QK + (h + 1) * GDN_DV]
        bcol = beta[rs, h:h + 1]
        g_c = gcol[rs, h:h + 1]
        g_r = grow[h:h + 1, rs]
        dec_c = jnp.sum(jnp.where(tril, g_r, 0.0), axis=1, keepdims=True)
        dec_r = jnp.sum(jnp.where(ii <= jj, g_c, 0.0), axis=0, keepdims=True)
        lmask = jnp.where(tril, jnp.exp(jnp.where(tril, dec_c - dec_r, 0.0)), 0.0)
        kb = kh * bcol
        edec = jnp.exp(dec_c)
        dlast = dec_c[C - 1:C, :]
        loc[cc, h] = dict(
            m=-jnp.where(strict, _dot_nt(kb, kh) * lmask, 0.0), rhs=jnp.concatenate([vh * bcol, kb * edec], axis=1),
            attn=_dot_nt(qh, kh) * lmask, qe=qh * edec, kdt=(kh * jnp.exp(dlast - dec_c)).T, elast=jnp.exp(dlast))
    tinv = {k: eye + loc[k]["m"] for k in hcs}
    mpow = {k: _dot3(loc[k]["m"], loc[k]["m"]) for k in hcs}
    n_steps = (C - 1).bit_length() - 1
    for step in range(n_steps):
        for k in hcs:
            if step < n_steps - 1:
                r = _dot3(mpow[k], jnp.concatenate([mpow[k], tinv[k]], axis=1))
                mpow[k], tinv[k] = r[:, :C], tinv[k] + r[:, C:]
            else:
                tinv[k] = tinv[k] + _dot3(mpow[k], tinv[k])
    uw = {k: _dot3(tinv[k], loc[k]["rhs"]) for k in hcs}
    for cc in range(tc // C):
        rs = slice(cc * C, (cc + 1) * C)
        for h in range(GDN_HEADS):
            d = loc[cc, h]
            s_old = s_ref[h]
            ws_qs = _dot(jnp.concatenate([uw[cc, h][:, GDN_DV:], d["qe"]], axis=0), s_old)
            v_new = uw[cc, h][:, :GDN_DV] - ws_qs[:C]
            o = ws_qs[C:] + _dot(d["attn"], v_new)
            s_ref[h] = s_old * d["elast"] + _dot(d["kdt"], v_new)
            o = _rms(o) * on_ref[...] * jax.nn.silu(z[rs, h * GDN_DV:(h + 1) * GDN_DV])
            o_ref[0, rs, h * GDN_DV:(h + 1) * GDN_DV] = o

    @pl.when(ci == pl.num_programs(1) - 1)
    def _():
        sfin_ref[0] = s_ref[...]


def _gdn_prompt(qkv, z, small, wts, tc=128):
    B, T, _ = qkv.shape
    smallt = jnp.transpose(small[:, :, B_OFF:B_OFF + 2 * GDN_HEADS], (0, 2, 1))
    tokb = lambda w: pl.BlockSpec((1, tc, w), lambda b, c: (b, c, 0))
    return pl.pallas_call(
        _gdn_prompt_kernel, grid=(B, T // tc),
        in_specs=[tokb(CONV_DIM), tokb(GDN_V), tokb(SMALL_W), pl.BlockSpec((1, 2 * GDN_HEADS, tc), lambda b, c: (b, 0, c)),
                  _const_spec((CONV_W, CONV_DIM)), _const_spec((1, GDN_HEADS)), _const_spec((1, GDN_HEADS)),
                  _const_spec((GDN_HEADS, 1)), _const_spec((GDN_HEADS, 1)), _const_spec((1, GDN_DV))],
        out_specs=[tokb(GDN_V), pl.BlockSpec((1, GDN_HEADS, GDN_DK, GDN_DV), lambda b, c: (b, 0, 0, 0)),
                   pl.BlockSpec((1, 8, CONV_DIM), lambda b, c: (b, 0, 0))],
        out_shape=[jax.ShapeDtypeStruct((B, T, GDN_V), F32), jax.ShapeDtypeStruct((B, GDN_HEADS, GDN_DK, GDN_DV), F32),
                   jax.ShapeDtypeStruct((B, 8, CONV_DIM), F32)],
        scratch_shapes=[pltpu.VMEM((GDN_HEADS, GDN_DK, GDN_DV), F32), pltpu.VMEM((tc + 8, CONV_DIM), F32)],
        compiler_params=_cparams(("parallel", "arbitrary")), name="gdn_prompt",
    )(qkv, z, small, smallt, wts["conv_w"], wts["alog"], wts["dtb"], wts["alog_t"], wts["dtb_t"], wts["onorm"])


def _gdn_sample_kernel(xx_ref, z_ref, small_ref, cw_ref, alog_ref, dtb_ref, on_ref, s_ref,
                       o_ref, sout_ref, qt_ref, kt_ref, wt_ref, u_ref, sc_ref):
    b = pl.program_id(0)
    nb = z_ref.shape[0]

    @pl.when(b == 0)
    def _():
        y = xx_ref[0] * cw_ref[0:1, :]
        for j in range(1, CONV_W):
            y = y + xx_ref[j] * cw_ref[j:j + 1, :]
        c = jax.nn.silu(y)
        small = small_ref[...]
        beta, g = _gdn_gates(small[:, B_OFF:B_OFF + GDN_HEADS], small[:, A_OFF:A_OFF + GDN_HEADS], alog_ref[...], dtb_ref[...])
        a = jnp.exp(g)
        attn = []
        for h in range(GDN_HEADS):
            qh = _l2n(c[:, h * GDN_DK:(h + 1) * GDN_DK]) * (GDN_DK ** -0.5)
            kh = _l2n(c[:, GDN_QK + h * GDN_DK:GDN_QK + (h + 1) * GDN_DK])
            vh = c[:, 2 * GDN_QK + h * GDN_DV:2 * GDN_QK + (h + 1) * GDN_DV]
            bh, ah = beta[:, h:h + 1], a[:, h:h + 1]
            qt_ref[h] = (qh * ah).T
            kt_ref[h] = kh.T
            wt_ref[h] = (kh * bh * ah).T
            u_ref[h] = vh * bh
            attn.append(jnp.sum(qh * kh, axis=1, keepdims=True))
        sc_ref[...] = jnp.concatenate([a] + attn + [jnp.zeros((nb, LANES - 2 * GDN_HEADS), F32)], axis=1)

    lane = lax.broadcasted_iota(jnp.int32, (1, nb), 1)
    pick = lane == b
    sc = sc_ref[pl.ds(b, 1), :]
    zrow = z_ref[pl.ds(b, 1), :]
    for h in range(GDN_HEADS):
        wcol = jnp.sum(jnp.where(pick, wt_ref[h], 0.0), axis=1, keepdims=True)
        qcol = jnp.sum(jnp.where(pick, qt_ref[h], 0.0), axis=1, keepdims=True)
        kcol = jnp.sum(jnp.where(pick, kt_ref[h], 0.0), axis=1, keepdims=True)
        s_old = s_ref[0, h]
        v_new = u_ref[h, pl.ds(b, 1), :] - jnp.sum(s_old * wcol, axis=0, keepdims=True)
        o = jnp.sum(s_old * qcol, axis=0, keepdims=True) + sc[:, GDN_HEADS + h:GDN_HEADS + h + 1] * v_new
        sout_ref[0, h] = s_old * sc[:, h:h + 1] + kcol * v_new
        o = _rms(o) * on_ref[...] * jax.nn.silu(zrow[:, h * GDN_DV:(h + 1) * GDN_DV])
        o_ref[0, :, h * GDN_DV:(h + 1) * GDN_DV] = o


def _gdn_sample(xx4, z2, small2, state, wts):
    nb = z2.shape[0]
    cst = lambda shape: pl.BlockSpec(shape, lambda b: (0,) * len(shape))
    sspec = pl.BlockSpec((1, GDN_HEADS, GDN_DK, GDN_DV), lambda b: (b, 0, 0, 0))
    return pl.pallas_call(
        _gdn_sample_kernel, grid=(nb,),
        in_specs=[cst(xx4.shape), cst(z2.shape), cst(small2.shape), cst((CONV_W, CONV_DIM)), cst((1, GDN_HEADS)),
                  cst((1, GDN_HEADS)), cst((1, GDN_DV)), sspec],
        out_specs=[pl.BlockSpec((1, 1, GDN_V), lambda b: (b, 0, 0)), sspec],
        out_shape=[jax.ShapeDtypeStruct((nb, 1, GDN_V), F32), jax.ShapeDtypeStruct(state.shape, F32)],
        scratch_shapes=[pltpu.VMEM((GDN_HEADS, GDN_DK, nb), F32)] * 3
        + [pltpu.VMEM((GDN_HEADS, nb, GDN_DV), F32), pltpu.VMEM((nb, LANES), F32)],
        compiler_params=_cparams(("arbitrary",)), name="gdn_sample",
    )(xx4, z2, small2, wts["conv_w"], wts["alog"], wts["dtb"], wts["onorm"], state)


def _mem_prompt_kernel(qm_ref, kv_ref, o_ref):
    qm = qm_ref[0]
    kv = kv_ref[0]
    for h in range(MEM_HEADS):
        sl = slice(h * MEM_HD, (h + 1) * MEM_HD)
        s = _dot_nt(qm[:, sl], kv[:, sl]) * MEM_SCALE
        e = jnp.exp(s - jnp.max(s, axis=-1, keepdims=True))
        p = e / jnp.sum(e, axis=-1, keepdims=True)
        o_ref[0, :, sl] = _dot(p, kv[:, MEM_Q + h * MEM_HD:MEM_Q + (h + 1) * MEM_HD])


def _mem_prompt(qm, mkv, tq=256):
    B, T, _ = qm.shape
    M = mkv.shape[1]
    return pl.pallas_call(
        _mem_prompt_kernel, grid=(B, T // tq),
        in_specs=[pl.BlockSpec((1, tq, MEM_Q), lambda b, t: (b, t, 0)), pl.BlockSpec((1, M, 2 * MEM_Q), lambda b, t: (b, 0, 0))],
        out_specs=pl.BlockSpec((1, tq, MEM_Q), lambda b, t: (b, t, 0)),
        out_shape=jax.ShapeDtypeStruct((B, T, MEM_Q), F32),
        compiler_params=_cparams(("parallel", "parallel")), name="mem_attn_prompt",
    )(qm, mkv)


def _mem_sample_kernel(qm_ref, kv_ref, o_ref):
    q = qm_ref[0].astype(F32)
    for h in range(MEM_HEADS):
        sl = slice(h * MEM_HD, (h + 1) * MEM_HD)
        k = kv_ref[0, :, 0, h, :]
        v = kv_ref[0, :, 1, h, :]
        s = jnp.sum(k * q[:, sl], axis=1, keepdims=True) * MEM_SCALE
        e = jnp.exp(s - jnp.max(s, axis=0, keepdims=True))
        p = e / jnp.sum(e, axis=0, keepdims=True)
        o_ref[0, :, sl] = jnp.sum(v * p, axis=0, keepdims=True)


def _mem_sample(qm3, cache):
    B, M = cache.shape[0], cache.shape[1]
    return pl.pallas_call(
        _mem_sample_kernel, grid=(B,),
        in_specs=[pl.BlockSpec((1, 1, MEM_Q), lambda b: (b, 0, 0)),
                  pl.BlockSpec((1, M, 2, MEM_HEADS, MEM_HD), lambda b: (b, 0, 0, 0, 0))],
        out_specs=pl.BlockSpec((1, 1, MEM_Q), lambda b: (b, 0, 0)),
        out_shape=jax.ShapeDtypeStruct((B, 1, MEM_Q), F32),
        compiler_params=_cparams(("parallel",)), name="mem_attn_sample",
    )(qm3, cache)


def _merge_ffn_kernel(on_ref, og_ref, om_ref, gate_ref, x_ref, wbn_ref, wbg_ref, wbm_ref, wout_ref, gffn_ref, w1_ref, w2_ref, y_ref):
    merged = (gate_ref[:, 0:D_MODEL] * _dot(on_ref[...], wbn_ref[...])
              + gate_ref[:, D_MODEL:2 * D_MODEL] * _dot(og_ref[...], wbg_ref[...])
              + gate_ref[:, 2 * D_MODEL:3 * D_MODEL] * _dot(om_ref[...], wbm_ref[...]))
    h = x_ref[...] + _dot(merged, wout_ref[...])
    f = _dot(_rms(h) * gffn_ref[...], w1_ref[...])
    f = jnp.square(jnp.maximum(f, 0.0))
    y_ref[...] = h + _dot(f, w2_ref[...])


def _merge_ffn(o_nsa, o_gdn, o_mem, gates, x2d, wts, tm):
    n = x2d.shape[0]
    tok = lambda w: pl.BlockSpec((tm, w), lambda i: (i, 0))
    return pl.pallas_call(
        _merge_ffn_kernel, grid=(n // tm,),
        in_specs=[tok(Q_NSA), tok(GDN_V), tok(MEM_Q), tok(N_BRANCH * D_MODEL), tok(D_MODEL),
                  _const_spec((Q_NSA, D_MODEL)), _const_spec((GDN_V, D_MODEL)), _const_spec((MEM_Q, D_MODEL)),
                  _const_spec((D_MODEL, D_MODEL)), _const_spec((1, D_MODEL)), _const_spec((D_MODEL, D_FF)), _const_spec((D_FF, D_MODEL))],
        out_specs=tok(D_MODEL),
        out_shape=jax.ShapeDtypeStruct((n, D_MODEL), F32),
        compiler_params=_cparams(("parallel",)), name="merge_ffn",
    )(o_nsa, o_gdn, o_mem, gates, x2d, wts["w_br_nsa"], wts["w_br_gdn"], wts["w_br_mem"], wts["w_out"], wts["g_ffn"],
      wts["w_ff1"], wts["w_ff2"])


def _prep_weights(g_mix, w_in, nsa_q_norm, nsa_k_norm, cmp_pe, cmp_w1, cmp_w2, gdn_conv_w, gdn_A_log, gdn_dt_bias, gdn_o_norm,
                  g_mem, w_mem_kv, mem_q_norm, mem_k_norm, w_br_nsa, w_br_gdn, w_br_mem, w_out, g_ffn, w_ff1, w_ff2):
    offs = [0]
    for s in IN_SPLITS:
        offs.append(offs[-1] + s)
    wt = w_in.T
    seg = lambda i: wt[offs[i]:offs[i + 1]]
    small = jnp.concatenate([seg(2), seg(4), seg(5)], axis=0)
    small = jnp.pad(small, ((0, SMALL_W - small.shape[0]), (0, 0)))
    kv = seg(1)
    k_of = lambda c: kv[c * KV_ROW:c * KV_ROW + KV_ROW // 2]
    w_tok = jnp.concatenate([seg(0), seg(3), seg(6), seg(7), seg(8), small, k_of(1), k_of(2)], axis=0).T.astype(BF16)
    w_ft = jnp.concatenate([kv, seg(0), jnp.pad(seg(2), ((0, G_ROWS - 3 * NSA_HEADS), (0, 0)))], axis=0).astype(BF16)
    R = CMP_BLOCK // CMP_STRIDE
    w1r = cmp_w1.reshape(2, R, CMP_STRIDE, NSA_HD, CMP_HID)
    wbd = jnp.einsum("krsdf,gh->ksgdhrf", w1r, jnp.eye(NSA_KV_HEADS, dtype=F32))
    wbd = wbd.reshape(2, CMP_STRIDE * NSA_KV_HEADS * NSA_HD, NSA_KV_HEADS * R * CMP_HID).astype(BF16)
    row = lambda v: v.reshape(1, -1)
    return dict(
        g_mix=row(g_mix), w_tok=w_tok, w_ft=w_ft, qn=row(jnp.tile(nsa_q_norm, LANES // NSA_HD)), qn_col=nsa_q_norm.reshape(NSA_HD, 1),
        kn=nsa_k_norm.reshape(3, NSA_HD, 1), kn_row=jnp.tile(nsa_k_norm, (1, LANES // NSA_HD)), mqn=row(mem_q_norm), mkn=row(mem_k_norm), g_mem=row(g_mem),
        w_mem_kv=w_mem_kv.astype(BF16), wbd=wbd,
        pe=jnp.broadcast_to(cmp_pe.reshape(2, 1, CMP_BLOCK * NSA_HD), (2, 8, CMP_BLOCK * NSA_HD)),
        w1=cmp_w1.astype(BF16), w2=cmp_w2.astype(BF16), conv_w=gdn_conv_w,
        alog=row(gdn_A_log), dtb=row(gdn_dt_bias), alog_t=gdn_A_log.reshape(-1, 1), dtb_t=gdn_dt_bias.reshape(-1, 1),
        onorm=row(gdn_o_norm), w_br_nsa=w_br_nsa.astype(BF16), w_br_gdn=w_br_gdn.astype(BF16), w_br_mem=w_br_mem.astype(BF16),
        w_out=w_out.astype(BF16), g_ffn=row(g_ffn), w_ff1=w_ff1.astype(BF16), w_ff2=w_ff2.astype(BF16))


def _rows_5d(kt):
    B, _, N = kt.shape
    return jnp.transpose(kt.reshape(B, 2, NSA_KV_HEADS, NSA_HD, N), (0, 4, 1, 2, 3))


def _feature_major(rows):
    B, N = rows.shape[0], rows.shape[1]
    return jnp.transpose(rows, (0, 2, 3, 4, 1)).reshape(B, KV_ROW, N)


def kernel(x_prompt, x_sample, cache_cmp_kv, cache_sel_kv, cache_win_kv, state_gdn, state_gdn_conv, cache_mem_kv, page_table, mem_prompt, g_mix, w_in, nsa_q_norm, nsa_k_norm, cmp_pe, cmp_w1, cmp_w2, gdn_conv_w, gdn_A_log, gdn_dt_bias, gdn_o_norm, g_mem, w_mem_kv, mem_q_norm, mem_k_norm, w_br_nsa, w_br_gdn, w_br_mem, w_out, g_ffn, w_ff1, w_ff2):
    wts = _prep_weights(g_mix, w_in, nsa_q_norm, nsa_k_norm, cmp_pe, cmp_w1, cmp_w2, gdn_conv_w, gdn_A_log, gdn_dt_bias,
                        gdn_o_norm, g_mem, w_mem_kv, mem_q_norm, mem_k_norm, w_br_nsa, w_br_gdn, w_br_mem, w_out, g_ffn, w_ff1, w_ff2)
    B, T, D = x_prompt.shape
    nb = x_sample.shape[0]
    assert x_sample.shape[1] == 1 and T % 256 == 0 and T >= WINDOW + 128
    n_pages = page_table.shape[1]
    past = n_pages * PAGE_SIZE

    p = _inproj(x_prompt, jnp.arange(T, dtype=jnp.int32), wts, tm=256)
    mkv = _memkv(mem_prompt.reshape(-1, D), wts).reshape(B, -1, 2 * MEM_Q)
    ckv = _pcompress(p["kct"], wts)
    o_nsa = _pattn_t(p, ckv)
    o_gdn, p_state, conv_tail = _gdn_prompt(p["qkv"], p["z"], p["small"], wts)
    o_mem = _mem_prompt(p["qm"], mkv)
    y_prompt = _merge_ffn(o_nsa.reshape(-1, Q_NSA), o_gdn.reshape(-1, GDN_V), o_mem.reshape(-1, MEM_Q),
                          p["gates"].reshape(-1, N_BRANCH * D), x_prompt.reshape(-1, D), wts, tm=256).reshape(B, T, D)
    p_cmp, p_sel = _rows_5d(p["kct"]), _rows_5d(p["kst"])
    p_win = _rows_5d(p["kwt"][:, :, T - min(WINDOW, T):])
    p_conv = conv_tail[:, 8 - (CONV_W - 1):, :]
    p_mem_kv = mkv.reshape(B, -1, 2, MEM_HEADS, MEM_HD)

    xs = x_sample.reshape(1, nb, D)
    s = _inproj(xs, jnp.full((nb,), past, jnp.int32), wts, tm=nb)
    sq, sqkv, sz, sqm, sgates, ssmall, skct, skst, skwt = (s[k] for k in ("q", "qkv", "z", "qm", "gates", "small", "kct", "kst", "kwt"))
    pool_cmp = _feature_major(cache_cmp_kv)
    pool_sel = _feature_major(cache_sel_kv)
    q3 = sq.reshape(nb, 1, Q_NSA)
    ocmp, idx = _s1(pool_cmp, page_table, q3, wts, q_pos=past)
    n_sel = -(-(past + 1) // SEL_BLOCK)
    n_top = min(SEL_TOP, n_sel)
    idx = idx[:, :NSA_KV_HEADS, :n_top].reshape(nb, NSA_KV_HEADS * n_top)
    blk_per_page = PAGE_SIZE // SEL_BLOCK
    is_new = idx * SEL_BLOCK >= past
    page = jnp.take_along_axis(page_table, jnp.where(is_new, 0, idx // blk_per_page), axis=1)
    meta = jnp.where(is_new, blk_per_page, idx % blk_per_page).astype(jnp.int32)
    tokrow = lambda kt: jnp.transpose(kt[0], (1, 0)).reshape(nb, 1, KV_ROW)
    cwint = _feature_major(cache_win_kv)
    o_nsa_s, swin = _s2(pool_sel, page.astype(jnp.int32), meta, q3, ocmp, ssmall.reshape(nb, 1, SMALL_W), tokrow(skst), tokrow(skwt),
                        skwt[0], cwint, n_top)
    xx4 = jnp.concatenate([jnp.transpose(state_gdn_conv, (1, 0, 2)), sqkv], axis=0)
    o_gdn_s, s_state = _gdn_sample(xx4, sz[0], ssmall[0], state_gdn, wts)
    o_mem_s = _mem_sample(sqm.reshape(nb, 1, MEM_Q), cache_mem_kv)
    y_sample = _merge_ffn(o_nsa_s.reshape(nb, Q_NSA), o_gdn_s.reshape(nb, GDN_V), o_mem_s.reshape(nb, MEM_Q), sgates[0], x_sample.reshape(nb, D),
                          wts, tm=nb).reshape(nb, 1, D)
    s_cmp = jnp.transpose(skct[0], (1, 0)).reshape(nb, 1, 2, NSA_KV_HEADS, NSA_HD)
    s_sel = jnp.transpose(skst[0], (1, 0)).reshape(nb, 1, 2, NSA_KV_HEADS, NSA_HD)
    s_win = _rows_5d(swin)
    s_conv = jnp.transpose(xx4[1:], (1, 0, 2))
    return (y_prompt, y_sample, p_cmp, p_sel, p_win, p_state, p_conv, p_mem_kv, s_cmp, s_sel, s_win, s_state, s_conv)
```

```python
import functools

import jax
import jax.numpy as jnp
from jax import lax
from jax.experimental import pallas as pl
from jax.experimental.pallas import tpu as pltpu

F32 = jnp.float32
BF16 = jnp.bfloat16
HI = lax.Precision.HIGHEST

D_MODEL = 1024
PAGE_SIZE = 128
NSA_HEADS = 8
NSA_KV_HEADS = 2
NSA_HD = 64
NSA_GROUP = NSA_HEADS // NSA_KV_HEADS
NSA_SCALE = NSA_HD ** -0.5
CMP_BLOCK = 32
CMP_STRIDE = 16
CMP_HID = 128
SEL_BLOCK = 64
SEL_TOP = 16
WINDOW = 512
FORCE_SCORE = 1e9
GDN_HEADS = 4
GDN_DK = 128
GDN_DV = 128
CONV_W = 4
GDN_CHUNK = 64
MEM_HEADS = 4
MEM_HD = 128
MEM_SCALE = MEM_HD ** -0.5
D_FF = 4 * D_MODEL
ROPE_THETA = 10000.0
EPS = 1e-6

Q_NSA = NSA_HEADS * NSA_HD
KV_ROW = 2 * NSA_KV_HEADS * NSA_HD
GDN_QK = GDN_HEADS * GDN_DK
GDN_V = GDN_HEADS * GDN_DV
CONV_DIM = 2 * GDN_QK + GDN_V
MEM_Q = MEM_HEADS * MEM_HD
N_BRANCH = 3
IN_SPLITS = (Q_NSA, 3 * KV_ROW, 3 * NSA_HEADS, CONV_DIM, GDN_HEADS, GDN_HEADS, GDN_V, MEM_Q, N_BRANCH * D_MODEL)

LANES = 128
SMALL_W = LANES
G_NSA_OFF, B_OFF, A_OFF = 0, 3 * NSA_HEADS, 3 * NSA_HEADS + GDN_HEADS
TOK_Q, TOK_QKV, TOK_Z, TOK_QM, TOK_GBR, TOK_SMALL, TOK_KTOK = 0, 512, 2048, 2560, 3072, 6144, 6272
TOK_W = TOK_KTOK + 2 * LANES
FT_KV, FT_Q, FT_G = 0, 3 * KV_ROW, 3 * KV_ROW + Q_NSA
G_ROWS = 32
FT_W = FT_G + G_ROWS
V7X_VMEM_LIMIT = 56 * 1024 * 1024


def _cparams(sem):
    return pltpu.CompilerParams(dimension_semantics=sem, vmem_limit_bytes=V7X_VMEM_LIMIT)


def _dot(a, b):
    return jnp.dot(a.astype(BF16), b.astype(BF16), preferred_element_type=F32)


def _dot_nt(a, b):
    return lax.dot_general(a.astype(BF16), b.astype(BF16), (((1,), (1,)), ((), ())), preferred_element_type=F32)


def _dot_hi(a, b):
    return jnp.dot(a, b, precision=HI, preferred_element_type=F32)


def _split_bf16(a):
    hi = a.astype(BF16)
    return hi, (a - hi.astype(F32)).astype(BF16)


def _dot3(a, b):
    ah, al = _split_bf16(a)
    bh, bl = _split_bf16(b)
    d = lambda x, y: jnp.dot(x, y, preferred_element_type=F32)
    return d(ah, bh) + (d(ah, bl) + d(al, bh))


def _rms(x, axis=-1):
    return x * lax.rsqrt(jnp.mean(x * x, axis=axis, keepdims=True) + EPS)


def _const_spec(shape):
    nd = len(shape)
    return pl.BlockSpec(shape, lambda *_: (0,) * nd, pipeline_mode=pl.Buffered(1))


def _inproj_kernel(x_ref, gmix_ref, wtok_ref, wft_ref, qn_ref, qnc_ref, kn_ref, knr_ref, cq_ref, sq_ref, ck_ref, sk_ref, mqn_ref, blk_ref,
                   q_ref, qkv_ref, z_ref, qm_ref, gate_ref, small_ref, kstok0_ref, kstok1_ref, kwtok_ref,
                   kc_ref, ks_ref, kw_ref, qt_ref, gt_ref):
    x = x_ref[0]
    ub = (_rms(x) * gmix_ref[...]).astype(BF16)
    tm = x.shape[0]
    lane = lax.broadcasted_iota(jnp.int32, (tm, LANES), 1)
    lo = lane < NSA_HD
    first_half = (lane % NSA_HD) < (NSA_HD // 2)

    def norm_rope_slab(col0, gain):
        qs = jnp.dot(ub, wtok_ref[:, col0:col0 + LANES], preferred_element_type=F32)
        sq = qs * qs
        ss_lo = jnp.sum(jnp.where(lo, sq, 0.0), axis=-1, keepdims=True)
        ss_hi = jnp.sum(jnp.where(lo, 0.0, sq), axis=-1, keepdims=True)
        r = jnp.where(lo, lax.rsqrt(ss_lo / NSA_HD + EPS), lax.rsqrt(ss_hi / NSA_HD + EPS))
        qs = qs * r * gain
        rot = jnp.where(first_half, -pltpu.roll(qs, LANES - NSA_HD // 2, axis=1), pltpu.roll(qs, NSA_HD // 2, axis=1))
        return qs * cq_ref[...] + rot * sq_ref[...]

    for i in range(Q_NSA // LANES):
        q_ref[0, :, i * LANES:(i + 1) * LANES] = (norm_rope_slab(TOK_Q + i * LANES, qn_ref[...]) * NSA_SCALE).astype(BF16)
    ks = norm_rope_slab(TOK_KTOK, knr_ref[1:2, :])
    kstok0_ref[0] = jnp.where(lo, ks, blk_ref[...]).astype(BF16)
    kstok1_ref[0] = jnp.where(lo, pltpu.roll(ks, NSA_HD, axis=1), blk_ref[...]).astype(BF16)
    kwtok_ref[0] = norm_rope_slab(TOK_KTOK + LANES, knr_ref[2:3, :]).astype(BF16)
    qkv_ref[0] = jnp.dot(ub, wtok_ref[:, TOK_QKV:TOK_Z], preferred_element_type=F32)
    z_ref[0] = jnp.dot(ub, wtok_ref[:, TOK_Z:TOK_QM], preferred_element_type=F32)
    for h in range(MEM_HEADS):
        qm = jnp.dot(ub, wtok_ref[:, TOK_QM + h * MEM_HD:TOK_QM + (h + 1) * MEM_HD], preferred_element_type=F32)
        qm_ref[0, :, h * MEM_HD:(h + 1) * MEM_HD] = (_rms(qm) * mqn_ref[...]).astype(BF16)
    for i in range(N_BRANCH):
        gb = jnp.dot(ub, wtok_ref[:, TOK_GBR + i * D_MODEL:TOK_GBR + (i + 1) * D_MODEL], preferred_element_type=F32)
        gate_ref[0, :, i * D_MODEL:(i + 1) * D_MODEL] = jax.nn.sigmoid(gb)
    small_ref[0] = jnp.dot(ub, wtok_ref[:, TOK_SMALL:TOK_SMALL + SMALL_W], preferred_element_type=F32)
    ft = lax.dot_general(wft_ref[...], ub, (((1,), (1,)), ((), ())), preferred_element_type=F32)
    cos = ck_ref[...]
    sin = sk_ref[...]
    half = NSA_HD // 2

    def norm_rope_rows(row0, gain_col):
        kh = _rms(ft[row0:row0 + NSA_HD, :], axis=0) * gain_col
        x1, x2 = kh[:half], kh[half:]
        return x1 * cos - x2 * sin, x2 * cos + x1 * sin

    for c, out_ref in enumerate((kc_ref, ks_ref, kw_ref)):
        base = FT_KV + c * KV_ROW
        for g in range(NSA_KV_HEADS):
            r1, r2 = norm_rope_rows(base + g * NSA_HD, kn_ref[c])
            out_ref[0, g * NSA_HD:g * NSA_HD + half, :] = r1
            out_ref[0, g * NSA_HD + half:(g + 1) * NSA_HD, :] = r2
        out_ref[0, KV_ROW // 2:, :] = ft[base + KV_ROW // 2:base + KV_ROW, :]
    for h in range(NSA_HEADS):
        r1, r2 = norm_rope_rows(FT_Q + h * NSA_HD, qnc_ref[...])
        qt_ref[0, h * NSA_HD:(h + 1) * NSA_HD, :] = (jnp.concatenate([r1, r2], axis=0) * NSA_SCALE).astype(BF16)
    gt_ref[0] = jax.nn.sigmoid(ft[FT_G:FT_G + G_ROWS, :])


def _inproj(x, pos, wts, tm):
    B, T, _ = x.shape
    half = NSA_HD // 2
    inv = ROPE_THETA ** (-jnp.arange(half, dtype=F32) / half)
    ang = pos.astype(F32)[:, None] * inv[None, :]
    cos, sin = jnp.cos(ang), jnp.sin(ang)
    cq, sq = jnp.tile(cos, (1, LANES // half)), jnp.tile(sin, (1, LANES // half))
    ck, sk = cos.T, sin.T
    tok = lambda w: pl.BlockSpec((1, tm, w), lambda b, t: (b, t, 0))
    ftm = lambda r: pl.BlockSpec((1, r, tm), lambda b, t: (b, 0, t))
    tok_shape = lambda w, dt: jax.ShapeDtypeStruct((B, T, w), dt)
    ft_shape = lambda r, dt: jax.ShapeDtypeStruct((B, r, T), dt)
    names = ("q", "qkv", "z", "qm", "gates", "small", "ks_tok0", "ks_tok1", "kw_tok", "kct", "kst", "kwt", "qt", "gt")
    blk = (pos[:, None] // SEL_BLOCK + NSA_HD == jnp.arange(LANES)[None, :]).astype(F32)
    per_t = pl.BlockSpec((tm, LANES), lambda b, t: (t, 0))
    outs = pl.pallas_call(
        _inproj_kernel,
        grid=(B, T // tm),
        in_specs=[tok(D_MODEL), _const_spec((1, D_MODEL)), _const_spec((D_MODEL, TOK_W)), _const_spec((FT_W, D_MODEL)),
                  _const_spec((1, LANES)), _const_spec((NSA_HD, 1)), _const_spec((3, NSA_HD, 1)), _const_spec((3, LANES)),
                  per_t, per_t,
                  pl.BlockSpec((half, tm), lambda b, t: (0, t)), pl.BlockSpec((half, tm), lambda b, t: (0, t)),
                  _const_spec((1, MEM_HD)), per_t],
        out_specs=[tok(Q_NSA), tok(CONV_DIM), tok(GDN_V), tok(MEM_Q), tok(N_BRANCH * D_MODEL), tok(SMALL_W),
                   tok(LANES), tok(LANES), tok(LANES), ftm(KV_ROW), ftm(KV_ROW), ftm(KV_ROW), ftm(Q_NSA), ftm(G_ROWS)],
        out_shape=[tok_shape(Q_NSA, BF16), tok_shape(CONV_DIM, F32), tok_shape(GDN_V, F32), tok_shape(MEM_Q, BF16),
                   tok_shape(N_BRANCH * D_MODEL, F32), tok_shape(SMALL_W, F32),
                   tok_shape(LANES, BF16), tok_shape(LANES, BF16), tok_shape(LANES, BF16),
                   ft_shape(KV_ROW, F32), ft_shape(KV_ROW, F32), ft_shape(KV_ROW, F32), ft_shape(Q_NSA, BF16), ft_shape(G_ROWS, F32)],
        compiler_params=_cparams(("parallel", "parallel")),
        name="inproj",
    )(x, wts["g_mix"], wts["w_tok"], wts["w_ft"], wts["qn"], wts["qn_col"], wts["kn"], wts["kn_row"], cq, sq, ck, sk, wts["mqn"], blk)
    return dict(zip(names, outs))


def _memkv_kernel(m_ref, g_ref, w_ref, kn_ref, o_ref):
    u = _rms(m_ref[...]) * g_ref[...]
    kv = _dot(u, w_ref[...])
    for h in range(MEM_HEADS):
        sl = slice(h * MEM_HD, (h + 1) * MEM_HD)
        o_ref[:, sl] = _rms(kv[:, sl]) * kn_ref[...]
    o_ref[:, MEM_Q:] = kv[:, MEM_Q:]


def _memkv(mem2d, wts, tm=512):
    n = mem2d.shape[0]
    tm = min(tm, n)
    return pl.pallas_call(
        _memkv_kernel, grid=(n // tm,),
        in_specs=[pl.BlockSpec((tm, D_MODEL), lambda i: (i, 0)), _const_spec((1, D_MODEL)),
                  _const_spec((D_MODEL, 2 * MEM_Q)), _const_spec((1, MEM_HD))],
        out_specs=pl.BlockSpec((tm, 2 * MEM_Q), lambda i: (i, 0)),
        out_shape=jax.ShapeDtypeStruct((n, 2 * MEM_Q), F32),
        compiler_params=_cparams(("parallel",)), name="memkv",
    )(mem2d, wts["g_mem"], wts["w_mem_kv"], wts["mkn"])


CHUNK_PITCH = CMP_STRIDE + 8


def _transpose_pages(get_page, pages, xs_ref):
    cpp = PAGE_SIZE // CMP_STRIDE
    for p in pages:
        xt = get_page(p).T
        for c in range(cpp):
            r0 = (p * cpp + c) * CHUNK_PITCH
            xs_ref[0, r0:r0 + CMP_STRIDE, :] = xt[c * CMP_STRIDE:(c + 1) * CMP_STRIDE, :LANES]
            xs_ref[1, r0:r0 + CMP_STRIDE, :] = xt[c * CMP_STRIDE:(c + 1) * CMP_STRIDE, LANES:]


def _compress(n_chunks, xs_ref, hs_ref, wbd_ref, pe_ref, w1_ref, w2_ref, between=None):
    hs_ref[n_chunks:, :] = jnp.zeros((8, CMP_HID), F32)
    parts = []
    for kv in range(2):
        h = jnp.zeros((n_chunks, 2 * NSA_KV_HEADS * CMP_HID), F32)
        for s2 in range(CMP_STRIDE // 2):
            lhs = jnp.concatenate([xs_ref[kv, pl.ds(2 * s2, n_chunks, stride=CHUNK_PITCH), :],
                                   xs_ref[kv, pl.ds(2 * s2 + 1, n_chunks, stride=CHUNK_PITCH), :]], axis=1)
            h = h + jnp.dot(lhs.astype(BF16), wbd_ref[kv, s2 * 2 * LANES:(s2 + 1) * 2 * LANES, :], preferred_element_type=F32)
            if between is not None:
                between(kv * (CMP_STRIDE // 2) + s2)
        pe_h = jnp.dot(pe_ref[kv].astype(BF16), w1_ref[kv], preferred_element_type=F32)[0:1]
        for g in range(NSA_KV_HEADS):
            h0 = h[:, g * 2 * CMP_HID:g * 2 * CMP_HID + CMP_HID]
            hs_ref[0:n_chunks, :] = h[:, g * 2 * CMP_HID + CMP_HID:(g + 1) * 2 * CMP_HID]
            hh = h0 + hs_ref[pl.ds(1, n_chunks), :] + pe_h
            parts.append(jnp.dot(jax.nn.gelu(hh).astype(BF16), w2_ref[kv], preferred_element_type=F32))
    return jnp.concatenate(parts, axis=1)


def _masked_softmax(s, allow):
    s = jnp.where(allow, s, -1e30)
    e = jnp.exp(s - jnp.max(s, axis=-1, keepdims=True))
    p = e / jnp.sum(e, axis=-1, keepdims=True)
    return jnp.where(allow, p, 0.0)


def _cmp_probs(qg, kc, tpos, n_cmp):
    n_chunks = kc.shape[0]
    s = _dot_nt(qg, kc)
    i = lax.broadcasted_iota(jnp.int32, (1, n_chunks), 1)
    allow = jnp.where(i < n_cmp, i * CMP_STRIDE + CMP_BLOCK - 1, jnp.int32(2 ** 30)) <= tpos
    return _masked_softmax(s, allow)


def _select_blocks(psum, tpos, n_cmp, n_sel, ns_pad):
    n_chunks = psum.shape[1]
    ci = lax.broadcasted_iota(jnp.int32, (n_chunks, ns_pad), 0)
    sj = lax.broadcasted_iota(jnp.int32, (n_chunks, ns_pad), 1)
    hit = (ci * CMP_STRIDE < (sj + 1) * SEL_BLOCK) & (ci * CMP_STRIDE + CMP_BLOCK > sj * SEL_BLOCK) & (ci < n_cmp) & (sj < n_sel)
    imp = _dot(psum, jnp.where(hit, 1.0, 0.0))
    jj = lax.broadcasted_iota(jnp.int32, (1, ns_pad), 1)
    imp = jnp.where((jj * SEL_BLOCK <= tpos) & (jj < n_sel), imp, -jnp.inf)
    imp = jnp.where((jj == 0) | (jj == tpos // SEL_BLOCK), FORCE_SCORE, imp)
    cnt = jnp.zeros(imp.shape, F32)
    for j in range(n_sel):
        col = imp[:, j:j + 1]
        later = jnp.where(jj > j, 1.0, 0.0)
        cnt = cnt + jnp.where(col > imp, 1.0, jnp.where(col == imp, later, 0.0))
    return jnp.where(cnt < min(SEL_TOP, n_sel), 1.0, 0.0)


def _head_rows(q, g):
    return jnp.concatenate([q[:, (g * NSA_GROUP + j) * NSA_HD:(g * NSA_GROUP + j + 1) * NSA_HD] for j in range(NSA_GROUP)], axis=0)


def _gate_merge(gsig, o_cmp, o_sel, o_win, tq):
    cols = []
    for h in range(NSA_HEADS):
        g, j = divmod(h, NSA_GROUP)
        rows = slice(j * tq, (j + 1) * tq)
        cols.append(gsig[:, 3 * h:3 * h + 1] * o_cmp[g][rows] + gsig[:, 3 * h + 1:3 * h + 2] * o_sel[g][rows]
                    + gsig[:, 3 * h + 2:3 * h + 3] * o_win[g][rows])
    return jnp.concatenate(cols, axis=1)


def _pcompress_kernel(kc_ref, wbd_ref, pe_ref, w1_ref, w2_ref, o_ref, xs_ref, hs_ref):
    n_pages = kc_ref.shape[2] // PAGE_SIZE
    _transpose_pages(lambda p: kc_ref[0, :, p * PAGE_SIZE:(p + 1) * PAGE_SIZE], range(n_pages), xs_ref)
    o_ref[0] = _compress(n_pages * PAGE_SIZE // CMP_STRIDE, xs_ref, hs_ref, wbd_ref, pe_ref, w1_ref, w2_ref)


def _pcompress(kct, wts):
    B, _, T = kct.shape
    n_chunks = T // CMP_STRIDE
    return pl.pallas_call(
        _pcompress_kernel, grid=(B,),
        in_specs=[pl.BlockSpec((1, KV_ROW, T), lambda b: (b, 0, 0)), _const_spec(wts["wbd"].shape), _const_spec(wts["pe"].shape),
                  _const_spec(wts["w1"].shape), _const_spec(wts["w2"].shape)],
        out_specs=pl.BlockSpec((1, n_chunks, KV_ROW), lambda b: (b, 0, 0)),
        out_shape=jax.ShapeDtypeStruct((B, n_chunks, KV_ROW), F32),
        scratch_shapes=[pltpu.VMEM((2, n_chunks * CHUNK_PITCH, LANES), F32), pltpu.VMEM((n_chunks + 8, CMP_HID), F32)],
        compiler_params=_cparams(("parallel",)), name="prompt_compress",
    )(kct, wts["wbd"], wts["pe"], wts["w1"], wts["w2"])


def _softmax_pv(q, kt, vt, bias):
    s = _dot(q, kt) + bias
    e = jnp.exp(s - jnp.max(s, axis=-1, keepdims=True))
    return _dot_nt(e, vt) / jnp.sum(e, axis=-1, keepdims=True)


def _pattn_kernel(q_ref, ckv_ref, ks_ref, kw_ref, small_ref, o_ref, osel_ref, *, T, tq, n_ext):
    n_chunks = T // CMP_STRIDE
    n_cmp = n_chunks - CMP_BLOCK // CMP_STRIDE + 1
    n_sel = -(-T // SEL_BLOCK)
    ns_pad = -(-n_sel // LANES) * LANES
    span = WINDOW + tq
    qi = pl.program_id(1)
    q0 = qi * tq
    q = q_ref[0]
    ckv = ckv_ref[0]
    tcol = q0 + lax.broadcasted_iota(jnp.int32, (tq, 1), 0)
    wstart = pl.multiple_of(jnp.maximum(q0 - WINDOW, 0), LANES)
    keyw = wstart + lax.broadcasted_iota(jnp.int32, (1, span), 1)
    bias_w = jnp.where(keyw <= tcol, jnp.where(keyw > tcol - WINDOW, 0.0, -1e30), -1e30)
    gsig = jax.nn.sigmoid(small_ref[0][:, G_NSA_OFF:G_NSA_OFF + 3 * NSA_HEADS])
    tiles_per_ext = T // tq // n_ext
    o_cmp, o_win = [], []
    for g in range(NSA_KV_HEADS):
        ksl = slice(g * NSA_HD, (g + 1) * NSA_HD)
        vsl = slice(KV_ROW // 2 + g * NSA_HD, KV_ROW // 2 + (g + 1) * NSA_HD)
        qs = [q[:, (g * NSA_GROUP + j) * NSA_HD:(g * NSA_GROUP + j + 1) * NSA_HD] for j in range(NSA_GROUP)]
        psum = None
        for j in range(NSA_GROUP):
            p = _cmp_probs(qs[j], ckv[:, ksl], tcol, n_cmp)
            o_cmp.append(_dot(p, ckv[:, vsl]))
            psum = p if psum is None else psum + p
        sel = _select_blocks(psum, tcol, n_cmp, n_sel, ns_pad).astype(BF16)
        kwg = kw_ref[0, ksl, pl.ds(wstart, span)]
        vwg = kw_ref[0, vsl, pl.ds(wstart, span)]
        for j in range(NSA_GROUP):
            o_win.append(_softmax_pv(qs[j], kwg, vwg, bias_w))
        for v in range(n_ext):
            kext = (v + 1) * (T // n_ext)

            @pl.when((qi >= v * tiles_per_ext) & (qi < (v + 1) * tiles_per_ext))
            def _():
                es = lax.broadcasted_iota(jnp.int32, (ns_pad, kext), 0)
                ek = lax.broadcasted_iota(jnp.int32, (ns_pad, kext), 1)
                expand = jnp.where(ek // SEL_BLOCK == es, 1.0, 0.0).astype(BF16)
                keysel = jnp.dot(sel, expand, preferred_element_type=F32)
                keyi = lax.broadcasted_iota(jnp.int32, (1, kext), 1)
                bias = jnp.where(keyi <= tcol, jnp.where(keysel > 0.5, 0.0, -1e30), -1e30)
                for j in range(NSA_GROUP):
                    h = g * NSA_GROUP + j
                    osel_ref[:, h * NSA_HD:(h + 1) * NSA_HD] = _softmax_pv(qs[j], ks_ref[0, ksl, 0:kext], ks_ref[0, vsl, 0:kext], bias)
    cols = []
    for h in range(NSA_HEADS):
        cols.append(gsig[:, 3 * h:3 * h + 1] * o_cmp[h] + gsig[:, 3 * h + 1:3 * h + 2] * osel_ref[:, h * NSA_HD:(h + 1) * NSA_HD]
                    + gsig[:, 3 * h + 2:3 * h + 3] * o_win[h])
    o_ref[0] = jnp.concatenate(cols, axis=1)


def _pattn(q, ckv, kst, kwt, small, tq=128, n_ext=4):
    B, T, _ = q.shape
    n_chunks = T // CMP_STRIDE
    assert T % (n_ext * tq) == 0
    full = lambda b, t: (b, 0, 0)
    return pl.pallas_call(
        functools.partial(_pattn_kernel, T=T, tq=tq, n_ext=n_ext), grid=(B, T // tq),
        in_specs=[pl.BlockSpec((1, tq, Q_NSA), lambda b, t: (b, t, 0)), pl.BlockSpec((1, n_chunks, KV_ROW), full),
                  pl.BlockSpec((1, KV_ROW, T), full), pl.BlockSpec((1, KV_ROW, T), full),
                  pl.BlockSpec((1, tq, SMALL_W), lambda b, t: (b, t, 0))],
        out_specs=pl.BlockSpec((1, tq, Q_NSA), lambda b, t: (b, t, 0)),
        out_shape=jax.ShapeDtypeStruct((B, T, Q_NSA), F32),
        scratch_shapes=[pltpu.VMEM((tq, Q_NSA), F32)],
        compiler_params=_cparams(("parallel", "parallel")), name="prompt_nsa_attn",
    )(q, ckv, kst, kwt, small)


KEY_TILE = 256


def _select_blocks_t(psum_t, tl, n_cmp, n_sel, ns_rows):
    n_chunks = psum_t.shape[0]
    sj = lax.broadcasted_iota(jnp.int32, (ns_rows, n_chunks), 0)
    ci = lax.broadcasted_iota(jnp.int32, (ns_rows, n_chunks), 1)
    hit = (ci * CMP_STRIDE < (sj + 1) * SEL_BLOCK) & (ci * CMP_STRIDE + CMP_BLOCK > sj * SEL_BLOCK) & (ci < n_cmp) & (sj < n_sel)
    imp = _dot(jnp.where(hit, 1.0, 0.0), psum_t)
    jj = lax.broadcasted_iota(jnp.int32, (ns_rows, 1), 0)
    imp = jnp.where((jj * SEL_BLOCK <= tl) & (jj < n_sel), imp, -jnp.inf)
    imp = jnp.where((jj == 0) | (jj == tl // SEL_BLOCK), FORCE_SCORE, imp)
    cnt = jnp.zeros(imp.shape, F32)
    for j in range(n_sel):
        row = imp[j:j + 1, :]
        later = jnp.where(jj > j, 1.0, 0.0)
        cnt = cnt + jnp.where(row > imp, 1.0, jnp.where(row == imp, later, 0.0))
    return jnp.where(cnt < min(SEL_TOP, n_sel), 1.0, 0.0)


V_AUG = NSA_HD + 8


def _col_attend(n_tiles, tile_w, k_tile, q_list, v_aug, bias_tile, s_ref, e_ref):
    outs = []
    n_keys = n_tiles * tile_w
    n_heads = len(q_list)
    m_prev = None
    for st in range(n_heads + 1):
        m8 = None
        for i in range(n_tiles):
            rows = slice(i * tile_w, (i + 1) * tile_w)
            if st < n_heads:
                s = jnp.dot(k_tile(i), q_list[st], preferred_element_type=F32)
                bt = bias_tile(i)
                if bt is not None:
                    s = s + bt
                s_ref[st, rows, :] = s
                for r in range(tile_w // 8):
                    m8 = s[8 * r:8 * r + 8] if m8 is None else jnp.maximum(m8, s[8 * r:8 * r + 8])
            if st >= 1:
                e_ref[st - 1, rows, :] = jnp.exp(s_ref[st - 1, rows, :] - m_prev).astype(BF16)
        if st >= 1:
            a = jnp.dot(v_aug, e_ref[st - 1, 0:n_keys, :], preferred_element_type=F32)
            outs.append(a[:NSA_HD] / a[NSA_HD:NSA_HD + 1])
        if st < n_heads:
            m_prev = jnp.max(m8, axis=0, keepdims=True)
    return outs


def _ones_rows(v_t):
    n = v_t.shape[1]
    r = lax.broadcasted_iota(jnp.int32, (V_AUG - NSA_HD, n), 0)
    return jnp.concatenate([v_t, jnp.where(r == 0, 1.0, 0.0)], axis=0).astype(BF16)


def _pattn_t_kernel(qt_ref, gt_ref, ckv_ref, kstok0_ref, kstok1_ref, vst_ref, kwtok_ref, vwt_ref, o_ref, osel_ref, s_ref, e_ref,
                    *, T, tq, n_ext):
    kstok_refs = (kstok0_ref, kstok1_ref)
    n_chunks = T // CMP_STRIDE
    n_cmp = n_chunks - CMP_BLOCK // CMP_STRIDE + 1
    n_sel = -(-T // SEL_BLOCK)
    ns_rows = -(-n_sel // 8) * 8
    span = WINDOW + tq
    qi = pl.program_id(1)
    q0 = qi * tq
    qt = qt_ref[0]
    gt = gt_ref[0]
    ckv = ckv_ref[0]
    tl = q0 + lax.broadcasted_iota(jnp.int32, (1, tq), 1)
    wstart = pl.multiple_of(jnp.maximum(q0 - WINDOW, 0), LANES)
    ci = lax.broadcasted_iota(jnp.int32, (n_chunks, 1), 0)
    bias_c = jnp.where(jnp.where(ci < n_cmp, ci * CMP_STRIDE + CMP_BLOCK - 1, jnp.int32(2 ** 30)) <= tl, 0.0, -1e30)
    zeros_half = jnp.zeros((NSA_HD, tq), BF16)
    tiles_per_ext = T // tq // n_ext
    kw = wstart + lax.broadcasted_iota(jnp.int32, (span, 1), 0)
    bias_w = jnp.where(kw <= tl, jnp.where(kw > tl - WINDOW, 0.0, -1e30), -1e30)
    o_cmp, o_win, sel_bias, q_hs = [], [], [], []
    for g in range(NSA_KV_HEADS):
        ksl = slice(g * NSA_HD, (g + 1) * NSA_HD)
        vsl = slice(KV_ROW // 2 + g * NSA_HD, KV_ROW // 2 + (g + 1) * NSA_HD)
        q_h = [qt[(g * NSA_GROUP + j) * NSA_HD:(g * NSA_GROUP + j + 1) * NSA_HD, :] for j in range(NSA_GROUP)]
        q_pad = [jnp.concatenate([qh, zeros_half] if g == 0 else [zeros_half, qh], axis=0) for qh in q_h]
        q_hs.append(q_h)
        kc = ckv[:, ksl].astype(BF16)
        vc_t = ckv[:, vsl].T.astype(BF16)
        psum = None
        for j in range(NSA_GROUP):
            s = jnp.dot(kc, q_h[j], preferred_element_type=F32) + bias_c
            e = jnp.exp(s - jnp.max(s, axis=0, keepdims=True))
            p = jnp.where(bias_c == 0.0, e / jnp.sum(e, axis=0, keepdims=True), 0.0)
            o_cmp.append(jnp.dot(vc_t, p.astype(BF16), preferred_element_type=F32))
            psum = p if psum is None else psum + p
        sel = _select_blocks_t(psum, tl, n_cmp, n_sel, ns_rows)
        sel_bias.append(jnp.concatenate([(sel - 1.0) * 1e30, jnp.zeros((LANES - NSA_HD - ns_rows, tq), F32)], axis=0).astype(BF16))
        vw_aug = _ones_rows(vwt_ref[0, ksl, pl.ds(wstart, span)])
        o_win += _col_attend(
            span // LANES, LANES, lambda i: kwtok_ref[0, pl.ds(wstart + i * LANES, LANES), :], q_pad, vw_aug,
            lambda i: bias_w[i * LANES:(i + 1) * LANES, :], s_ref, e_ref)

    ext_w = T // n_ext
    for v in range(n_ext):
        kext = (v + 1) * ext_w
        n_tiles = kext // KEY_TILE
        first_diag = n_tiles - ext_w // KEY_TILE

        @pl.when((qi >= v * tiles_per_ext) & (qi < (v + 1) * tiles_per_ext))
        def _():
            keyd = (kext - ext_w) + lax.broadcasted_iota(jnp.int32, (ext_w, 1), 0)
            causal = jnp.where(keyd <= tl, 0.0, -1e30)
            for g in range(NSA_KV_HEADS):
                ksl = slice(g * NSA_HD, (g + 1) * NSA_HD)
                vs_aug = _ones_rows(vst_ref[0, ksl, 0:kext])
                q_aug = [jnp.concatenate([qh, sel_bias[g]], axis=0) for qh in q_hs[g]]
                outs = _col_attend(
                    n_tiles, KEY_TILE, lambda i: kstok_refs[g][0, i * KEY_TILE:(i + 1) * KEY_TILE, :], q_aug, vs_aug,
                    lambda i: causal[(i - first_diag) * KEY_TILE:(i - first_diag + 1) * KEY_TILE, :] if i >= first_diag else None,
                    s_ref, e_ref)
                for j in range(NSA_GROUP):
                    h = g * NSA_GROUP + j
                    osel_ref[h * NSA_HD:(h + 1) * NSA_HD, :] = outs[j]

    heads = [gt[3 * h:3 * h + 1, :] * o_cmp[h] + gt[3 * h + 1:3 * h + 2, :] * osel_ref[h * NSA_HD:(h + 1) * NSA_HD, :]
             + gt[3 * h + 2:3 * h + 3, :] * o_win[h] for h in range(NSA_HEADS)]
    o_ref[0] = jnp.concatenate(heads, axis=0).T


def _pattn_t(p, ckv, tq=256, n_ext=4):
    B, _, T = p["qt"].shape
    n_chunks = T // CMP_STRIDE
    assert T % (n_ext * tq) == 0 and (T // n_ext) % KEY_TILE == 0 and -(-T // SEL_BLOCK) <= LANES - NSA_HD
    full = lambda b, t: (b, 0, 0)
    vhalf = pl.BlockSpec((1, KV_ROW // 2, T), lambda b, t: (b, 1, 0))
    return pl.pallas_call(
        functools.partial(_pattn_t_kernel, T=T, tq=tq, n_ext=n_ext), grid=(B, T // tq),
        in_specs=[pl.BlockSpec((1, Q_NSA, tq), lambda b, t: (b, 0, t)), pl.BlockSpec((1, G_ROWS, tq), lambda b, t: (b, 0, t)),
                  pl.BlockSpec((1, n_chunks, KV_ROW), full), pl.BlockSpec((1, T, LANES), full), pl.BlockSpec((1, T, LANES), full), vhalf,
                  pl.BlockSpec((1, T, LANES), full), vhalf],
        out_specs=pl.BlockSpec((1, tq, Q_NSA), lambda b, t: (b, t, 0)),
        out_shape=jax.ShapeDtypeStruct((B, T, Q_NSA), F32),
        scratch_shapes=[pltpu.VMEM((Q_NSA, tq), F32), pltpu.VMEM((NSA_GROUP, T, tq), F32), pltpu.VMEM((NSA_GROUP, T, tq), BF16)],
        compiler_params=_cparams(("parallel", "parallel")), name="prompt_nsa_attn",
    )(p["qt"], p["gt"], ckv, p["ks_tok0"], p["ks_tok1"], p["kst"], p["kw_tok"], p["kwt"])


def _s1_kernel(pt_ref, *refs, n_pages, q_pos):
    pages = refs[:n_pages]
    q_ref, wbd_ref, pe_ref, w1_ref, w2_ref, ocmp_ref, idx_ref, xs0_ref, xs1_ref, hs_ref = refs[n_pages:]
    b = pl.program_id(0)

    @pl.when(b == 0)
    def _():
        xs1_ref[...] = jnp.zeros(xs1_ref.shape, F32)

    for parity, (xs_w, xs_r) in enumerate(((xs0_ref, xs1_ref), (xs1_ref, xs0_ref))):
        @pl.when(b % 2 == parity)
        def _():
            _s1_body(pages, q_ref, wbd_ref, pe_ref, w1_ref, w2_ref, ocmp_ref, idx_ref, xs_w, xs_r, hs_ref, n_pages=n_pages, q_pos=q_pos)


def _s1_body(pages, q_ref, wbd_ref, pe_ref, w1_ref, w2_ref, ocmp_ref, idx_ref, xs_w, xs_r, hs_ref, *, n_pages, q_pos):
    n_chunks = n_pages * PAGE_SIZE // CMP_STRIDE
    n_cmp = n_chunks - CMP_BLOCK // CMP_STRIDE + 1
    n_sel = -(-(q_pos + 1) // SEL_BLOCK)
    ns_pad = -(-n_sel // LANES) * LANES
    per = -(-n_pages // CMP_STRIDE)

    def between(k):
        _transpose_pages(lambda p: pages[p][0], range(k * per, min((k + 1) * per, n_pages)), xs_w)

    ckv = _compress(n_chunks, xs_r, hs_ref, wbd_ref, pe_ref, w1_ref, w2_ref, between=between)
    qrow = q_ref[0].astype(F32)
    q8 = jnp.concatenate([qrow[:, h * NSA_HD:(h + 1) * NSA_HD] for h in range(NSA_HEADS)], axis=0)
    row = lax.broadcasted_iota(jnp.int32, (NSA_HEADS, 1), 0)
    tpos = jnp.full((NSA_HEADS, 1), q_pos, jnp.int32)
    o_all = jnp.zeros((NSA_HEADS, NSA_HD), F32)
    psum = jnp.zeros((NSA_HEADS, n_chunks), F32)
    for g in range(NSA_KV_HEADS):
        kc = ckv[:, g * NSA_HD:(g + 1) * NSA_HD]
        vc = ckv[:, KV_ROW // 2 + g * NSA_HD:KV_ROW // 2 + (g + 1) * NSA_HD]
        p = _cmp_probs(q8, kc, tpos, n_cmp)
        mine = (row // NSA_GROUP) == g
        o_all = jnp.where(mine, _dot(p, vc), o_all)
        pg = jnp.sum(jnp.where(mine, p, 0.0), axis=0, keepdims=True)
        psum = jnp.where(row == g, pg, psum)
    sel = _select_blocks(psum, tpos, n_cmp, n_sel, ns_pad)
    a = lax.broadcasted_iota(jnp.int32, (ns_pad, ns_pad), 0)
    b = lax.broadcasted_iota(jnp.int32, (ns_pad, ns_pad), 1)
    before = jnp.dot(sel.astype(BF16), jnp.where(a < b, 1.0, 0.0).astype(BF16), preferred_element_type=F32)
    jj = lax.broadcasted_iota(jnp.int32, (1, ns_pad), 1).astype(F32)
    lane = lax.broadcasted_iota(jnp.int32, (1, LANES), 1)
    idx = jnp.zeros((NSA_HEADS, LANES), F32)
    for k in range(min(SEL_TOP, n_sel)):
        ik = jnp.sum(jnp.where((sel > 0.5) & (before == k), jj, 0.0), axis=1, keepdims=True)
        idx = jnp.where(lane == k, ik, idx)
    idx_ref[0] = idx.astype(jnp.int32)
    ocmp_ref[0] = jnp.concatenate([o_all, jnp.zeros((NSA_HEADS, LANES - NSA_HD), F32)], axis=1)


def _s1(pool_t, page_table, q3, wts, q_pos):
    B, n_pages = page_table.shape
    n_chunks = n_pages * PAGE_SIZE // CMP_STRIDE

    def page_spec(j):
        return pl.BlockSpec((1, KV_ROW, PAGE_SIZE), lambda b, pt: (pt[jnp.minimum(b, B - 1), j], 0, 0))

    prev = lambda b, pt: (jnp.maximum(b - 1, 0), 0, 0)
    cst = lambda shape: pl.BlockSpec(shape, lambda b, pt: (0,) * len(shape), pipeline_mode=pl.Buffered(1))
    xs_shape = pltpu.VMEM((2, n_chunks * CHUNK_PITCH, LANES), F32)
    grid_spec = pltpu.PrefetchScalarGridSpec(
        num_scalar_prefetch=1, grid=(B + 1,),
        in_specs=[page_spec(j) for j in range(n_pages)]
        + [pl.BlockSpec((1, 1, Q_NSA), prev), cst(wts["wbd"].shape), cst(wts["pe"].shape),
           cst(wts["w1"].shape), cst(wts["w2"].shape)],
        out_specs=[pl.BlockSpec((1, NSA_HEADS, LANES), prev), pl.BlockSpec((1, NSA_HEADS, LANES), prev)],
        scratch_shapes=[xs_shape, xs_shape, pltpu.VMEM((n_chunks + 8, CMP_HID), F32)],
    )
    return pl.pallas_call(
        functools.partial(_s1_kernel, n_pages=n_pages, q_pos=q_pos), grid_spec=grid_spec,
        out_shape=[jax.ShapeDtypeStruct((B, NSA_HEADS, LANES), F32), jax.ShapeDtypeStruct((B, NSA_HEADS, LANES), jnp.int32)],
        compiler_params=_cparams(("arbitrary",)), name="sample_compress_select",
    )(page_table, *([pool_t] * n_pages), q3, wts["wbd"], wts["pe"], wts["w1"], wts["w2"])


def _s2_kernel(phys_ref, meta_ref, *refs, n_top, win_len):
    n_blk = NSA_KV_HEADS * n_top
    pages = refs[:n_blk]
    q_ref, ocmp_ref, small_ref, nsel_ref, nwin_ref, nwint_ref, cwin_ref, o_ref, swin_ref = refs[n_blk:]
    b = pl.program_id(0)
    qrow = q_ref[0].astype(F32)
    q8 = jnp.concatenate([qrow[:, h * NSA_HD:(h + 1) * NSA_HD] for h in range(NSA_HEADS)], axis=0)
    row = lax.broadcasted_iota(jnp.int32, (NSA_HEADS, 1), 0)
    lane = lax.broadcasted_iota(jnp.int32, (1, PAGE_SIZE), 1)
    nsel = nsel_ref[0]
    nwin = nwin_ref[0]
    cwin = cwin_ref[0]
    r = lax.broadcasted_iota(jnp.int32, (1, win_len), 1)
    allow_w = (r > win_len - WINDOW) & (r <= win_len)
    o_sel = jnp.zeros((NSA_HEADS, NSA_HD), F32)
    o_win = jnp.zeros((NSA_HEADS, NSA_HD), F32)
    for g in range(NSA_KV_HEADS):
        mine = (row // NSA_GROUP) == g
        ksl = slice(g * NSA_HD, (g + 1) * NSA_HD)
        vsl = slice(KV_ROW // 2 + g * NSA_HD, KV_ROW // 2 + (g + 1) * NSA_HD)
        kts, vts, masks = [], [], []
        has_new = jnp.zeros((1, 1), F32)
        for k in range(n_top):
            m = meta_ref[b, g * n_top + k]
            kts.append(pages[g * n_top + k][0, ksl, :])
            vts.append(pages[g * n_top + k][0, vsl, :])
            masks.append((lane // SEL_BLOCK) == m)
            has_new = has_new + jnp.where(m == 2, 1.0, 0.0)
        kt = jnp.concatenate(kts, axis=1)
        vt = jnp.concatenate(vts, axis=1)
        allow = jnp.concatenate(masks, axis=1)
        s = jnp.where(allow, _dot(q8, kt), -1e30)
        s_new = jnp.where(has_new > 0.5, jnp.sum(q8 * nsel[:, ksl], axis=1, keepdims=True), -1e30)
        mx = jnp.maximum(jnp.max(s, axis=1, keepdims=True), s_new)
        e = jnp.where(allow, jnp.exp(s - mx), 0.0)
        e_new = jnp.where(has_new > 0.5, jnp.exp(s_new - mx), 0.0)
        den = jnp.sum(e, axis=1, keepdims=True) + e_new
        og = (_dot_nt(e, vt) + e_new * nsel[:, vsl]) / den
        o_sel = jnp.where(mine, og, o_sel)
        s = jnp.where(allow_w, _dot(q8, cwin[ksl, :]), -1e30)
        s_new = jnp.sum(q8 * nwin[:, ksl], axis=1, keepdims=True)
        mx = jnp.maximum(jnp.max(s, axis=1, keepdims=True), s_new)
        e = jnp.where(allow_w, jnp.exp(s - mx), 0.0)
        e_new = jnp.exp(s_new - mx)
        den = jnp.sum(e, axis=1, keepdims=True) + e_new
        og = (_dot_nt(e, cwin[vsl, :]) + e_new * nwin[:, vsl]) / den
        o_win = jnp.where(mine, og, o_win)
    o_cmp = ocmp_ref[0][:, :NSA_HD]
    gsig = jax.nn.sigmoid(small_ref[0][:, G_NSA_OFF:G_NSA_OFF + 3 * NSA_HEADS])
    cols = []
    for h in range(NSA_HEADS):
        cols.append(gsig[:, 3 * h:3 * h + 1] * o_cmp[h:h + 1] + gsig[:, 3 * h + 1:3 * h + 2] * o_sel[h:h + 1]
                    + gsig[:, 3 * h + 2:3 * h + 3] * o_win[h:h + 1])
    o_ref[0] = jnp.concatenate(cols, axis=1)
    blane = lax.broadcasted_iota(jnp.int32, (1, nwint_ref.shape[1]), 1)
    newcol = jnp.sum(jnp.where(blane == b, nwint_ref[...], 0.0), axis=1, keepdims=True)
    swin_ref[0] = jnp.concatenate([cwin[:, 1:], newcol], axis=1)


def _s2(pool_t, phys, meta, q3, ocmp, small3, nsel3, nwin3, nwint, cwint, n_top):
    B = q3.shape[0]
    win_len = cwint.shape[2]
    n_blk = NSA_KV_HEADS * n_top

    def page_spec(j):
        return pl.BlockSpec((1, KV_ROW, PAGE_SIZE), lambda b, ph, me: (ph[b, j], 0, 0))

    per_b = lambda shape: pl.BlockSpec(shape, lambda b, ph, me: (b,) + (0,) * (len(shape) - 1))
    grid_spec = pltpu.PrefetchScalarGridSpec(
        num_scalar_prefetch=2, grid=(B,),
        in_specs=[page_spec(j) for j in range(n_blk)]
        + [per_b((1, 1, Q_NSA)), per_b((1, NSA_HEADS, LANES)), per_b((1, 1, SMALL_W)), per_b((1, 1, KV_ROW)), per_b((1, 1, KV_ROW)),
           pl.BlockSpec(nwint.shape, lambda b, ph, me: (0, 0)), per_b((1, KV_ROW, win_len))],
        out_specs=[per_b((1, 1, Q_NSA)), per_b((1, KV_ROW, win_len))],
    )
    return pl.pallas_call(
        functools.partial(_s2_kernel, n_top=n_top, win_len=win_len), grid_spec=grid_spec,
        out_shape=[jax.ShapeDtypeStruct((B, 1, Q_NSA), F32), jax.ShapeDtypeStruct((B, KV_ROW, win_len), F32)],
        compiler_params=_cparams(("parallel",)), name="sample_sel_win_attn",
    )(phys, meta, *([pool_t] * n_blk), q3, ocmp, small3, nsel3, nwin3, nwint, cwint)


def _gdn_gates(b_raw, a_raw, alog, dtb):
    beta = jax.nn.sigmoid(b_raw)
    g = -jnp.exp(alog) * jax.nn.softplus(a_raw + dtb)
    return beta, g


def _l2n(x):
    return x * lax.rsqrt(jnp.sum(x * x, axis=-1, keepdims=True) + EPS)


def _gdn_prompt_kernel(qkv_ref, z_ref, small_ref, smallt_ref, cw_ref, alog_ref, dtb_ref, alogt_ref, dtbt_ref, on_ref,
                       o_ref, sfin_ref, conv_ref, s_ref, xx_ref):
    ci = pl.program_id(1)
    tc = qkv_ref.shape[1]
    C = GDN_CHUNK

    @pl.when(ci == 0)
    def _():
        s_ref[...] = jnp.zeros(s_ref.shape, F32)
        xx_ref[0:8, :] = jnp.zeros((8, CONV_DIM), F32)

    xx_ref[8:8 + tc, :] = qkv_ref[0]
    y = xx_ref[pl.ds(8 - (CONV_W - 1), tc), :] * cw_ref[0:1, :]
    for j in range(1, CONV_W):
        y = y + xx_ref[pl.ds(8 - (CONV_W - 1) + j, tc), :] * cw_ref[j:j + 1, :]
    c = jax.nn.silu(y)
    tail = xx_ref[tc:tc + 8, :]
    conv_ref[0] = tail
    xx_ref[0:8, :] = tail
    small = small_ref[0]
    beta, gcol = _gdn_gates(small[:, B_OFF:B_OFF + GDN_HEADS], small[:, A_OFF:A_OFF + GDN_HEADS], alog_ref[...], dtb_ref[...])
    _, grow = _gdn_gates(smallt_ref[0][0:GDN_HEADS], smallt_ref[0][GDN_HEADS:2 * GDN_HEADS], alogt_ref[...], dtbt_ref[...])
    z = z_ref[0]
    ii = lax.broadcasted_iota(jnp.int32, (C, C), 0)
    jj = lax.broadcasted_iota(jnp.int32, (C, C), 1)
    tril = ii >= jj
    strict = ii > jj
    eye = jnp.where(ii == jj, 1.0, 0.0)
    hcs = [(cc, h) for cc in range(tc // C) for h in range(GDN_HEADS)]
    loc = {}
    for cc, h in hcs:
        rs = slice(cc * C, (cc + 1) * C)
        qh = _l2n(c[rs, h * GDN_DK:(h + 1) * GDN_DK]) * (GDN_DK ** -0.5)
        kh = _l2n(c[rs, GDN_QK + h * GDN_DK:GDN_QK + (h + 1) * GDN_DK])
        vh = c[rs, 2 * GDN_QK + h * GDN_DV:2 * GDN_QK + (h + 1) * GDN_DV]
        bcol = beta[rs, h:h + 1]
        g_c = gcol[rs, h:h + 1]
        g_r = grow[h:h + 1, rs]
        dec_c = jnp.sum(jnp.where(tril, g_r, 0.0), axis=1, keepdims=True)
        dec_r = jnp.sum(jnp.where(ii <= jj, g_c, 0.0), axis=0, keepdims=True)
        lmask = jnp.where(tril, jnp.exp(jnp.where(tril, dec_c - dec_r, 0.0)), 0.0)
        kb = kh * bcol
        edec = jnp.exp(dec_c)
        dlast = dec_c[C - 1:C, :]
        loc[cc, h] = dict(
            m=-jnp.where(strict, _dot_nt(kb, kh) * lmask, 0.0), rhs=jnp.concatenate([vh * bcol, kb * edec], axis=1),
            attn=_dot_nt(qh, kh) * lmask, qe=qh * edec, kdt=(kh * jnp.exp(dlast - dec_c)).T, elast=jnp.exp(dlast))
    tinv = {k: eye + loc[k]["m"] for k in hcs}
    mpow = {k: _dot3(loc[k]["m"], loc[k]["m"]) for k in hcs}
    n_steps = (C - 1).bit_length() - 1
    for step in range(n_steps):
        for k in hcs:
            if step < n_steps - 1:
                r = _dot3(mpow[k], jnp.concatenate([mpow[k], tinv[k]], axis=1))
                mpow[k], tinv[k] = r[:, :C], tinv[k] + r[:, C:]
            else:
                tinv[k] = tinv[k] + _dot3(mpow[k], tinv[k])
    uw = {k: _dot3(tinv[k], loc[k]["rhs"]) for k in hcs}
    for cc in range(tc // C):
        rs = slice(cc * C, (cc + 1) * C)
        for h in range(GDN_HEADS):
            d = loc[cc, h]
            s_old = s_ref[h]
            ws_qs = _dot(jnp.concatenate([uw[cc, h][:, GDN_DV:], d["qe"]], axis=0), s_old)
            v_new = uw[cc, h][:, :GDN_DV] - ws_qs[:C]
            o = ws_qs[C:] + _dot(d["attn"], v_new)
            s_ref[h] = s_old * d["elast"] + _dot(d["kdt"], v_new)
            o = _rms(o) * on_ref[...] * jax.nn.silu(z[rs, h * GDN_DV:(h + 1) * GDN_DV])
            o_ref[0, rs, h * GDN_DV:(h + 1) * GDN_DV] = o

    @pl.when(ci == pl.num_programs(1) - 1)
    def _():
        sfin_ref[0] = s_ref[...]


def _gdn_prompt(qkv, z, small, wts, tc=128):
    B, T, _ = qkv.shape
    smallt = jnp.transpose(small[:, :, B_OFF:B_OFF + 2 * GDN_HEADS], (0, 2, 1))
    tokb = lambda w: pl.BlockSpec((1, tc, w), lambda b, c: (b, c, 0))
    return pl.pallas_call(
        _gdn_prompt_kernel, grid=(B, T // tc),
        in_specs=[tokb(CONV_DIM), tokb(GDN_V), tokb(SMALL_W), pl.BlockSpec((1, 2 * GDN_HEADS, tc), lambda b, c: (b, 0, c)),
                  _const_spec((CONV_W, CONV_DIM)), _const_spec((1, GDN_HEADS)), _const_spec((1, GDN_HEADS)),
                  _const_spec((GDN_HEADS, 1)), _const_spec((GDN_HEADS, 1)), _const_spec((1, GDN_DV))],
        out_specs=[tokb(GDN_V), pl.BlockSpec((1, GDN_HEADS, GDN_DK, GDN_DV), lambda b, c: (b, 0, 0, 0)),
                   pl.BlockSpec((1, 8, CONV_DIM), lambda b, c: (b, 0, 0))],
        out_shape=[jax.ShapeDtypeStruct((B, T, GDN_V), F32), jax.ShapeDtypeStruct((B, GDN_HEADS, GDN_DK, GDN_DV), F32),
                   jax.ShapeDtypeStruct((B, 8, CONV_DIM), F32)],
        scratch_shapes=[pltpu.VMEM((GDN_HEADS, GDN_DK, GDN_DV), F32), pltpu.VMEM((tc + 8, CONV_DIM), F32)],
        compiler_params=_cparams(("parallel", "arbitrary")), name="gdn_prompt",
    )(qkv, z, small, smallt, wts["conv_w"], wts["alog"], wts["dtb"], wts["alog_t"], wts["dtb_t"], wts["onorm"])


def _gdn_sample_kernel(xx_ref, z_ref, small_ref, cw_ref, alog_ref, dtb_ref, on_ref, s_ref,
                       o_ref, sout_ref, qt_ref, kt_ref, wt_ref, u_ref, sc_ref):
    b = pl.program_id(0)
    nb = z_ref.shape[0]

    @pl.when(b == 0)
    def _():
        y = xx_ref[0] * cw_ref[0:1, :]
        for j in range(1, CONV_W):
            y = y + xx_ref[j] * cw_ref[j:j + 1, :]
        c = jax.nn.silu(y)
        small = small_ref[...]
        beta, g = _gdn_gates(small[:, B_OFF:B_OFF + GDN_HEADS], small[:, A_OFF:A_OFF + GDN_HEADS], alog_ref[...], dtb_ref[...])
        a = jnp.exp(g)
        attn = []
        for h in range(GDN_HEADS):
            qh = _l2n(c[:, h * GDN_DK:(h + 1) * GDN_DK]) * (GDN_DK ** -0.5)
            kh = _l2n(c[:, GDN_QK + h * GDN_DK:GDN_QK + (h + 1) * GDN_DK])
            vh = c[:, 2 * GDN_QK + h * GDN_DV:2 * GDN_QK + (h + 1) * GDN_DV]
            bh, ah = beta[:, h:h + 1], a[:, h:h + 1]
            qt_ref[h] = (qh * ah).T
            kt_ref[h] = kh.T
            wt_ref[h] = (kh * bh * ah).T
            u_ref[h] = vh * bh
            attn.append(jnp.sum(qh * kh, axis=1, keepdims=True))
        sc_ref[...] = jnp.concatenate([a] + attn + [jnp.zeros((nb, LANES - 2 * GDN_HEADS), F32)], axis=1)

    lane = lax.broadcasted_iota(jnp.int32, (1, nb), 1)
    pick = lane == b
    sc = sc_ref[pl.ds(b, 1), :]
    zrow = z_ref[pl.ds(b, 1), :]
    for h in range(GDN_HEADS):
        wcol = jnp.sum(jnp.where(pick, wt_ref[h], 0.0), axis=1, keepdims=True)
        qcol = jnp.sum(jnp.where(pick, qt_ref[h], 0.0), axis=1, keepdims=True)
        kcol = jnp.sum(jnp.where(pick, kt_ref[h], 0.0), axis=1, keepdims=True)
        s_old = s_ref[0, h]
        v_new = u_ref[h, pl.ds(b, 1), :] - jnp.sum(s_old * wcol, axis=0, keepdims=True)
        o = jnp.sum(s_old * qcol, axis=0, keepdims=True) + sc[:, GDN_HEADS + h:GDN_HEADS + h + 1] * v_new
        sout_ref[0, h] = s_old * sc[:, h:h + 1] + kcol * v_new
        o = _rms(o) * on_ref[...] * jax.nn.silu(zrow[:, h * GDN_DV:(h + 1) * GDN_DV])
        o_ref[0, :, h * GDN_DV:(h + 1) * GDN_DV] = o


def _gdn_sample(xx4, z2, small2, state, wts):
    nb = z2.shape[0]
    cst = lambda shape: pl.BlockSpec(shape, lambda b: (0,) * len(shape))
    sspec = pl.BlockSpec((1, GDN_HEADS, GDN_DK, GDN_DV), lambda b: (b, 0, 0, 0))
    return pl.pallas_call(
        _gdn_sample_kernel, grid=(nb,),
        in_specs=[cst(xx4.shape), cst(z2.shape), cst(small2.shape), cst((CONV_W, CONV_DIM)), cst((1, GDN_HEADS)),
                  cst((1, GDN_HEADS)), cst((1, GDN_DV)), sspec],
        out_specs=[pl.BlockSpec((1, 1, GDN_V), lambda b: (b, 0, 0)), sspec],
        out_shape=[jax.ShapeDtypeStruct((nb, 1, GDN_V), F32), jax.ShapeDtypeStruct(state.shape, F32)],
        scratch_shapes=[pltpu.VMEM((GDN_HEADS, GDN_DK, nb), F32)] * 3
        + [pltpu.VMEM((GDN_HEADS, nb, GDN_DV), F32), pltpu.VMEM((nb, LANES), F32)],
        compiler_params=_cparams(("arbitrary",)), name="gdn_sample",
    )(xx4, z2, small2, wts["conv_w"], wts["alog"], wts["dtb"], wts["onorm"], state)


def _mem_prompt_kernel(qm_ref, kv_ref, o_ref):
    qm = qm_ref[0]
    kv = kv_ref[0]
    for h in range(MEM_HEADS):
        sl = slice(h * MEM_HD, (h + 1) * MEM_HD)
        s = _dot_nt(qm[:, sl], kv[:, sl]) * MEM_SCALE
        e = jnp.exp(s - jnp.max(s, axis=-1, keepdims=True))
        p = e / jnp.sum(e, axis=-1, keepdims=True)
        o_ref[0, :, sl] = _dot(p, kv[:, MEM_Q + h * MEM_HD:MEM_Q + (h + 1) * MEM_HD])


def _mem_prompt(qm, mkv, tq=256):
    B, T, _ = qm.shape
    M = mkv.shape[1]
    return pl.pallas_call(
        _mem_prompt_kernel, grid=(B, T // tq),
        in_specs=[pl.BlockSpec((1, tq, MEM_Q), lambda b, t: (b, t, 0)), pl.BlockSpec((1, M, 2 * MEM_Q), lambda b, t: (b, 0, 0))],
        out_specs=pl.BlockSpec((1, tq, MEM_Q), lambda b, t: (b, t, 0)),
        out_shape=jax.ShapeDtypeStruct((B, T, MEM_Q), F32),
        compiler_params=_cparams(("parallel", "parallel")), name="mem_attn_prompt",
    )(qm, mkv)


def _mem_sample_kernel(qm_ref, kv_ref, o_ref):
    q = qm_ref[0].astype(F32)
    for h in range(MEM_HEADS):
        sl = slice(h * MEM_HD, (h + 1) * MEM_HD)
        k = kv_ref[0, :, 0, h, :]
        v = kv_ref[0, :, 1, h, :]
        s = jnp.sum(k * q[:, sl], axis=1, keepdims=True) * MEM_SCALE
        e = jnp.exp(s - jnp.max(s, axis=0, keepdims=True))
        p = e / jnp.sum(e, axis=0, keepdims=True)
        o_ref[0, :, sl] = jnp.sum(v * p, axis=0, keepdims=True)


def _mem_sample(qm3, cache):
    B, M = cache.shape[0], cache.shape[1]
    return pl.pallas_call(
        _mem_sample_kernel, grid=(B,),
        in_specs=[pl.BlockSpec((1, 1, MEM_Q), lambda b: (b, 0, 0)),
                  pl.BlockSpec((1, M, 2, MEM_HEADS, MEM_HD), lambda b: (b, 0, 0, 0, 0))],
        out_specs=pl.BlockSpec((1, 1, MEM_Q), lambda b: (b, 0, 0)),
        out_shape=jax.ShapeDtypeStruct((B, 1, MEM_Q), F32),
        compiler_params=_cparams(("parallel",)), name="mem_attn_sample",
    )(qm3, cache)


def _merge_ffn_kernel(on_ref, og_ref, om_ref, gate_ref, x_ref, wbn_ref, wbg_ref, wbm_ref, wout_ref, gffn_ref, w1_ref, w2_ref, y_ref):
    merged = (gate_ref[:, 0:D_MODEL] * _dot(on_ref[...], wbn_ref[...])
              + gate_ref[:, D_MODEL:2 * D_MODEL] * _dot(og_ref[...], wbg_ref[...])
              + gate_ref[:, 2 * D_MODEL:3 * D_MODEL] * _dot(om_ref[...], wbm_ref[...]))
    h = x_ref[...] + _dot(merged, wout_ref[...])
    f = _dot(_rms(h) * gffn_ref[...], w1_ref[...])
    f = jnp.square(jnp.maximum(f, 0.0))
    y_ref[...] = h + _dot(f, w2_ref[...])


def _merge_ffn(o_nsa, o_gdn, o_mem, gates, x2d, wts, tm):
    n = x2d.shape[0]
    tok = lambda w: pl.BlockSpec((tm, w), lambda i: (i, 0))
    return pl.pallas_call(
        _merge_ffn_kernel, grid=(n // tm,),
        in_specs=[tok(Q_NSA), tok(GDN_V), tok(MEM_Q), tok(N_BRANCH * D_MODEL), tok(D_MODEL),
                  _const_spec((Q_NSA, D_MODEL)), _const_spec((GDN_V, D_MODEL)), _const_spec((MEM_Q, D_MODEL)),
                  _const_spec((D_MODEL, D_MODEL)), _const_spec((1, D_MODEL)), _const_spec((D_MODEL, D_FF)), _const_spec((D_FF, D_MODEL))],
        out_specs=tok(D_MODEL),
        out_shape=jax.ShapeDtypeStruct((n, D_MODEL), F32),
        compiler_params=_cparams(("parallel",)), name="merge_ffn",
    )(o_nsa, o_gdn, o_mem, gates, x2d, wts["w_br_nsa"], wts["w_br_gdn"], wts["w_br_mem"], wts["w_out"], wts["g_ffn"],
      wts["w_ff1"], wts["w_ff2"])


def _prep_weights(g_mix, w_in, nsa_q_norm, nsa_k_norm, cmp_pe, cmp_w1, cmp_w2, gdn_conv_w, gdn_A_log, gdn_dt_bias, gdn_o_norm,
                  g_mem, w_mem_kv, mem_q_norm, mem_k_norm, w_br_nsa, w_br_gdn, w_br_mem, w_out, g_ffn, w_ff1, w_ff2):
    offs = [0]
    for s in IN_SPLITS:
        offs.append(offs[-1] + s)
    wt = w_in.T
    seg = lambda i: wt[offs[i]:offs[i + 1]]
    small = jnp.concatenate([seg(2), seg(4), seg(5)], axis=0)
    small = jnp.pad(small, ((0, SMALL_W - small.shape[0]), (0, 0)))
    kv = seg(1)
    k_of = lambda c: kv[c * KV_ROW:c * KV_ROW + KV_ROW // 2]
    w_tok = jnp.concatenate([seg(0), seg(3), seg(6), seg(7), seg(8), small, k_of(1), k_of(2)], axis=0).T.astype(BF16)
    w_ft = jnp.concatenate([kv, seg(0), jnp.pad(seg(2), ((0, G_ROWS - 3 * NSA_HEADS), (0, 0)))], axis=0).astype(BF16)
    R = CMP_BLOCK // CMP_STRIDE
    w1r = cmp_w1.reshape(2, R, CMP_STRIDE, NSA_HD, CMP_HID)
    wbd = jnp.einsum("krsdf,gh->ksgdhrf", w1r, jnp.eye(NSA_KV_HEADS, dtype=F32))
    wbd = wbd.reshape(2, CMP_STRIDE * NSA_KV_HEADS * NSA_HD, NSA_KV_HEADS * R * CMP_HID).astype(BF16)
    row = lambda v: v.reshape(1, -1)
    return dict(
        g_mix=row(g_mix), w_tok=w_tok, w_ft=w_ft, qn=row(jnp.tile(nsa_q_norm, LANES // NSA_HD)), qn_col=nsa_q_norm.reshape(NSA_HD, 1),
        kn=nsa_k_norm.reshape(3, NSA_HD, 1), kn_row=jnp.tile(nsa_k_norm, (1, LANES // NSA_HD)), mqn=row(mem_q_norm), mkn=row(mem_k_norm), g_mem=row(g_mem),
        w_mem_kv=w_mem_kv.astype(BF16), wbd=wbd,
        pe=jnp.broadcast_to(cmp_pe.reshape(2, 1, CMP_BLOCK * NSA_HD), (2, 8, CMP_BLOCK * NSA_HD)),
        w1=cmp_w1.astype(BF16), w2=cmp_w2.astype(BF16), conv_w=gdn_conv_w,
        alog=row(gdn_A_log), dtb=row(gdn_dt_bias), alog_t=gdn_A_log.reshape(-1, 1), dtb_t=gdn_dt_bias.reshape(-1, 1),
        onorm=row(gdn_o_norm), w_br_nsa=w_br_nsa.astype(BF16), w_br_gdn=w_br_gdn.astype(BF16), w_br_mem=w_br_mem.astype(BF16),
        w_out=w_out.astype(BF16), g_ffn=row(g_ffn), w_ff1=w_ff1.astype(BF16), w_ff2=w_ff2.astype(BF16))


def _rows_5d(kt):
    B, _, N = kt.shape
    return jnp.transpose(kt.reshape(B, 2, NSA_KV_HEADS, NSA_HD, N), (0, 4, 1, 2, 3))


def _feature_major(rows):
    B, N = rows.shape[0], rows.shape[1]
    return jnp.transpose(rows, (0, 2, 3, 4, 1)).reshape(B, KV_ROW, N)


def kernel(x_prompt, x_sample, cache_cmp_kv, cache_sel_kv, cache_win_kv, state_gdn, state_gdn_conv, cache_mem_kv, page_table, mem_prompt, g_mix, w_in, nsa_q_norm, nsa_k_norm, cmp_pe, cmp_w1, cmp_w2, gdn_conv_w, gdn_A_log, gdn_dt_bias, gdn_o_norm, g_mem, w_mem_kv, mem_q_norm, mem_k_norm, w_br_nsa, w_br_gdn, w_br_mem, w_out, g_ffn, w_ff1, w_ff2):
    wts = _prep_weights(g_mix, w_in, nsa_q_norm, nsa_k_norm, cmp_pe, cmp_w1, cmp_w2, gdn_conv_w, gdn_A_log, gdn_dt_bias,
                        gdn_o_norm, g_mem, w_mem_kv, mem_q_norm, mem_k_norm, w_br_nsa, w_br_gdn, w_br_mem, w_out, g_ffn, w_ff1, w_ff2)
    B, T, D = x_prompt.shape
    nb = x_sample.shape[0]
    assert x_sample.shape[1] == 1 and T % 256 == 0 and T >= WINDOW + 128
    n_pages = page_table.shape[1]
    past = n_pages * PAGE_SIZE

    p = _inproj(x_prompt, jnp.arange(T, dtype=jnp.int32), wts, tm=256)
    mkv = _memkv(mem_prompt.reshape(-1, D), wts).reshape(B, -1, 2 * MEM_Q)
    ckv = _pcompress(p["kct"], wts)
    o_nsa = _pattn_t(p, ckv)
    o_gdn, p_state, conv_tail = _gdn_prompt(p["qkv"], p["z"], p["small"], wts)
    o_mem = _mem_prompt(p["qm"], mkv)
    y_prompt = _merge_ffn(o_nsa.reshape(-1, Q_NSA), o_gdn.reshape(-1, GDN_V), o_mem.reshape(-1, MEM_Q),
                          p["gates"].reshape(-1, N_BRANCH * D), x_prompt.reshape(-1, D), wts, tm=256).reshape(B, T, D)
    p_cmp, p_sel = _rows_5d(p["kct"]), _rows_5d(p["kst"])
    p_win = _rows_5d(p["kwt"][:, :, T - min(WINDOW, T):])
    p_conv = conv_tail[:, 8 - (CONV_W - 1):, :]
    p_mem_kv = mkv.reshape(B, -1, 2, MEM_HEADS, MEM_HD)

    xs = x_sample.reshape(1, nb, D)
    s = _inproj(xs, jnp.full((nb,), past, jnp.int32), wts, tm=nb)
    sq, sqkv, sz, sqm, sgates, ssmall, skct, skst, skwt = (s[k] for k in ("q", "qkv", "z", "qm", "gates", "small", "kct", "kst", "kwt"))
    pool_cmp = _feature_major(cache_cmp_kv)
    pool_sel = _feature_major(cache_sel_kv)
    q3 = sq.reshape(nb, 1, Q_NSA)
    ocmp, idx = _s1(pool_cmp, page_table, q3, wts, q_pos=past)
    n_sel = -(-(past + 1) // SEL_BLOCK)
    n_top = min(SEL_TOP, n_sel)
    idx = idx[:, :NSA_KV_HEADS, :n_top].reshape(nb, NSA_KV_HEADS * n_top)
    blk_per_page = PAGE_SIZE // SEL_BLOCK
    is_new = idx * SEL_BLOCK >= past
    page = jnp.take_along_axis(page_table, jnp.where(is_new, 0, idx // blk_per_page), axis=1)
    meta = jnp.where(is_new, blk_per_page, idx % blk_per_page).astype(jnp.int32)
    tokrow = lambda kt: jnp.transpose(kt[0], (1, 0)).reshape(nb, 1, KV_ROW)
    cwint = _feature_major(cache_win_kv)
    o_nsa_s, swin = _s2(pool_sel, page.astype(jnp.int32), meta, q3, ocmp, ssmall.reshape(nb, 1, SMALL_W), tokrow(skst), tokrow(skwt),
                        skwt[0], cwint, n_top)
    xx4 = jnp.concatenate([jnp.transpose(state_gdn_conv, (1, 0, 2)), sqkv], axis=0)
    o_gdn_s, s_state = _gdn_sample(xx4, sz[0], ssmall[0], state_gdn, wts)
    o_mem_s = _mem_sample(sqm.reshape(nb, 1, MEM_Q), cache_mem_kv)
    y_sample = _merge_ffn(o_nsa_s.reshape(nb, Q_NSA), o_gdn_s.reshape(nb, GDN_V), o_mem_s.reshape(nb, MEM_Q), sgates[0], x_sample.reshape(nb, D),
                          wts, tm=nb).reshape(nb, 1, D)
    s_cmp = jnp.transpose(skct[0], (1, 0)).reshape(nb, 1, 2, NSA_KV_HEADS, NSA_HD)
    s_sel = jnp.transpose(skst[0], (1, 0)).reshape(nb, 1, 2, NSA_KV_HEADS, NSA_HD)
    s_win = _rows_5d(swin)
    s_conv = jnp.transpose(xx4[1:], (1, 0, 2))
    return (y_prompt, y_sample, p_cmp, p_sel, p_win, p_state, p_conv, p_mem_kv, s_cmp, s_sel, s_win, s_state, s_conv)
```

```python
import functools

import jax
import jax.numpy as jnp
from jax import lax
from jax.experimental import pallas as pl
from jax.experimental.pallas import tpu as pltpu

F32 = jnp.float32
BF16 = jnp.bfloat16
HI = lax.Precision.HIGHEST

D_MODEL = 1024
PAGE_SIZE = 128
NSA_HEADS = 8
NSA_KV_HEADS = 2
NSA_HD = 64
NSA_GROUP = NSA_HEADS // NSA_KV_HEADS
NSA_SCALE = NSA_HD ** -0.5
CMP_BLOCK = 32
CMP_STRIDE = 16
CMP_HID = 128
SEL_BLOCK = 64
SEL_TOP = 16
WINDOW = 512
FORCE_SCORE = 1e9
GDN_HEADS = 4
GDN_DK = 128
GDN_DV = 128
CONV_W = 4
GDN_CHUNK = 64
MEM_HEADS = 4
MEM_HD = 128
MEM_SCALE = MEM_HD ** -0.5
D_FF = 4 * D_MODEL
ROPE_THETA = 10000.0
EPS = 1e-6

Q_NSA = NSA_HEADS * NSA_HD
KV_ROW = 2 * NSA_KV_HEADS * NSA_HD
GDN_QK = GDN_HEADS * GDN_DK
GDN_V = GDN_HEADS * GDN_DV
CONV_DIM = 2 * GDN_QK + GDN_V
MEM_Q = MEM_HEADS * MEM_HD
N_BRANCH = 3
IN_SPLITS = (Q_NSA, 3 * KV_ROW, 3 * NSA_HEADS, CONV_DIM, GDN_HEADS, GDN_HEADS, GDN_V, MEM_Q, N_BRANCH * D_MODEL)

LANES = 128
SMALL_W = LANES
G_NSA_OFF, B_OFF, A_OFF = 0, 3 * NSA_HEADS, 3 * NSA_HEADS + GDN_HEADS
TOK_Q, TOK_QKV, TOK_Z, TOK_QM, TOK_GBR, TOK_SMALL, TOK_KTOK = 0, 512, 2048, 2560, 3072, 6144, 6272
TOK_W = TOK_KTOK + 2 * LANES
FT_KV, FT_Q, FT_G = 0, 3 * KV_ROW, 3 * KV_ROW + Q_NSA
G_ROWS = 32
FT_W = FT_G + G_ROWS
V7X_VMEM_LIMIT = 56 * 1024 * 1024


def _cparams(sem):
    return pltpu.CompilerParams(dimension_semantics=sem, vmem_limit_bytes=V7X_VMEM_LIMIT)


def _dot(a, b):
    return jnp.dot(a.astype(BF16), b.astype(BF16), preferred_element_type=F32)


def _dot_nt(a, b):
    return lax.dot_general(a.astype(BF16), b.astype(BF16), (((1,), (1,)), ((), ())), preferred_element_type=F32)


def _dot_hi(a, b):
    return jnp.dot(a, b, precision=HI, preferred_element_type=F32)


def _split_bf16(a):
    hi = a.astype(BF16)
    return hi, (a - hi.astype(F32)).astype(BF16)


def _dot3(a, b):
    ah, al = _split_bf16(a)
    bh, bl = _split_bf16(b)
    d = lambda x, y: jnp.dot(x, y, preferred_element_type=F32)
    return d(ah, bh) + (d(ah, bl) + d(al, bh))


def _rms(x, axis=-1):
    return x * lax.rsqrt(jnp.mean(x * x, axis=axis, keepdims=True) + EPS)


def _const_spec(shape):
    nd = len(shape)
    return pl.BlockSpec(shape, lambda *_: (0,) * nd, pipeline_mode=pl.Buffered(1))


def _inproj_kernel(x_ref, gmix_ref, wtok_ref, wft_ref, qn_ref, qnc_ref, kn_ref, knr_ref, cq_ref, sq_ref, ck_ref, sk_ref, mqn_ref, blk_ref,
                   q_ref, qkv_ref, z_ref, qm_ref, gate_ref, small_ref, kstok0_ref, kstok1_ref, kwtok_ref,
                   kc_ref, ks_ref, kw_ref, qt_ref, gt_ref):
    x = x_ref[0]
    ub = (_rms(x) * gmix_ref[...]).astype(BF16)
    tm = x.shape[0]
    lane = lax.broadcasted_iota(jnp.int32, (tm, LANES), 1)
    lo = lane < NSA_HD
    first_half = (lane % NSA_HD) < (NSA_HD // 2)

    def norm_rope_slab(col0, gain):
        qs = jnp.dot(ub, wtok_ref[:, col0:col0 + LANES], preferred_element_type=F32)
        sq = qs * qs
        ss_lo = jnp.sum(jnp.where(lo, sq, 0.0), axis=-1, keepdims=True)
        ss_hi = jnp.sum(jnp.where(lo, 0.0, sq), axis=-1, keepdims=True)
        r = jnp.where(lo, lax.rsqrt(ss_lo / NSA_HD + EPS), lax.rsqrt(ss_hi / NSA_HD + EPS))
        qs = qs * r * gain
        rot = jnp.where(first_half, -pltpu.roll(qs, LANES - NSA_HD // 2, axis=1), pltpu.roll(qs, NSA_HD // 2, axis=1))
        return qs * cq_ref[...] + rot * sq_ref[...]

    for i in range(Q_NSA // LANES):
        q_ref[0, :, i * LANES:(i + 1) * LANES] = (norm_rope_slab(TOK_Q + i * LANES, qn_ref[...]) * NSA_SCALE).astype(BF16)
    ks = norm_rope_slab(TOK_KTOK, knr_ref[1:2, :])
    kstok0_ref[0] = jnp.where(lo, ks, blk_ref[...]).astype(BF16)
    kstok1_ref[0] = jnp.where(lo, pltpu.roll(ks, NSA_HD, axis=1), blk_ref[...]).astype(BF16)
    kwtok_ref[0] = norm_rope_slab(TOK_KTOK + LANES, knr_ref[2:3, :]).astype(BF16)
    qkv_ref[0] = jnp.dot(ub, wtok_ref[:, TOK_QKV:TOK_Z], preferred_element_type=F32)
    z_ref[0] = jnp.dot(ub, wtok_ref[:, TOK_Z:TOK_QM], preferred_element_type=F32)
    for h in range(MEM_HEADS):
        qm = jnp.dot(ub, wtok_ref[:, TOK_QM + h * MEM_HD:TOK_QM + (h + 1) * MEM_HD], preferred_element_type=F32)
        qm_ref[0, :, h * MEM_HD:(h + 1) * MEM_HD] = (_rms(qm) * mqn_ref[...]).astype(BF16)
    for i in range(N_BRANCH):
        gb = jnp.dot(ub, wtok_ref[:, TOK_GBR + i * D_MODEL:TOK_GBR + (i + 1) * D_MODEL], preferred_element_type=F32)
        gate_ref[0, :, i * D_MODEL:(i + 1) * D_MODEL] = jax.nn.sigmoid(gb)
    small_ref[0] = jnp.dot(ub, wtok_ref[:, TOK_SMALL:TOK_SMALL + SMALL_W], preferred_element_type=F32)
    ft = lax.dot_general(wft_ref[...], ub, (((1,), (1,)), ((), ())), preferred_element_type=F32)
    cos = ck_ref[...]
    sin = sk_ref[...]
    half = NSA_HD // 2

    def norm_rope_rows(row0, gain_col):
        kh = _rms(ft[row0:row0 + NSA_HD, :], axis=0) * gain_col
        x1, x2 = kh[:half], kh[half:]
        return x1 * cos - x2 * sin, x2 * cos + x1 * sin

    for c, out_ref in enumerate((kc_ref, ks_ref, kw_ref)):
        base = FT_KV + c * KV_ROW
        for g in range(NSA_KV_HEADS):
            r1, r2 = norm_rope_rows(base + g * NSA_HD, kn_ref[c])
            out_ref[0, g * NSA_HD:g * NSA_HD + half, :] = r1
            out_ref[0, g * NSA_HD + half:(g + 1) * NSA_HD, :] = r2
        out_ref[0, KV_ROW // 2:, :] = ft[base + KV_ROW // 2:base + KV_ROW, :]
    for h in range(NSA_HEADS):
        r1, r2 = norm_rope_rows(FT_Q + h * NSA_HD, qnc_ref[...])
        qt_ref[0, h * NSA_HD:(h + 1) * NSA_HD, :] = (jnp.concatenate([r1, r2], axis=0) * NSA_SCALE).astype(BF16)
    gt_ref[0] = jax.nn.sigmoid(ft[FT_G:FT_G + G_ROWS, :])


def _inproj(x, pos, wts, tm):
    B, T, _ = x.shape
    half = NSA_HD // 2
    inv = ROPE_THETA ** (-jnp.arange(half, dtype=F32) / half)
    ang = pos.astype(F32)[:, None] * inv[None, :]
    cos, sin = jnp.cos(ang), jnp.sin(ang)
    cq, sq = jnp.tile(cos, (1, LANES // half)), jnp.tile(sin, (1, LANES // half))
    ck, sk = cos.T, sin.T
    tok = lambda w: pl.BlockSpec((1, tm, w), lambda b, t: (b, t, 0))
    ftm = lambda r: pl.BlockSpec((1, r, tm), lambda b, t: (b, 0, t))
    tok_shape = lambda w, dt: jax.ShapeDtypeStruct((B, T, w), dt)
    ft_shape = lambda r, dt: jax.ShapeDtypeStruct((B, r, T), dt)
    names = ("q", "qkv", "z", "qm", "gates", "small", "ks_tok0", "ks_tok1", "kw_tok", "kct", "kst", "kwt", "qt", "gt")
    blk = (pos[:, None] // SEL_BLOCK + NSA_HD == jnp.arange(LANES)[None, :]).astype(F32)
    per_t = pl.BlockSpec((tm, LANES), lambda b, t: (t, 0))
    outs = pl.pallas_call(
        _inproj_kernel,
        grid=(B, T // tm),
        in_specs=[tok(D_MODEL), _const_spec((1, D_MODEL)), _const_spec((D_MODEL, TOK_W)), _const_spec((FT_W, D_MODEL)),
                  _const_spec((1, LANES)), _const_spec((NSA_HD, 1)), _const_spec((3, NSA_HD, 1)), _const_spec((3, LANES)),
                  per_t, per_t,
                  pl.BlockSpec((half, tm), lambda b, t: (0, t)), pl.BlockSpec((half, tm), lambda b, t: (0, t)),
                  _const_spec((1, MEM_HD)), per_t],
        out_specs=[tok(Q_NSA), tok(CONV_DIM), tok(GDN_V), tok(MEM_Q), tok(N_BRANCH * D_MODEL), tok(SMALL_W),
                   tok(LANES), tok(LANES), tok(LANES), ftm(KV_ROW), ftm(KV_ROW), ftm(KV_ROW), ftm(Q_NSA), ftm(G_ROWS)],
        out_shape=[tok_shape(Q_NSA, BF16), tok_shape(CONV_DIM, F32), tok_shape(GDN_V, F32), tok_shape(MEM_Q, BF16),
                   tok_shape(N_BRANCH * D_MODEL, F32), tok_shape(SMALL_W, F32),
                   tok_shape(LANES, BF16), tok_shape(LANES, BF16), tok_shape(LANES, BF16),
                   ft_shape(KV_ROW, F32), ft_shape(KV_ROW, F32), ft_shape(KV_ROW, F32), ft_shape(Q_NSA, BF16), ft_shape(G_ROWS, F32)],
        compiler_params=_cparams(("parallel", "parallel")),
        name="inproj",
    )(x, wts["g_mix"], wts["w_tok"], wts["w_ft"], wts["qn"], wts["qn_col"], wts["kn"], wts["kn_row"], cq, sq, ck, sk, wts["mqn"], blk)
    return dict(zip(names, outs))


def _memkv_kernel(m_ref, g_ref, w_ref, kn_ref, o_ref):
    u = _rms(m_ref[...]) * g_ref[...]
    kv = _dot(u, w_ref[...])
    for h in range(MEM_HEADS):
        sl = slice(h * MEM_HD, (h + 1) * MEM_HD)
        o_ref[:, sl] = _rms(kv[:, sl]) * kn_ref[...]
    o_ref[:, MEM_Q:] = kv[:, MEM_Q:]


def _memkv(mem2d, wts, tm=512):
    n = mem2d.shape[0]
    tm = min(tm, n)
    return pl.pallas_call(
        _memkv_kernel, grid=(n // tm,),
        in_specs=[pl.BlockSpec((tm, D_MODEL), lambda i: (i, 0)), _const_spec((1, D_MODEL)),
                  _const_spec((D_MODEL, 2 * MEM_Q)), _const_spec((1, MEM_HD))],
        out_specs=pl.BlockSpec((tm, 2 * MEM_Q), lambda i: (i, 0)),
        out_shape=jax.ShapeDtypeStruct((n, 2 * MEM_Q), F32),
        compiler_params=_cparams(("parallel",)), name="memkv",
    )(mem2d, wts["g_mem"], wts["w_mem_kv"], wts["mkn"])


CHUNK_PITCH = CMP_STRIDE + 8


def _transpose_pages(get_page, pages, xs_ref):
    cpp = PAGE_SIZE // CMP_STRIDE
    for p in pages:
        xt = get_page(p).T
        for c in range(cpp):
            r0 = (p * cpp + c) * CHUNK_PITCH
            xs_ref[0, r0:r0 + CMP_STRIDE, :] = xt[c * CMP_STRIDE:(c + 1) * CMP_STRIDE, :LANES]
            xs_ref[1, r0:r0 + CMP_STRIDE, :] = xt[c * CMP_STRIDE:(c + 1) * CMP_STRIDE, LANES:]


def _compress(n_chunks, xs_ref, hs_ref, wbd_ref, pe_ref, w1_ref, w2_ref, between=None):
    hs_ref[n_chunks:, :] = jnp.zeros((8, CMP_HID), F32)
    parts = []
    for kv in range(2):
        lhs = jnp.concatenate([xs_ref[kv, pl.ds(s, n_chunks, stride=CHUNK_PITCH), :].astype(BF16) for s in range(CMP_STRIDE)], axis=1)
        if between is not None:
            between(kv)
        h = jnp.dot(lhs, wbd_ref[kv], preferred_element_type=F32)
        pe_h = jnp.dot(pe_ref[kv].astype(BF16), w1_ref[kv], preferred_element_type=F32)[0:1]
        for g in range(NSA_KV_HEADS):
            h0 = h[:, g * 2 * CMP_HID:g * 2 * CMP_HID + CMP_HID]
            hs_ref[0:n_chunks, :] = h[:, g * 2 * CMP_HID + CMP_HID:(g + 1) * 2 * CMP_HID]
            hh = h0 + hs_ref[pl.ds(1, n_chunks), :] + pe_h
            parts.append(jnp.dot(jax.nn.gelu(hh).astype(BF16), w2_ref[kv], preferred_element_type=F32))
    return jnp.concatenate(parts, axis=1)


def _masked_softmax(s, allow):
    s = jnp.where(allow, s, -1e30)
    e = jnp.exp(s - jnp.max(s, axis=-1, keepdims=True))
    p = e / jnp.sum(e, axis=-1, keepdims=True)
    return jnp.where(allow, p, 0.0)


def _cmp_probs(qg, kc, tpos, n_cmp):
    n_chunks = kc.shape[0]
    s = _dot_nt(qg, kc)
    i = lax.broadcasted_iota(jnp.int32, (1, n_chunks), 1)
    allow = jnp.where(i < n_cmp, i * CMP_STRIDE + CMP_BLOCK - 1, jnp.int32(2 ** 30)) <= tpos
    return _masked_softmax(s, allow)


def _select_blocks(psum, tpos, n_cmp, n_sel, ns_pad):
    n_chunks = psum.shape[1]
    ci = lax.broadcasted_iota(jnp.int32, (n_chunks, ns_pad), 0)
    sj = lax.broadcasted_iota(jnp.int32, (n_chunks, ns_pad), 1)
    hit = (ci * CMP_STRIDE < (sj + 1) * SEL_BLOCK) & (ci * CMP_STRIDE + CMP_BLOCK > sj * SEL_BLOCK) & (ci < n_cmp) & (sj < n_sel)
    imp = _dot(psum, jnp.where(hit, 1.0, 0.0))
    jj = lax.broadcasted_iota(jnp.int32, (1, ns_pad), 1)
    imp = jnp.where((jj * SEL_BLOCK <= tpos) & (jj < n_sel), imp, -jnp.inf)
    imp = jnp.where((jj == 0) | (jj == tpos // SEL_BLOCK), FORCE_SCORE, imp)
    cnt = jnp.zeros(imp.shape, F32)
    for j in range(n_sel):
        col = imp[:, j:j + 1]
        later = jnp.where(jj > j, 1.0, 0.0)
        cnt = cnt + jnp.where(col > imp, 1.0, jnp.where(col == imp, later, 0.0))
    return jnp.where(cnt < min(SEL_TOP, n_sel), 1.0, 0.0)


def _head_rows(q, g):
    return jnp.concatenate([q[:, (g * NSA_GROUP + j) * NSA_HD:(g * NSA_GROUP + j + 1) * NSA_HD] for j in range(NSA_GROUP)], axis=0)


def _gate_merge(gsig, o_cmp, o_sel, o_win, tq):
    cols = []
    for h in range(NSA_HEADS):
        g, j = divmod(h, NSA_GROUP)
        rows = slice(j * tq, (j + 1) * tq)
        cols.append(gsig[:, 3 * h:3 * h + 1] * o_cmp[g][rows] + gsig[:, 3 * h + 1:3 * h + 2] * o_sel[g][rows]
                    + gsig[:, 3 * h + 2:3 * h + 3] * o_win[g][rows])
    return jnp.concatenate(cols, axis=1)


def _pcompress_kernel(kc_ref, wbd_ref, pe_ref, w1_ref, w2_ref, o_ref, xs_ref, hs_ref):
    n_pages = kc_ref.shape[2] // PAGE_SIZE
    _transpose_pages(lambda p: kc_ref[0, :, p * PAGE_SIZE:(p + 1) * PAGE_SIZE], range(n_pages), xs_ref)
    o_ref[0] = _compress(n_pages * PAGE_SIZE // CMP_STRIDE, xs_ref, hs_ref, wbd_ref, pe_ref, w1_ref, w2_ref)


def _pcompress(kct, wts):
    B, _, T = kct.shape
    n_chunks = T // CMP_STRIDE
    return pl.pallas_call(
        _pcompress_kernel, grid=(B,),
        in_specs=[pl.BlockSpec((1, KV_ROW, T), lambda b: (b, 0, 0)), _const_spec(wts["wbd"].shape), _const_spec(wts["pe"].shape),
                  _const_spec(wts["w1"].shape), _const_spec(wts["w2"].shape)],
        out_specs=pl.BlockSpec((1, n_chunks, KV_ROW), lambda b: (b, 0, 0)),
        out_shape=jax.ShapeDtypeStruct((B, n_chunks, KV_ROW), F32),
        scratch_shapes=[pltpu.VMEM((2, n_chunks * CHUNK_PITCH, LANES), F32), pltpu.VMEM((n_chunks + 8, CMP_HID), F32)],
        compiler_params=_cparams(("parallel",)), name="prompt_compress",
    )(kct, wts["wbd"], wts["pe"], wts["w1"], wts["w2"])


def _softmax_pv(q, kt, vt, bias):
    s = _dot(q, kt) + bias
    e = jnp.exp(s - jnp.max(s, axis=-1, keepdims=True))
    return _dot_nt(e, vt) / jnp.sum(e, axis=-1, keepdims=True)


def _pattn_kernel(q_ref, ckv_ref, ks_ref, kw_ref, small_ref, o_ref, osel_ref, *, T, tq, n_ext):
    n_chunks = T // CMP_STRIDE
    n_cmp = n_chunks - CMP_BLOCK // CMP_STRIDE + 1
    n_sel = -(-T // SEL_BLOCK)
    ns_pad = -(-n_sel // LANES) * LANES
    span = WINDOW + tq
    qi = pl.program_id(1)
    q0 = qi * tq
    q = q_ref[0]
    ckv = ckv_ref[0]
    tcol = q0 + lax.broadcasted_iota(jnp.int32, (tq, 1), 0)
    wstart = pl.multiple_of(jnp.maximum(q0 - WINDOW, 0), LANES)
    keyw = wstart + lax.broadcasted_iota(jnp.int32, (1, span), 1)
    bias_w = jnp.where(keyw <= tcol, jnp.where(keyw > tcol - WINDOW, 0.0, -1e30), -1e30)
    gsig = jax.nn.sigmoid(small_ref[0][:, G_NSA_OFF:G_NSA_OFF + 3 * NSA_HEADS])
    tiles_per_ext = T // tq // n_ext
    o_cmp, o_win = [], []
    for g in range(NSA_KV_HEADS):
        ksl = slice(g * NSA_HD, (g + 1) * NSA_HD)
        vsl = slice(KV_ROW // 2 + g * NSA_HD, KV_ROW // 2 + (g + 1) * NSA_HD)
        qs = [q[:, (g * NSA_GROUP + j) * NSA_HD:(g * NSA_GROUP + j + 1) * NSA_HD] for j in range(NSA_GROUP)]
        psum = None
        for j in range(NSA_GROUP):
            p = _cmp_probs(qs[j], ckv[:, ksl], tcol, n_cmp)
            o_cmp.append(_dot(p, ckv[:, vsl]))
            psum = p if psum is None else psum + p
        sel = _select_blocks(psum, tcol, n_cmp, n_sel, ns_pad).astype(BF16)
        kwg = kw_ref[0, ksl, pl.ds(wstart, span)]
        vwg = kw_ref[0, vsl, pl.ds(wstart, span)]
        for j in range(NSA_GROUP):
            o_win.append(_softmax_pv(qs[j], kwg, vwg, bias_w))
        for v in range(n_ext):
            kext = (v + 1) * (T // n_ext)

            @pl.when((qi >= v * tiles_per_ext) & (qi < (v + 1) * tiles_per_ext))
            def _():
                es = lax.broadcasted_iota(jnp.int32, (ns_pad, kext), 0)
                ek = lax.broadcasted_iota(jnp.int32, (ns_pad, kext), 1)
                expand = jnp.where(ek // SEL_BLOCK == es, 1.0, 0.0).astype(BF16)
                keysel = jnp.dot(sel, expand, preferred_element_type=F32)
                keyi = lax.broadcasted_iota(jnp.int32, (1, kext), 1)
                bias = jnp.where(keyi <= tcol, jnp.where(keysel > 0.5, 0.0, -1e30), -1e30)
                for j in range(NSA_GROUP):
                    h = g * NSA_GROUP + j
                    osel_ref[:, h * NSA_HD:(h + 1) * NSA_HD] = _softmax_pv(qs[j], ks_ref[0, ksl, 0:kext], ks_ref[0, vsl, 0:kext], bias)
    cols = []
    for h in range(NSA_HEADS):
        cols.append(gsig[:, 3 * h:3 * h + 1] * o_cmp[h] + gsig[:, 3 * h + 1:3 * h + 2] * osel_ref[:, h * NSA_HD:(h + 1) * NSA_HD]
                    + gsig[:, 3 * h + 2:3 * h + 3] * o_win[h])
    o_ref[0] = jnp.concatenate(cols, axis=1)


def _pattn(q, ckv, kst, kwt, small, tq=128, n_ext=4):
    B, T, _ = q.shape
    n_chunks = T // CMP_STRIDE
    assert T % (n_ext * tq) == 0
    full = lambda b, t: (b, 0, 0)
    return pl.pallas_call(
        functools.partial(_pattn_kernel, T=T, tq=tq, n_ext=n_ext), grid=(B, T // tq),
        in_specs=[pl.BlockSpec((1, tq, Q_NSA), lambda b, t: (b, t, 0)), pl.BlockSpec((1, n_chunks, KV_ROW), full),
                  pl.BlockSpec((1, KV_ROW, T), full), pl.BlockSpec((1, KV_ROW, T), full),
                  pl.BlockSpec((1, tq, SMALL_W), lambda b, t: (b, t, 0))],
        out_specs=pl.BlockSpec((1, tq, Q_NSA), lambda b, t: (b, t, 0)),
        out_shape=jax.ShapeDtypeStruct((B, T, Q_NSA), F32),
        scratch_shapes=[pltpu.VMEM((tq, Q_NSA), F32)],
        compiler_params=_cparams(("parallel", "parallel")), name="prompt_nsa_attn",
    )(q, ckv, kst, kwt, small)


KEY_TILE = 256


def _select_blocks_t(psum_t, tl, n_cmp, n_sel, ns_rows):
    n_chunks = psum_t.shape[0]
    sj = lax.broadcasted_iota(jnp.int32, (ns_rows, n_chunks), 0)
    ci = lax.broadcasted_iota(jnp.int32, (ns_rows, n_chunks), 1)
    hit = (ci * CMP_STRIDE < (sj + 1) * SEL_BLOCK) & (ci * CMP_STRIDE + CMP_BLOCK > sj * SEL_BLOCK) & (ci < n_cmp) & (sj < n_sel)
    imp = _dot(jnp.where(hit, 1.0, 0.0), psum_t)
    jj = lax.broadcasted_iota(jnp.int32, (ns_rows, 1), 0)
    imp = jnp.where((jj * SEL_BLOCK <= tl) & (jj < n_sel), imp, -jnp.inf)
    imp = jnp.where((jj == 0) | (jj == tl // SEL_BLOCK), FORCE_SCORE, imp)
    cnt = jnp.zeros(imp.shape, F32)
    for j in range(n_sel):
        row = imp[j:j + 1, :]
        later = jnp.where(jj > j, 1.0, 0.0)
        cnt = cnt + jnp.where(row > imp, 1.0, jnp.where(row == imp, later, 0.0))
    return jnp.where(cnt < min(SEL_TOP, n_sel), 1.0, 0.0)


V_AUG = NSA_HD + 8


def _col_attend(n_tiles, tile_w, k_tile, q_list, v_aug, bias_tile, s_ref, e_ref):
    outs = []
    n_keys = n_tiles * tile_w
    n_heads = len(q_list)
    m_prev = None
    for st in range(n_heads + 1):
        m8 = None
        for i in range(n_tiles):
            rows = slice(i * tile_w, (i + 1) * tile_w)
            if st < n_heads:
                s = jnp.dot(k_tile(i), q_list[st], preferred_element_type=F32)
                bt = bias_tile(i)
                if bt is not None:
                    s = s + bt
                s_ref[st, rows, :] = s
                for r in range(tile_w // 8):
                    m8 = s[8 * r:8 * r + 8] if m8 is None else jnp.maximum(m8, s[8 * r:8 * r + 8])
            if st >= 1:
                e_ref[st - 1, rows, :] = jnp.exp(s_ref[st - 1, rows, :] - m_prev).astype(BF16)
        if st >= 1:
            a = jnp.dot(v_aug, e_ref[st - 1, 0:n_keys, :], preferred_element_type=F32)
            outs.append(a[:NSA_HD] / a[NSA_HD:NSA_HD + 1])
        if st < n_heads:
            m_prev = jnp.max(m8, axis=0, keepdims=True)
    return outs


def _ones_rows(v_t):
    n = v_t.shape[1]
    r = lax.broadcasted_iota(jnp.int32, (V_AUG - NSA_HD, n), 0)
    return jnp.concatenate([v_t, jnp.where(r == 0, 1.0, 0.0)], axis=0).astype(BF16)


def _pattn_t_kernel(qt_ref, gt_ref, ckv_ref, kstok0_ref, kstok1_ref, vst_ref, kwtok_ref, vwt_ref, o_ref, osel_ref, s_ref, e_ref,
                    *, T, tq, n_ext):
    kstok_refs = (kstok0_ref, kstok1_ref)
    n_chunks = T // CMP_STRIDE
    n_cmp = n_chunks - CMP_BLOCK // CMP_STRIDE + 1
    n_sel = -(-T // SEL_BLOCK)
    ns_rows = -(-n_sel // 8) * 8
    span = WINDOW + tq
    qi = pl.program_id(1)
    q0 = qi * tq
    qt = qt_ref[0]
    gt = gt_ref[0]
    ckv = ckv_ref[0]
    tl = q0 + lax.broadcasted_iota(jnp.int32, (1, tq), 1)
    wstart = pl.multiple_of(jnp.maximum(q0 - WINDOW, 0), LANES)
    ci = lax.broadcasted_iota(jnp.int32, (n_chunks, 1), 0)
    bias_c = jnp.where(jnp.where(ci < n_cmp, ci * CMP_STRIDE + CMP_BLOCK - 1, jnp.int32(2 ** 30)) <= tl, 0.0, -1e30)
    zeros_half = jnp.zeros((NSA_HD, tq), BF16)
    tiles_per_ext = T // tq // n_ext
    kw = wstart + lax.broadcasted_iota(jnp.int32, (span, 1), 0)
    bias_w = jnp.where(kw <= tl, jnp.where(kw > tl - WINDOW, 0.0, -1e30), -1e30)
    o_cmp, o_win, sel_bias, q_hs = [], [], [], []
    for g in range(NSA_KV_HEADS):
        ksl = slice(g * NSA_HD, (g + 1) * NSA_HD)
        vsl = slice(KV_ROW // 2 + g * NSA_HD, KV_ROW // 2 + (g + 1) * NSA_HD)
        q_h = [qt[(g * NSA_GROUP + j) * NSA_HD:(g * NSA_GROUP + j + 1) * NSA_HD, :] for j in range(NSA_GROUP)]
        q_pad = [jnp.concatenate([qh, zeros_half] if g == 0 else [zeros_half, qh], axis=0) for qh in q_h]
        q_hs.append(q_h)
        kc = ckv[:, ksl].astype(BF16)
        vc_t = ckv[:, vsl].T.astype(BF16)
        psum = None
        for j in range(NSA_GROUP):
            s = jnp.dot(kc, q_h[j], preferred_element_type=F32) + bias_c
            e = jnp.exp(s - jnp.max(s, axis=0, keepdims=True))
            p = jnp.where(bias_c == 0.0, e / jnp.sum(e, axis=0, keepdims=True), 0.0)
            o_cmp.append(jnp.dot(vc_t, p.astype(BF16), preferred_element_type=F32))
            psum = p if psum is None else psum + p
        sel = _select_blocks_t(psum, tl, n_cmp, n_sel, ns_rows)
        sel_bias.append(jnp.concatenate([(sel - 1.0) * 1e30, jnp.zeros((LANES - NSA_HD - ns_rows, tq), F32)], axis=0).astype(BF16))
        vw_aug = _ones_rows(vwt_ref[0, ksl, pl.ds(wstart, span)])
        o_win += _col_attend(
            span // LANES, LANES, lambda i: kwtok_ref[0, pl.ds(wstart + i * LANES, LANES), :], q_pad, vw_aug,
            lambda i: bias_w[i * LANES:(i + 1) * LANES, :], s_ref, e_ref)

    ext_w = T // n_ext
    for v in range(n_ext):
        kext = (v + 1) * ext_w
        n_tiles = kext // KEY_TILE
        first_diag = n_tiles - ext_w // KEY_TILE

        @pl.when((qi >= v * tiles_per_ext) & (qi < (v + 1) * tiles_per_ext))
        def _():
            keyd = (kext - ext_w) + lax.broadcasted_iota(jnp.int32, (ext_w, 1), 0)
            causal = jnp.where(keyd <= tl, 0.0, -1e30)
            for g in range(NSA_KV_HEADS):
                ksl = slice(g * NSA_HD, (g + 1) * NSA_HD)
                vs_aug = _ones_rows(vst_ref[0, ksl, 0:kext])
                q_aug = [jnp.concatenate([qh, sel_bias[g]], axis=0) for qh in q_hs[g]]
                outs = _col_attend(
                    n_tiles, KEY_TILE, lambda i: kstok_refs[g][0, i * KEY_TILE:(i + 1) * KEY_TILE, :], q_aug, vs_aug,
                    lambda i: causal[(i - first_diag) * KEY_TILE:(i - first_diag + 1) * KEY_TILE, :] if i >= first_diag else None,
                    s_ref, e_ref)
                for j in range(NSA_GROUP):
                    h = g * NSA_GROUP + j
                    osel_ref[h * NSA_HD:(h + 1) * NSA_HD, :] = outs[j]

    heads = [gt[3 * h:3 * h + 1, :] * o_cmp[h] + gt[3 * h + 1:3 * h + 2, :] * osel_ref[h * NSA_HD:(h + 1) * NSA_HD, :]
             + gt[3 * h + 2:3 * h + 3, :] * o_win[h] for h in range(NSA_HEADS)]
    o_ref[0] = jnp.concatenate(heads, axis=0).T


def _pattn_t(p, ckv, tq=256, n_ext=4):
    B, _, T = p["qt"].shape
    n_chunks = T // CMP_STRIDE
    assert T % (n_ext * tq) == 0 and (T // n_ext) % KEY_TILE == 0 and -(-T // SEL_BLOCK) <= LANES - NSA_HD
    full = lambda b, t: (b, 0, 0)
    vhalf = pl.BlockSpec((1, KV_ROW // 2, T), lambda b, t: (b, 1, 0))
    return pl.pallas_call(
        functools.partial(_pattn_t_kernel, T=T, tq=tq, n_ext=n_ext), grid=(B, T // tq),
        in_specs=[pl.BlockSpec((1, Q_NSA, tq), lambda b, t: (b, 0, t)), pl.BlockSpec((1, G_ROWS, tq), lambda b, t: (b, 0, t)),
                  pl.BlockSpec((1, n_chunks, KV_ROW), full), pl.BlockSpec((1, T, LANES), full), pl.BlockSpec((1, T, LANES), full), vhalf,
                  pl.BlockSpec((1, T, LANES), full), vhalf],
        out_specs=pl.BlockSpec((1, tq, Q_NSA), lambda b, t: (b, t, 0)),
        out_shape=jax.ShapeDtypeStruct((B, T, Q_NSA), F32),
        scratch_shapes=[pltpu.VMEM((Q_NSA, tq), F32), pltpu.VMEM((NSA_GROUP, T, tq), F32), pltpu.VMEM((NSA_GROUP, T, tq), BF16)],
        compiler_params=_cparams(("parallel", "parallel")), name="prompt_nsa_attn",
    )(p["qt"], p["gt"], ckv, p["ks_tok0"], p["ks_tok1"], p["kst"], p["kw_tok"], p["kwt"])


def _s1_kernel(pt_ref, *refs, n_pages, q_pos):
    pages = refs[:n_pages]
    q_ref, wbd_ref, pe_ref, w1_ref, w2_ref, ocmp_ref, idx_ref, xs0_ref, xs1_ref, hs_ref = refs[n_pages:]
    b = pl.program_id(0)

    @pl.when(b == 0)
    def _():
        xs1_ref[...] = jnp.zeros(xs1_ref.shape, F32)

    for parity, (xs_w, xs_r) in enumerate(((xs0_ref, xs1_ref), (xs1_ref, xs0_ref))):
        @pl.when(b % 2 == parity)
        def _():
            _s1_body(pages, q_ref, wbd_ref, pe_ref, w1_ref, w2_ref, ocmp_ref, idx_ref, xs_w, xs_r, hs_ref, n_pages=n_pages, q_pos=q_pos)


def _s1_body(pages, q_ref, wbd_ref, pe_ref, w1_ref, w2_ref, ocmp_ref, idx_ref, xs_w, xs_r, hs_ref, *, n_pages, q_pos):
    n_chunks = n_pages * PAGE_SIZE // CMP_STRIDE
    n_cmp = n_chunks - CMP_BLOCK // CMP_STRIDE + 1
    n_sel = -(-(q_pos + 1) // SEL_BLOCK)
    ns_pad = -(-n_sel // LANES) * LANES
    per = -(-n_pages // 2)

    def between(k):
        _transpose_pages(lambda p: pages[p][0], range(k * per, min((k + 1) * per, n_pages)), xs_w)

    ckv = _compress(n_chunks, xs_r, hs_ref, wbd_ref, pe_ref, w1_ref, w2_ref, between=between)
    qrow = q_ref[0].astype(F32)
    q8 = jnp.concatenate([qrow[:, h * NSA_HD:(h + 1) * NSA_HD] for h in range(NSA_HEADS)], axis=0)
    row = lax.broadcasted_iota(jnp.int32, (NSA_HEADS, 1), 0)
    tpos = jnp.full((NSA_HEADS, 1), q_pos, jnp.int32)
    o_all = jnp.zeros((NSA_HEADS, NSA_HD), F32)
    psum = jnp.zeros((NSA_HEADS, n_chunks), F32)
    for g in range(NSA_KV_HEADS):
        kc = ckv[:, g * NSA_HD:(g + 1) * NSA_HD]
        vc = ckv[:, KV_ROW // 2 + g * NSA_HD:KV_ROW // 2 + (g + 1) * NSA_HD]
        p = _cmp_probs(q8, kc, tpos, n_cmp)
        mine = (row // NSA_GROUP) == g
        o_all = jnp.where(mine, _dot(p, vc), o_all)
        pg = jnp.sum(jnp.where(mine, p, 0.0), axis=0, keepdims=True)
        psum = jnp.where(row == g, pg, psum)
    sel = _select_blocks(psum, tpos, n_cmp, n_sel, ns_pad)
    a = lax.broadcasted_iota(jnp.int32, (ns_pad, ns_pad), 0)
    b = lax.broadcasted_iota(jnp.int32, (ns_pad, ns_pad), 1)
    before = jnp.dot(sel.astype(BF16), jnp.where(a < b, 1.0, 0.0).astype(BF16), preferred_element_type=F32)
    jj = lax.broadcasted_iota(jnp.int32, (1, ns_pad), 1).astype(F32)
    lane = lax.broadcasted_iota(jnp.int32, (1, LANES), 1)
    idx = jnp.zeros((NSA_HEADS, LANES), F32)
    for k in range(min(SEL_TOP, n_sel)):
        ik = jnp.sum(jnp.where((sel > 0.5) & (before == k), jj, 0.0), axis=1, keepdims=True)
        idx = jnp.where(lane == k, ik, idx)
    idx_ref[0] = idx.astype(jnp.int32)
    ocmp_ref[0] = jnp.concatenate([o_all, jnp.zeros((NSA_HEADS, LANES - NSA_HD), F32)], axis=1)


def _s1(pool_t, page_table, q3, wts, q_pos):
    B, n_pages = page_table.shape
    n_chunks = n_pages * PAGE_SIZE // CMP_STRIDE

    def page_spec(j):
        return pl.BlockSpec((1, KV_ROW, PAGE_SIZE), lambda b, pt: (pt[jnp.minimum(b, B - 1), j], 0, 0))

    prev = lambda b, pt: (jnp.maximum(b - 1, 0), 0, 0)
    cst = lambda shape: pl.BlockSpec(shape, lambda b, pt: (0,) * len(shape), pipeline_mode=pl.Buffered(1))
    xs_shape = pltpu.VMEM((2, n_chunks * CHUNK_PITCH, LANES), F32)
    grid_spec = pltpu.PrefetchScalarGridSpec(
        num_scalar_prefetch=1, grid=(B + 1,),
        in_specs=[page_spec(j) for j in range(n_pages)]
        + [pl.BlockSpec((1, 1, Q_NSA), prev), cst(wts["wbd"].shape), cst(wts["pe"].shape),
           cst(wts["w1"].shape), cst(wts["w2"].shape)],
        out_specs=[pl.BlockSpec((1, NSA_HEADS, LANES), prev), pl.BlockSpec((1, NSA_HEADS, LANES), prev)],
        scratch_shapes=[xs_shape, xs_shape, pltpu.VMEM((n_chunks + 8, CMP_HID), F32)],
    )
    return pl.pallas_call(
        functools.partial(_s1_kernel, n_pages=n_pages, q_pos=q_pos), grid_spec=grid_spec,
        out_shape=[jax.ShapeDtypeStruct((B, NSA_HEADS, LANES), F32), jax.ShapeDtypeStruct((B, NSA_HEADS, LANES), jnp.int32)],
        compiler_params=_cparams(("arbitrary",)), name="sample_compress_select",
    )(page_table, *([pool_t] * n_pages), q3, wts["wbd"], wts["pe"], wts["w1"], wts["w2"])


def _s2_kernel(phys_ref, meta_ref, *refs, n_top, win_len):
    n_blk = NSA_KV_HEADS * n_top
    pages = refs[:n_blk]
    q_ref, ocmp_ref, small_ref, nsel_ref, nwin_ref, nwint_ref, cwin_ref, o_ref, swin_ref = refs[n_blk:]
    b = pl.program_id(0)
    qrow = q_ref[0].astype(F32)
    q8 = jnp.concatenate([qrow[:, h * NSA_HD:(h + 1) * NSA_HD] for h in range(NSA_HEADS)], axis=0)
    row = lax.broadcasted_iota(jnp.int32, (NSA_HEADS, 1), 0)
    lane = lax.broadcasted_iota(jnp.int32, (1, PAGE_SIZE), 1)
    nsel = nsel_ref[0]
    nwin = nwin_ref[0]
    cwin = cwin_ref[0]
    r = lax.broadcasted_iota(jnp.int32, (1, win_len), 1)
    allow_w = (r > win_len - WINDOW) & (r <= win_len)
    o_sel = jnp.zeros((NSA_HEADS, NSA_HD), F32)
    o_win = jnp.zeros((NSA_HEADS, NSA_HD), F32)
    for g in range(NSA_KV_HEADS):
        mine = (row // NSA_GROUP) == g
        ksl = slice(g * NSA_HD, (g + 1) * NSA_HD)
        vsl = slice(KV_ROW // 2 + g * NSA_HD, KV_ROW // 2 + (g + 1) * NSA_HD)
        kts, vts, masks = [], [], []
        has_new = jnp.zeros((1, 1), F32)
        for k in range(n_top):
            m = meta_ref[b, g * n_top + k]
            kts.append(pages[g * n_top + k][0, ksl, :])
            vts.append(pages[g * n_top + k][0, vsl, :])
            masks.append((lane // SEL_BLOCK) == m)
            has_new = has_new + jnp.where(m == 2, 1.0, 0.0)
        kt = jnp.concatenate(kts, axis=1)
        vt = jnp.concatenate(vts, axis=1)
        allow = jnp.concatenate(masks, axis=1)
        s = jnp.where(allow, _dot(q8, kt), -1e30)
        s_new = jnp.where(has_new > 0.5, jnp.sum(q8 * nsel[:, ksl], axis=1, keepdims=True), -1e30)
        mx = jnp.maximum(jnp.max(s, axis=1, keepdims=True), s_new)
        e = jnp.where(allow, jnp.exp(s - mx), 0.0)
        e_new = jnp.where(has_new > 0.5, jnp.exp(s_new - mx), 0.0)
        den = jnp.sum(e, axis=1, keepdims=True) + e_new
        og = (_dot_nt(e, vt) + e_new * nsel[:, vsl]) / den
        o_sel = jnp.where(mine, og, o_sel)
        s = jnp.where(allow_w, _dot(q8, cwin[ksl, :]), -1e30)
        s_new = jnp.sum(q8 * nwin[:, ksl], axis=1, keepdims=True)
        mx = jnp.maximum(jnp.max(s, axis=1, keepdims=True), s_new)
        e = jnp.where(allow_w, jnp.exp(s - mx), 0.0)
        e_new = jnp.exp(s_new - mx)
        den = jnp.sum(e, axis=1, keepdims=True) + e_new
        og = (_dot_nt(e, cwin[vsl, :]) + e_new * nwin[:, vsl]) / den
        o_win = jnp.where(mine, og, o_win)
    o_cmp = ocmp_ref[0][:, :NSA_HD]
    gsig = jax.nn.sigmoid(small_ref[0][:, G_NSA_OFF:G_NSA_OFF + 3 * NSA_HEADS])
    cols = []
    for h in range(NSA_HEADS):
        cols.append(gsig[:, 3 * h:3 * h + 1] * o_cmp[h:h + 1] + gsig[:, 3 * h + 1:3 * h + 2] * o_sel[h:h + 1]
                    + gsig[:, 3 * h + 2:3 * h + 3] * o_win[h:h + 1])
    o_ref[0] = jnp.concatenate(cols, axis=1)
    blane = lax.broadcasted_iota(jnp.int32, (1, nwint_ref.shape[1]), 1)
    newcol = jnp.sum(jnp.where(blane == b, nwint_ref[...], 0.0), axis=1, keepdims=True)
    swin_ref[0] = jnp.concatenate([cwin[:, 1:], newcol], axis=1)


def _s2(pool_t, phys, meta, q3, ocmp, small3, nsel3, nwin3, nwint, cwint, n_top):
    B = q3.shape[0]
    win_len = cwint.shape[2]
    n_blk = NSA_KV_HEADS * n_top

    def page_spec(j):
        return pl.BlockSpec((1, KV_ROW, PAGE_SIZE), lambda b, ph, me: (ph[b, j], 0, 0))

    per_b = lambda shape: pl.BlockSpec(shape, lambda b, ph, me: (b,) + (0,) * (len(shape) - 1))
    grid_spec = pltpu.PrefetchScalarGridSpec(
        num_scalar_prefetch=2, grid=(B,),
        in_specs=[page_spec(j) for j in range(n_blk)]
        + [per_b((1, 1, Q_NSA)), per_b((1, NSA_HEADS, LANES)), per_b((1, 1, SMALL_W)), per_b((1, 1, KV_ROW)), per_b((1, 1, KV_ROW)),
           pl.BlockSpec(nwint.shape, lambda b, ph, me: (0, 0)), per_b((1, KV_ROW, win_len))],
        out_specs=[per_b((1, 1, Q_NSA)), per_b((1, KV_ROW, win_len))],
    )
    return pl.pallas_call(
        functools.partial(_s2_kernel, n_top=n_top, win_len=win_len), grid_spec=grid_spec,
        out_shape=[jax.ShapeDtypeStruct((B, 1, Q_NSA), F32), jax.ShapeDtypeStruct((B, KV_ROW, win_len), F32)],
        compiler_params=_cparams(("parallel",)), name="sample_sel_win_attn",
    )(phys, meta, *([pool_t] * n_blk), q3, ocmp, small3, nsel3, nwin3, nwint, cwint)


def _gdn_gates(b_raw, a_raw, alog, dtb):
    beta = jax.nn.sigmoid(b_raw)
    g = -jnp.exp(alog) * jax.nn.softplus(a_raw + dtb)
    return beta, g


def _l2n(x):
    return x * lax.rsqrt(jnp.sum(x * x, axis=-1, keepdims=True) + EPS)


def _gdn_prompt_kernel(qkv_ref, z_ref, small_ref, smallt_ref, cw_ref, alog_ref, dtb_ref, alogt_ref, dtbt_ref, on_ref,
                       o_ref, sfin_ref, conv_ref, s_ref, xx_ref):
    ci = pl.program_id(1)
    tc = qkv_ref.shape[1]
    C = GDN_CHUNK

    @pl.when(ci == 0)
    def _():
        s_ref[...] = jnp.zeros(s_ref.shape, F32)
        xx_ref[0:8, :] = jnp.zeros((8, CONV_DIM), F32)

    xx_ref[8:8 + tc, :] = qkv_ref[0]
    y = xx_ref[pl.ds(8 - (CONV_W - 1), tc), :] * cw_ref[0:1, :]
    for j in range(1, CONV_W):
        y = y + xx_ref[pl.ds(8 - (CONV_W - 1) + j, tc), :] * cw_ref[j:j + 1, :]
    c = jax.nn.silu(y)
    tail = xx_ref[tc:tc + 8, :]
    conv_ref[0] = tail
    xx_ref[0:8, :] = tail
    small = small_ref[0]
    beta, gcol = _gdn_gates(small[:, B_OFF:B_OFF + GDN_HEADS], small[:, A_OFF:A_OFF + GDN_HEADS], alog_ref[...], dtb_ref[...])
    _, grow = _gdn_gates(smallt_ref[0][0:GDN_HEADS], smallt_ref[0][GDN_HEADS:2 * GDN_HEADS], alogt_ref[...], dtbt_ref[...])
    z = z_ref[0]
    ii = lax.broadcasted_iota(jnp.int32, (C, C), 0)
    jj = lax.broadcasted_iota(jnp.int32, (C, C), 1)
    tril = ii >= jj
    strict = ii > jj
    eye = jnp.where(ii == jj, 1.0, 0.0)
    hcs = [(cc, h) for cc in range(tc // C) for h in range(GDN_HEADS)]
    loc = {}
    for cc, h in hcs:
        rs = slice(cc * C, (cc + 1) * C)
        qh = _l2n(c[rs, h * GDN_DK:(h + 1) * GDN_DK]) * (GDN_DK ** -0.5)
        kh = _l2n(c[rs, GDN_QK + h * GDN_DK:GDN_QK + (h + 1) * GDN_DK])
        vh = c[rs, 2 * GDN_QK + h * GDN_DV:2 * GDN_QK + (h + 1) * GDN_DV]
        bcol = beta[rs, h:h + 1]
        g_c = gcol[rs, h:h + 1]
        g_r = grow[h:h + 1, rs]
        dec_c = jnp.sum(jnp.where(tril, g_r, 0.0), axis=1, keepdims=True)
        dec_r = jnp.sum(jnp.where(ii <= jj, g_c, 0.0), axis=0, keepdims=True)
        lmask = jnp.where(tril, jnp.exp(jnp.where(tril, dec_c - dec_r, 0.0)), 0.0)
        kb = kh * bcol
        edec = jnp.exp(dec_c)
        dlast = dec_c[C - 1:C, :]
        loc[cc, h] = dict(
            m=-jnp.where(strict, _dot_nt(kb, kh) * lmask, 0.0), rhs=jnp.concatenate([vh * bcol, kb * edec], axis=1),
            attn=_dot_nt(qh, kh) * lmask, qe=qh * edec, kdt=(kh * jnp.exp(dlast - dec_c)).T, elast=jnp.exp(dlast))
    tinv = {k: eye + loc[k]["m"] for k in hcs}
    mpow = {k: _dot3(loc[k]["m"], loc[k]["m"]) for k in hcs}
    n_steps = (C - 1).bit_length() - 1
    for step in range(n_steps):
        for k in hcs:
            if step < n_steps - 1:
                r = _dot3(mpow[k], jnp.concatenate([mpow[k], tinv[k]], axis=1))
                mpow[k], tinv[k] = r[:, :C], tinv[k] + r[:, C:]
            else:
                tinv[k] = tinv[k] + _dot3(mpow[k], tinv[k])
    uw = {k: _dot3(tinv[k], loc[k]["rhs"]) for k in hcs}
    for cc in range(tc // C):
        rs = slice(cc * C, (cc + 1) * C)
        for h in range(GDN_HEADS):
            d = loc[cc, h]
            s_old = s_ref[h]
            ws_qs = _dot(jnp.concatenate([uw[cc, h][:, GDN_DV:], d["qe"]], axis=0), s_old)
            v_new = uw[cc, h][:, :GDN_DV] - ws_qs[:C]
            o = ws_qs[C:] + _dot(d["attn"], v_new)
            s_ref[h] = s_old * d["elast"] + _dot(d["kdt"], v_new)
            o = _rms(o) * on_ref[...] * jax.nn.silu(z[rs, h * GDN_DV:(h + 1) * GDN_DV])
            o_ref[0, rs, h * GDN_DV:(h + 1) * GDN_DV] = o

    @pl.when(ci == pl.num_programs(1) - 1)
    def _():
        sfin_ref[0] = s_ref[...]


def _gdn_prompt(qkv, z, small, wts, tc=256):
    B, T, _ = qkv.shape
    smallt = jnp.transpose(small[:, :, B_OFF:B_OFF + 2 * GDN_HEADS], (0, 2, 1))
    tokb = lambda w: pl.BlockSpec((1, tc, w), lambda b, c: (b, c, 0))
    return pl.pallas_call(
        _gdn_prompt_kernel, grid=(B, T // tc),
        in_specs=[tokb(CONV_DIM), tokb(GDN_V), tokb(SMALL_W), pl.BlockSpec((1, 2 * GDN_HEADS, tc), lambda b, c: (b, 0, c)),
                  _const_spec((CONV_W, CONV_DIM)), _const_spec((1, GDN_HEADS)), _const_spec((1, GDN_HEADS)),
                  _const_spec((GDN_HEADS, 1)), _const_spec((GDN_HEADS, 1)), _const_spec((1, GDN_DV))],
        out_specs=[tokb(GDN_V), pl.BlockSpec((1, GDN_HEADS, GDN_DK, GDN_DV), lambda b, c: (b, 0, 0, 0)),
                   pl.BlockSpec((1, 8, CONV_DIM), lambda b, c: (b, 0, 0))],
        out_shape=[jax.ShapeDtypeStruct((B, T, GDN_V), F32), jax.ShapeDtypeStruct((B, GDN_HEADS, GDN_DK, GDN_DV), F32),
                   jax.ShapeDtypeStruct((B, 8, CONV_DIM), F32)],
        scratch_shapes=[pltpu.VMEM((GDN_HEADS, GDN_DK, GDN_DV), F32), pltpu.VMEM((tc + 8, CONV_DIM), F32)],
        compiler_params=_cparams(("parallel", "arbitrary")), name="gdn_prompt",
    )(qkv, z, small, smallt, wts["conv_w"], wts["alog"], wts["dtb"], wts["alog_t"], wts["dtb_t"], wts["onorm"])


def _gdn_sample_kernel(xx_ref, z_ref, small_ref, cw_ref, alog_ref, dtb_ref, on_ref, s_ref,
                       o_ref, sout_ref, qt_ref, kt_ref, wt_ref, u_ref, sc_ref):
    b = pl.program_id(0)
    nb = z_ref.shape[0]

    @pl.when(b == 0)
    def _():
        y = xx_ref[0] * cw_ref[0:1, :]
        for j in range(1, CONV_W):
            y = y + xx_ref[j] * cw_ref[j:j + 1, :]
        c = jax.nn.silu(y)
        small = small_ref[...]
        beta, g = _gdn_gates(small[:, B_OFF:B_OFF + GDN_HEADS], small[:, A_OFF:A_OFF + GDN_HEADS], alog_ref[...], dtb_ref[...])
        a = jnp.exp(g)
        attn = []
        for h in range(GDN_HEADS):
            qh = _l2n(c[:, h * GDN_DK:(h + 1) * GDN_DK]) * (GDN_DK ** -0.5)
            kh = _l2n(c[:, GDN_QK + h * GDN_DK:GDN_QK + (h + 1) * GDN_DK])
            vh = c[:, 2 * GDN_QK + h * GDN_DV:2 * GDN_QK + (h + 1) * GDN_DV]
            bh, ah = beta[:, h:h + 1], a[:, h:h + 1]
            qt_ref[h] = (qh * ah).T
            kt_ref[h] = kh.T
            wt_ref[h] = (kh * bh * ah).T
            u_ref[h] = vh * bh
            attn.append(jnp.sum(qh * kh, axis=1, keepdims=True))
        sc_ref[...] = jnp.concatenate([a] + attn + [jnp.zeros((nb, LANES - 2 * GDN_HEADS), F32)], axis=1)

    lane = lax.broadcasted_iota(jnp.int32, (1, nb), 1)
    pick = lane == b
    sc = sc_ref[pl.ds(b, 1), :]
    zrow = z_ref[pl.ds(b, 1), :]
    for h in range(GDN_HEADS):
        wcol = jnp.sum(jnp.where(pick, wt_ref[h], 0.0), axis=1, keepdims=True)
        qcol = jnp.sum(jnp.where(pick, qt_ref[h], 0.0), axis=1, keepdims=True)
        kcol = jnp.sum(jnp.where(pick, kt_ref[h], 0.0), axis=1, keepdims=True)
        s_old = s_ref[0, h]
        v_new = u_ref[h, pl.ds(b, 1), :] - jnp.sum(s_old * wcol, axis=0, keepdims=True)
        o = jnp.sum(s_old * qcol, axis=0, keepdims=True) + sc[:, GDN_HEADS + h:GDN_HEADS + h + 1] * v_new
        sout_ref[0, h] = s_old * sc[:, h:h + 1] + kcol * v_new
        o = _rms(o) * on_ref[...] * jax.nn.silu(zrow[:, h * GDN_DV:(h + 1) * GDN_DV])
        o_ref[0, :, h * GDN_DV:(h + 1) * GDN_DV] = o


def _gdn_sample(xx4, z2, small2, state, wts):
    nb = z2.shape[0]
    cst = lambda shape: pl.BlockSpec(shape, lambda b: (0,) * len(shape))
    sspec = pl.BlockSpec((1, GDN_HEADS, GDN_DK, GDN_DV), lambda b: (b, 0, 0, 0))
    return pl.pallas_call(
        _gdn_sample_kernel, grid=(nb,),
        in_specs=[cst(xx4.shape), cst(z2.shape), cst(small2.shape), cst((CONV_W, CONV_DIM)), cst((1, GDN_HEADS)),
                  cst((1, GDN_HEADS)), cst((1, GDN_DV)), sspec],
        out_specs=[pl.BlockSpec((1, 1, GDN_V), lambda b: (b, 0, 0)), sspec],
        out_shape=[jax.ShapeDtypeStruct((nb, 1, GDN_V), F32), jax.ShapeDtypeStruct(state.shape, F32)],
        scratch_shapes=[pltpu.VMEM((GDN_HEADS, GDN_DK, nb), F32)] * 3
        + [pltpu.VMEM((GDN_HEADS, nb, GDN_DV), F32), pltpu.VMEM((nb, LANES), F32)],
        compiler_params=_cparams(("arbitrary",)), name="gdn_sample",
    )(xx4, z2, small2, wts["conv_w"], wts["alog"], wts["dtb"], wts["onorm"], state)


def _mem_prompt_kernel(qm_ref, kv_ref, o_ref):
    qm = qm_ref[0]
    kv = kv_ref[0]
    for h in range(MEM_HEADS):
        sl = slice(h * MEM_HD, (h + 1) * MEM_HD)
        s = _dot_nt(qm[:, sl], kv[:, sl]) * MEM_SCALE
        e = jnp.exp(s - jnp.max(s, axis=-1, keepdims=True))
        p = e / jnp.sum(e, axis=-1, keepdims=True)
        o_ref[0, :, sl] = _dot(p, kv[:, MEM_Q + h * MEM_HD:MEM_Q + (h + 1) * MEM_HD])


def _mem_prompt(qm, mkv, tq=256):
    B, T, _ = qm.shape
    M = mkv.shape[1]
    return pl.pallas_call(
        _mem_prompt_kernel, grid=(B, T // tq),
        in_specs=[pl.BlockSpec((1, tq, MEM_Q), lambda b, t: (b, t, 0)), pl.BlockSpec((1, M, 2 * MEM_Q), lambda b, t: (b, 0, 0))],
        out_specs=pl.BlockSpec((1, tq, MEM_Q), lambda b, t: (b, t, 0)),
        out_shape=jax.ShapeDtypeStruct((B, T, MEM_Q), F32),
        compiler_params=_cparams(("parallel", "parallel")), name="mem_attn_prompt",
    )(qm, mkv)


def _mem_sample_kernel(qm_ref, kv_ref, o_ref):
    q = qm_ref[0].astype(F32)
    for h in range(MEM_HEADS):
        sl = slice(h * MEM_HD, (h + 1) * MEM_HD)
        k = kv_ref[0, :, 0, h, :]
        v = kv_ref[0, :, 1, h, :]
        s = jnp.sum(k * q[:, sl], axis=1, keepdims=True) * MEM_SCALE
        e = jnp.exp(s - jnp.max(s, axis=0, keepdims=True))
        p = e / jnp.sum(e, axis=0, keepdims=True)
        o_ref[0, :, sl] = jnp.sum(v * p, axis=0, keepdims=True)


def _mem_sample(qm3, cache):
    B, M = cache.shape[0], cache.shape[1]
    return pl.pallas_call(
        _mem_sample_kernel, grid=(B,),
        in_specs=[pl.BlockSpec((1, 1, MEM_Q), lambda b: (b, 0, 0)),
                  pl.BlockSpec((1, M, 2, MEM_HEADS, MEM_HD), lambda b: (b, 0, 0, 0, 0))],
        out_specs=pl.BlockSpec((1, 1, MEM_Q), lambda b: (b, 0, 0)),
        out_shape=jax.ShapeDtypeStruct((B, 1, MEM_Q), F32),
        compiler_params=_cparams(("parallel",)), name="mem_attn_sample",
    )(qm3, cache)


def _merge_ffn_kernel(on_ref, og_ref, om_ref, gate_ref, x_ref, wbn_ref, wbg_ref, wbm_ref, wout_ref, gffn_ref, w1_ref, w2_ref, y_ref):
    merged = (gate_ref[:, 0:D_MODEL] * _dot(on_ref[...], wbn_ref[...])
              + gate_ref[:, D_MODEL:2 * D_MODEL] * _dot(og_ref[...], wbg_ref[...])
              + gate_ref[:, 2 * D_MODEL:3 * D_MODEL] * _dot(om_ref[...], wbm_ref[...]))
    h = x_ref[...] + _dot(merged, wout_ref[...])
    f = _dot(_rms(h) * gffn_ref[...], w1_ref[...])
    f = jnp.square(jnp.maximum(f, 0.0))
    y_ref[...] = h + _dot(f, w2_ref[...])


def _merge_ffn(o_nsa, o_gdn, o_mem, gates, x2d, wts, tm):
    n = x2d.shape[0]
    tok = lambda w: pl.BlockSpec((tm, w), lambda i: (i, 0))
    return pl.pallas_call(
        _merge_ffn_kernel, grid=(n // tm,),
        in_specs=[tok(Q_NSA), tok(GDN_V), tok(MEM_Q), tok(N_BRANCH * D_MODEL), tok(D_MODEL),
                  _const_spec((Q_NSA, D_MODEL)), _const_spec((GDN_V, D_MODEL)), _const_spec((MEM_Q, D_MODEL)),
                  _const_spec((D_MODEL, D_MODEL)), _const_spec((1, D_MODEL)), _const_spec((D_MODEL, D_FF)), _const_spec((D_FF, D_MODEL))],
        out_specs=tok(D_MODEL),
        out_shape=jax.ShapeDtypeStruct((n, D_MODEL), F32),
        compiler_params=_cparams(("parallel",)), name="merge_ffn",
    )(o_nsa, o_gdn, o_mem, gates, x2d, wts["w_br_nsa"], wts["w_br_gdn"], wts["w_br_mem"], wts["w_out"], wts["g_ffn"],
      wts["w_ff1"], wts["w_ff2"])


def _prep_weights(g_mix, w_in, nsa_q_norm, nsa_k_norm, cmp_pe, cmp_w1, cmp_w2, gdn_conv_w, gdn_A_log, gdn_dt_bias, gdn_o_norm,
                  g_mem, w_mem_kv, mem_q_norm, mem_k_norm, w_br_nsa, w_br_gdn, w_br_mem, w_out, g_ffn, w_ff1, w_ff2):
    offs = [0]
    for s in IN_SPLITS:
        offs.append(offs[-1] + s)
    wt = w_in.T
    seg = lambda i: wt[offs[i]:offs[i + 1]]
    small = jnp.concatenate([seg(2), seg(4), seg(5)], axis=0)
    small = jnp.pad(small, ((0, SMALL_W - small.shape[0]), (0, 0)))
    kv = seg(1)
    k_of = lambda c: kv[c * KV_ROW:c * KV_ROW + KV_ROW // 2]
    w_tok = jnp.concatenate([seg(0), seg(3), seg(6), seg(7), seg(8), small, k_of(1), k_of(2)], axis=0).T.astype(BF16)
    w_ft = jnp.concatenate([kv, seg(0), jnp.pad(seg(2), ((0, G_ROWS - 3 * NSA_HEADS), (0, 0)))], axis=0).astype(BF16)
    R = CMP_BLOCK // CMP_STRIDE
    w1r = cmp_w1.reshape(2, R, CMP_STRIDE, NSA_HD, CMP_HID)
    wbd = jnp.einsum("krsdf,gh->ksgdhrf", w1r, jnp.eye(NSA_KV_HEADS, dtype=F32))
    wbd = wbd.reshape(2, CMP_STRIDE * NSA_KV_HEADS * NSA_HD, NSA_KV_HEADS * R * CMP_HID).astype(BF16)
    row = lambda v: v.reshape(1, -1)
    return dict(
        g_mix=row(g_mix), w_tok=w_tok, w_ft=w_ft, qn=row(jnp.tile(nsa_q_norm, LANES // NSA_HD)), qn_col=nsa_q_norm.reshape(NSA_HD, 1),
        kn=nsa_k_norm.reshape(3, NSA_HD, 1), kn_row=jnp.tile(nsa_k_norm, (1, LANES // NSA_HD)), mqn=row(mem_q_norm), mkn=row(mem_k_norm), g_mem=row(g_mem),
        w_mem_kv=w_mem_kv.astype(BF16), wbd=wbd,
        pe=jnp.broadcast_to(cmp_pe.reshape(2, 1, CMP_BLOCK * NSA_HD), (2, 8, CMP_BLOCK * NSA_HD)),
        w1=cmp_w1.astype(BF16), w2=cmp_w2.astype(BF16), conv_w=gdn_conv_w,
        alog=row(gdn_A_log), dtb=row(gdn_dt_bias), alog_t=gdn_A_log.reshape(-1, 1), dtb_t=gdn_dt_bias.reshape(-1, 1),
        onorm=row(gdn_o_norm), w_br_nsa=w_br_nsa.astype(BF16), w_br_gdn=w_br_gdn.astype(BF16), w_br_mem=w_br_mem.astype(BF16),
        w_out=w_out.astype(BF16), g_ffn=row(g_ffn), w_ff1=w_ff1.astype(BF16), w_ff2=w_ff2.astype(BF16))


def _rows_5d(kt):
    B, _, N = kt.shape
    return jnp.transpose(kt.reshape(B, 2, NSA_KV_HEADS, NSA_HD, N), (0, 4, 1, 2, 3))


def _feature_major(rows):
    B, N = rows.shape[0], rows.shape[1]
    return jnp.transpose(rows, (0, 2, 3, 4, 1)).reshape(B, KV_ROW, N)


def kernel(x_prompt, x_sample, cache_cmp_kv, cache_sel_kv, cache_win_kv, state_gdn, state_gdn_conv, cache_mem_kv, page_table, mem_prompt, g_mix, w_in, nsa_q_norm, nsa_k_norm, cmp_pe, cmp_w1, cmp_w2, gdn_conv_w, gdn_A_log, gdn_dt_bias, gdn_o_norm, g_mem, w_mem_kv, mem_q_norm, mem_k_norm, w_br_nsa, w_br_gdn, w_br_mem, w_out, g_ffn, w_ff1, w_ff2):
    wts = _prep_weights(g_mix, w_in, nsa_q_norm, nsa_k_norm, cmp_pe, cmp_w1, cmp_w2, gdn_conv_w, gdn_A_log, gdn_dt_bias,
                        gdn_o_norm, g_mem, w_mem_kv, mem_q_norm, mem_k_norm, w_br_nsa, w_br_gdn, w_br_mem, w_out, g_ffn, w_ff1, w_ff2)
    B, T, D = x_prompt.shape
    nb = x_sample.shape[0]
    assert x_sample.shape[1] == 1 and T % 256 == 0 and T >= WINDOW + 128
    n_pages = page_table.shape[1]
    past = n_pages * PAGE_SIZE

    p = _inproj(x_prompt, jnp.arange(T, dtype=jnp.int32), wts, tm=256)
    mkv = _memkv(mem_prompt.reshape(-1, D), wts).reshape(B, -1, 2 * MEM_Q)
    ckv = _pcompress(p["kct"], wts)
    o_nsa = _pattn_t(p, ckv)
    o_gdn, p_state, conv_tail = _gdn_prompt(p["qkv"], p["z"], p["small"], wts)
    o_mem = _mem_prompt(p["qm"], mkv)
    y_prompt = _merge_ffn(o_nsa.reshape(-1, Q_NSA), o_gdn.reshape(-1, GDN_V), o_mem.reshape(-1, MEM_Q),
                          p["gates"].reshape(-1, N_BRANCH * D), x_prompt.reshape(-1, D), wts, tm=256).reshape(B, T, D)
    p_cmp, p_sel = _rows_5d(p["kct"]), _rows_5d(p["kst"])
    p_win = _rows_5d(p["kwt"][:, :, T - min(WINDOW, T):])
    p_conv = conv_tail[:, 8 - (CONV_W - 1):, :]
    p_mem_kv = mkv.reshape(B, -1, 2, MEM_HEADS, MEM_HD)

    xs = x_sample.reshape(1, nb, D)
    s = _inproj(xs, jnp.full((nb,), past, jnp.int32), wts, tm=nb)
    sq, sqkv, sz, sqm, sgates, ssmall, skct, skst, skwt = (s[k] for k in ("q", "qkv", "z", "qm", "gates", "small", "kct", "kst", "kwt"))
    pool_cmp = _feature_major(cache_cmp_kv)
    pool_sel = _feature_major(cache_sel_kv)
    q3 = sq.reshape(nb, 1, Q_NSA)
    ocmp, idx = _s1(pool_cmp, page_table, q3, wts, q_pos=past)
    n_sel = -(-(past + 1) // SEL_BLOCK)
    n_top = min(SEL_TOP, n_sel)
    idx = idx[:, :NSA_KV_HEADS, :n_top].reshape(nb, NSA_KV_HEADS * n_top)
    blk_per_page = PAGE_SIZE // SEL_BLOCK
    is_new = idx * SEL_BLOCK >= past
    page = jnp.take_along_axis(page_table, jnp.where(is_new, 0, idx // blk_per_page), axis=1)
    meta = jnp.where(is_new, blk_per_page, idx % blk_per_page).astype(jnp.int32)
    tokrow = lambda kt: jnp.transpose(kt[0], (1, 0)).reshape(nb, 1, KV_ROW)
    cwint = _feature_major(cache_win_kv)
    o_nsa_s, swin = _s2(pool_sel, page.astype(jnp.int32), meta, q3, ocmp, ssmall.reshape(nb, 1, SMALL_W), tokrow(skst), tokrow(skwt),
                        skwt[0], cwint, n_top)
    xx4 = jnp.concatenate([jnp.transpose(state_gdn_conv, (1, 0, 2)), sqkv], axis=0)
    o_gdn_s, s_state = _gdn_sample(xx4, sz[0], ssmall[0], state_gdn, wts)
    o_mem_s = _mem_sample(sqm.reshape(nb, 1, MEM_Q), cache_mem_kv)
    y_sample = _merge_ffn(o_nsa_s.reshape(nb, Q_NSA), o_gdn_s.reshape(nb, GDN_V), o_mem_s.reshape(nb, MEM_Q), sgates[0], x_sample.reshape(nb, D),
                          wts, tm=nb).reshape(nb, 1, D)
    s_cmp = jnp.transpose(skct[0], (1, 0)).reshape(nb, 1, 2, NSA_KV_HEADS, NSA_HD)
    s_sel = jnp.transpose(skst[0], (1, 0)).reshape(nb, 1, 2, NSA_KV_HEADS, NSA_HD)
    s_win = _rows_5d(swin)
    s_conv = jnp.transpose(xx4[1:], (1, 0, 2))
    return (y_prompt, y_sample, p_cmp, p_sel, p_win, p_state, p_conv, p_mem_kv, s_cmp, s_sel, s_win, s_state, s_conv)
```

```python
import functools

import jax
import jax.numpy as jnp
from jax import lax
from jax.experimental import pallas as pl
from jax.experimental.pallas import tpu as pltpu

F32 = jnp.float32
BF16 = jnp.bfloat16
HI = lax.Precision.HIGHEST

D_MODEL = 1024
PAGE_SIZE = 128
NSA_HEADS = 8
NSA_KV_HEADS = 2
NSA_HD = 64
NSA_GROUP = NSA_HEADS // NSA_KV_HEADS
NSA_SCALE = NSA_HD ** -0.5
LOG2_E = 1.4426950408889634
CMP_BLOCK = 32
CMP_STRIDE = 16
CMP_HID = 128
SEL_BLOCK = 64
SEL_TOP = 16
WINDOW = 512
FORCE_SCORE = 1e9
GDN_HEADS = 4
GDN_DK = 128
GDN_DV = 128
CONV_W = 4
GDN_CHUNK = 64
MEM_HEADS = 4
MEM_HD = 128
MEM_SCALE = MEM_HD ** -0.5
D_FF = 4 * D_MODEL
ROPE_THETA = 10000.0
EPS = 1e-6

Q_NSA = NSA_HEADS * NSA_HD
KV_ROW = 2 * NSA_KV_HEADS * NSA_HD
GDN_QK = GDN_HEADS * GDN_DK
GDN_V = GDN_HEADS * GDN_DV
CONV_DIM = 2 * GDN_QK + GDN_V
MEM_Q = MEM_HEADS * MEM_HD
N_BRANCH = 3
IN_SPLITS = (Q_NSA, 3 * KV_ROW, 3 * NSA_HEADS, CONV_DIM, GDN_HEADS, GDN_HEADS, GDN_V, MEM_Q, N_BRANCH * D_MODEL)

LANES = 128
SMALL_W = LANES
G_NSA_OFF, B_OFF, A_OFF = 0, 3 * NSA_HEADS, 3 * NSA_HEADS + GDN_HEADS
TOK_Q, TOK_QKV, TOK_Z, TOK_QM, TOK_GBR, TOK_SMALL, TOK_KTOK = 0, 512, 2048, 2560, 3072, 6144, 6272
TOK_W = TOK_KTOK + 2 * LANES
FT_KV, FT_Q, FT_G = 0, 3 * KV_ROW, 3 * KV_ROW + Q_NSA
G_ROWS = 32
FT_W = FT_G + G_ROWS
V7X_VMEM_LIMIT = 56 * 1024 * 1024


def _cparams(sem):
    return pltpu.CompilerParams(dimension_semantics=sem, vmem_limit_bytes=V7X_VMEM_LIMIT)


def _dot(a, b):
    return jnp.dot(a.astype(BF16), b.astype(BF16), preferred_element_type=F32)


def _dot_nt(a, b):
    return lax.dot_general(a.astype(BF16), b.astype(BF16), (((1,), (1,)), ((), ())), preferred_element_type=F32)


def _dot_hi(a, b):
    return jnp.dot(a, b, precision=HI, preferred_element_type=F32)


def _split_bf16(a):
    hi = a.astype(BF16)
    return hi, (a - hi.astype(F32)).astype(BF16)


def _dot3(a, b):
    ah, al = _split_bf16(a)
    bh, bl = _split_bf16(b)
    d = lambda x, y: jnp.dot(x, y, preferred_element_type=F32)
    return d(ah, bh) + (d(ah, bl) + d(al, bh))


def _rms(x, axis=-1):
    return x * lax.rsqrt(jnp.mean(x * x, axis=axis, keepdims=True) + EPS)


def _const_spec(shape):
    nd = len(shape)
    return pl.BlockSpec(shape, lambda *_: (0,) * nd, pipeline_mode=pl.Buffered(1))


def _inproj_kernel(x_ref, gmix_ref, wtok_ref, wft_ref, qn_ref, qnc_ref, kn_ref, knr_ref, cq_ref, sq_ref, ck_ref, sk_ref, mqn_ref, blk_ref,
                   qkv_ref, z_ref, qm_ref, gate_ref, small_ref, kc_ref, ks_ref, kw_ref, *extra_refs, decode):
    x = x_ref[0]
    ub = (_rms(x) * gmix_ref[...]).astype(BF16)
    tm = x.shape[0]
    lane = lax.broadcasted_iota(jnp.int32, (tm, LANES), 1)
    lo = lane < NSA_HD
    first_half = (lane % NSA_HD) < (NSA_HD // 2)

    def norm_rope_slab(col0, gain):
        qs = jnp.dot(ub, wtok_ref[:, col0:col0 + LANES], preferred_element_type=F32)
        sq = qs * qs
        ss_lo = jnp.sum(jnp.where(lo, sq, 0.0), axis=-1, keepdims=True)
        ss_hi = jnp.sum(jnp.where(lo, 0.0, sq), axis=-1, keepdims=True)
        r = jnp.where(lo, lax.rsqrt(ss_lo / NSA_HD + EPS), lax.rsqrt(ss_hi / NSA_HD + EPS))
        qs = qs * r * gain
        rot = jnp.where(first_half, -pltpu.roll(qs, LANES - NSA_HD // 2, axis=1), pltpu.roll(qs, NSA_HD // 2, axis=1))
        return qs * cq_ref[...] + rot * sq_ref[...]

    if decode:
        (q_ref,) = extra_refs
        for i in range(Q_NSA // LANES):
            q_ref[0, :, i * LANES:(i + 1) * LANES] = (norm_rope_slab(TOK_Q + i * LANES, qn_ref[...]) * NSA_SCALE).astype(BF16)
    else:
        kstok0_ref, kstok1_ref, kwtok_ref, qt_ref, gt_ref = extra_refs
        ks = norm_rope_slab(TOK_KTOK, knr_ref[1:2, :])
        kstok0_ref[0] = jnp.where(lo, ks, blk_ref[...]).astype(BF16)
        kstok1_ref[0] = jnp.where(lo, pltpu.roll(ks, NSA_HD, axis=1), blk_ref[...]).astype(BF16)
        kwtok_ref[0] = norm_rope_slab(TOK_KTOK + LANES, knr_ref[2:3, :]).astype(BF16)
    qkv_ref[0] = jnp.dot(ub, wtok_ref[:, TOK_QKV:TOK_Z], preferred_element_type=F32)
    z_ref[0] = jnp.dot(ub, wtok_ref[:, TOK_Z:TOK_QM], preferred_element_type=F32)
    for h in range(MEM_HEADS):
        qm = jnp.dot(ub, wtok_ref[:, TOK_QM + h * MEM_HD:TOK_QM + (h + 1) * MEM_HD], preferred_element_type=F32)
        qm_ref[0, :, h * MEM_HD:(h + 1) * MEM_HD] = (_rms(qm) * mqn_ref[...]).astype(BF16)
    for i in range(N_BRANCH):
        gb = jnp.dot(ub, wtok_ref[:, TOK_GBR + i * D_MODEL:TOK_GBR + (i + 1) * D_MODEL], preferred_element_type=F32)
        gate_ref[0, :, i * D_MODEL:(i + 1) * D_MODEL] = jax.nn.sigmoid(gb)
    small_ref[0] = jnp.dot(ub, wtok_ref[:, TOK_SMALL:TOK_SMALL + SMALL_W], preferred_element_type=F32)
    ft_rows = FT_Q if decode else FT_W
    ft = lax.dot_general(wft_ref[0:ft_rows, :], ub, (((1,), (1,)), ((), ())), preferred_element_type=F32)
    cos = ck_ref[...]
    sin = sk_ref[...]
    half = NSA_HD // 2

    def norm_rope_rows(row0, gain_col):
        kh = _rms(ft[row0:row0 + NSA_HD, :], axis=0) * gain_col
        x1, x2 = kh[:half], kh[half:]
        return x1 * cos - x2 * sin, x2 * cos + x1 * sin

    for c, out_ref in enumerate((kc_ref, ks_ref, kw_ref)):
        base = FT_KV + c * KV_ROW
        for g in range(NSA_KV_HEADS):
            r1, r2 = norm_rope_rows(base + g * NSA_HD, kn_ref[c])
            out_ref[0, g * NSA_HD:g * NSA_HD + half, :] = r1
            out_ref[0, g * NSA_HD + half:(g + 1) * NSA_HD, :] = r2
        out_ref[0, KV_ROW // 2:, :] = ft[base + KV_ROW // 2:base + KV_ROW, :]
    if not decode:
        for h in range(NSA_HEADS):
            r1, r2 = norm_rope_rows(FT_Q + h * NSA_HD, qnc_ref[...])
            qt_ref[0, h * NSA_HD:(h + 1) * NSA_HD, :] = (jnp.concatenate([r1, r2], axis=0) * (NSA_SCALE * LOG2_E)).astype(BF16)
        gt_ref[0] = jax.nn.sigmoid(ft[FT_G:FT_G + G_ROWS, :])


def _inproj(x, pos, wts, tm, decode):
    B, T, _ = x.shape
    half = NSA_HD // 2
    inv = ROPE_THETA ** (-jnp.arange(half, dtype=F32) / half)
    ang = pos.astype(F32)[:, None] * inv[None, :]
    cos, sin = jnp.cos(ang), jnp.sin(ang)
    cq, sq = jnp.tile(cos, (1, LANES // half)), jnp.tile(sin, (1, LANES // half))
    ck, sk = cos.T, sin.T
    tok = lambda w: pl.BlockSpec((1, tm, w), lambda b, t: (b, t, 0))
    ftm = lambda r: pl.BlockSpec((1, r, tm), lambda b, t: (b, 0, t))
    tok_shape = lambda w, dt: jax.ShapeDtypeStruct((B, T, w), dt)
    ft_shape = lambda r, dt: jax.ShapeDtypeStruct((B, r, T), dt)
    names = ["qkv", "z", "qm", "gates", "small", "kct", "kst", "kwt"]
    out_specs = [tok(CONV_DIM), tok(GDN_V), tok(MEM_Q), tok(N_BRANCH * D_MODEL), tok(SMALL_W), ftm(KV_ROW), ftm(KV_ROW), ftm(KV_ROW)]
    out_shape = [tok_shape(CONV_DIM, F32), tok_shape(GDN_V, F32), tok_shape(MEM_Q, BF16), tok_shape(N_BRANCH * D_MODEL, F32),
                 tok_shape(SMALL_W, F32), ft_shape(KV_ROW, F32), ft_shape(KV_ROW, F32), ft_shape(KV_ROW, F32)]
    if decode:
        names += ["q"]
        out_specs += [tok(Q_NSA)]
        out_shape += [tok_shape(Q_NSA, BF16)]
    else:
        names += ["ks_tok0", "ks_tok1", "kw_tok", "qt", "gt"]
        out_specs += [tok(LANES), tok(LANES), tok(LANES), ftm(Q_NSA), ftm(G_ROWS)]
        out_shape += [tok_shape(LANES, BF16)] * 3 + [ft_shape(Q_NSA, BF16), ft_shape(G_ROWS, F32)]
    blk = (pos[:, None] // SEL_BLOCK + NSA_HD == jnp.arange(LANES)[None, :]).astype(F32)
    per_t = pl.BlockSpec((tm, LANES), lambda b, t: (t, 0))
    outs = pl.pallas_call(
        functools.partial(_inproj_kernel, decode=decode),
        grid=(B, T // tm),
        in_specs=[tok(D_MODEL), _const_spec((1, D_MODEL)), _const_spec((D_MODEL, TOK_W)), _const_spec((FT_W, D_MODEL)),
                  _const_spec((1, LANES)), _const_spec((NSA_HD, 1)), _const_spec((3, NSA_HD, 1)), _const_spec((3, LANES)),
                  per_t, per_t,
                  pl.BlockSpec((half, tm), lambda b, t: (0, t)), pl.BlockSpec((half, tm), lambda b, t: (0, t)),
                  _const_spec((1, MEM_HD)), per_t],
        out_specs=out_specs, out_shape=out_shape,
        compiler_params=_cparams(("parallel", "parallel")),
        name="inproj",
    )(x, wts["g_mix"], wts["w_tok"], wts["w_ft"], wts["qn"], wts["qn_col"], wts["kn"], wts["kn_row"], cq, sq, ck, sk, wts["mqn"], blk)
    return dict(zip(names, outs))


def _memkv_kernel(m_ref, g_ref, w_ref, kn_ref, o_ref):
    u = _rms(m_ref[...]) * g_ref[...]
    kv = _dot(u, w_ref[...])
    for h in range(MEM_HEADS):
        sl = slice(h * MEM_HD, (h + 1) * MEM_HD)
        o_ref[:, sl] = _rms(kv[:, sl]) * kn_ref[...]
    o_ref[:, MEM_Q:] = kv[:, MEM_Q:]


def _memkv(mem2d, wts, tm=512):
    n = mem2d.shape[0]
    tm = min(tm, n)
    return pl.pallas_call(
        _memkv_kernel, grid=(n // tm,),
        in_specs=[pl.BlockSpec((tm, D_MODEL), lambda i: (i, 0)), _const_spec((1, D_MODEL)),
                  _const_spec((D_MODEL, 2 * MEM_Q)), _const_spec((1, MEM_HD))],
        out_specs=pl.BlockSpec((tm, 2 * MEM_Q), lambda i: (i, 0)),
        out_shape=jax.ShapeDtypeStruct((n, 2 * MEM_Q), F32),
        compiler_params=_cparams(("parallel",)), name="memkv",
    )(mem2d, wts["g_mem"], wts["w_mem_kv"], wts["mkn"])


CHUNK_PITCH = CMP_STRIDE + 8


def _transpose_pages(get_page, pages, xs_ref):
    cpp = PAGE_SIZE // CMP_STRIDE
    for p in pages:
        xt = get_page(p).T
        for c in range(cpp):
            r0 = (p * cpp + c) * CHUNK_PITCH
            xs_ref[0, r0:r0 + CMP_STRIDE, :] = xt[c * CMP_STRIDE:(c + 1) * CMP_STRIDE, :LANES]
            xs_ref[1, r0:r0 + CMP_STRIDE, :] = xt[c * CMP_STRIDE:(c + 1) * CMP_STRIDE, LANES:]


def _compress(n_chunks, xs_ref, hs_ref, wbd_ref, pe_ref, w1_ref, w2_ref, between=None):
    hs_ref[n_chunks:, :] = jnp.zeros((8, CMP_HID), F32)
    parts = []
    for kv in range(2):
        lhs = jnp.concatenate([xs_ref[kv, pl.ds(s, n_chunks, stride=CHUNK_PITCH), :].astype(BF16) for s in range(CMP_STRIDE)], axis=1)
        if between is not None:
            between(kv)
        h = jnp.dot(lhs, wbd_ref[kv], preferred_element_type=F32)
        pe_h = jnp.dot(pe_ref[kv].astype(BF16), w1_ref[kv], preferred_element_type=F32)[0:1]
        for g in range(NSA_KV_HEADS):
            h0 = h[:, g * 2 * CMP_HID:g * 2 * CMP_HID + CMP_HID]
            hs_ref[0:n_chunks, :] = h[:, g * 2 * CMP_HID + CMP_HID:(g + 1) * 2 * CMP_HID]
            hh = h0 + hs_ref[pl.ds(1, n_chunks), :] + pe_h
            parts.append(jnp.dot(jax.nn.gelu(hh).astype(BF16), w2_ref[kv], preferred_element_type=F32))
    return jnp.concatenate(parts, axis=1)


def _masked_softmax(s, allow):
    s = jnp.where(allow, s, -1e30)
    e = jnp.exp(s - jnp.max(s, axis=-1, keepdims=True))
    p = e / jnp.sum(e, axis=-1, keepdims=True)
    return jnp.where(allow, p, 0.0)


def _cmp_probs(qg, kc, tpos, n_cmp):
    n_chunks = kc.shape[0]
    s = _dot_nt(qg, kc)
    i = lax.broadcasted_iota(jnp.int32, (1, n_chunks), 1)
    allow = jnp.where(i < n_cmp, i * CMP_STRIDE + CMP_BLOCK - 1, jnp.int32(2 ** 30)) <= tpos
    return _masked_softmax(s, allow)


def _select_blocks(psum, tpos, n_cmp, n_sel, ns_pad):
    n_chunks = psum.shape[1]
    ci = lax.broadcasted_iota(jnp.int32, (n_chunks, ns_pad), 0)
    sj = lax.broadcasted_iota(jnp.int32, (n_chunks, ns_pad), 1)
    hit = (ci * CMP_STRIDE < (sj + 1) * SEL_BLOCK) & (ci * CMP_STRIDE + CMP_BLOCK > sj * SEL_BLOCK) & (ci < n_cmp) & (sj < n_sel)
    imp = _dot(psum, jnp.where(hit, 1.0, 0.0))
    jj = lax.broadcasted_iota(jnp.int32, (1, ns_pad), 1)
    imp = jnp.where((jj * SEL_BLOCK <= tpos) & (jj < n_sel), imp, -jnp.inf)
    imp = jnp.where((jj == 0) | (jj == tpos // SEL_BLOCK), FORCE_SCORE, imp)
    cnt = jnp.zeros(imp.shape, F32)
    for j in range(n_sel):
        col = imp[:, j:j + 1]
        later = jnp.where(jj > j, 1.0, 0.0)
        cnt = cnt + jnp.where(col > imp, 1.0, jnp.where(col == imp, later, 0.0))
    return jnp.where(cnt < min(SEL_TOP, n_sel), 1.0, 0.0)


def _head_rows(q, g):
    return jnp.concatenate([q[:, (g * NSA_GROUP + j) * NSA_HD:(g * NSA_GROUP + j + 1) * NSA_HD] for j in range(NSA_GROUP)], axis=0)


def _gate_merge(gsig, o_cmp, o_sel, o_win, tq):
    cols = []
    for h in range(NSA_HEADS):
        g, j = divmod(h, NSA_GROUP)
        rows = slice(j * tq, (j + 1) * tq)
        cols.append(gsig[:, 3 * h:3 * h + 1] * o_cmp[g][rows] + gsig[:, 3 * h + 1:3 * h + 2] * o_sel[g][rows]
                    + gsig[:, 3 * h + 2:3 * h + 3] * o_win[g][rows])
    return jnp.concatenate(cols, axis=1)


def _pcompress_kernel(kc_ref, wbd_ref, pe_ref, w1_ref, w2_ref, o_ref, xs_ref, hs_ref):
    n_pages = kc_ref.shape[2] // PAGE_SIZE
    _transpose_pages(lambda p: kc_ref[0, :, p * PAGE_SIZE:(p + 1) * PAGE_SIZE], range(n_pages), xs_ref)
    o_ref[0] = _compress(n_pages * PAGE_SIZE // CMP_STRIDE, xs_ref, hs_ref, wbd_ref, pe_ref, w1_ref, w2_ref)


def _pcompress(kct, wts):
    B, _, T = kct.shape
    n_chunks = T // CMP_STRIDE
    return pl.pallas_call(
        _pcompress_kernel, grid=(B,),
        in_specs=[pl.BlockSpec((1, KV_ROW, T), lambda b: (b, 0, 0)), _const_spec(wts["wbd"].shape), _const_spec(wts["pe"].shape),
                  _const_spec(wts["w1"].shape), _const_spec(wts["w2"].shape)],
        out_specs=pl.BlockSpec((1, n_chunks, KV_ROW), lambda b: (b, 0, 0)),
        out_shape=jax.ShapeDtypeStruct((B, n_chunks, KV_ROW), F32),
        scratch_shapes=[pltpu.VMEM((2, n_chunks * CHUNK_PITCH, LANES), F32), pltpu.VMEM((n_chunks + 8, CMP_HID), F32)],
        compiler_params=_cparams(("parallel",)), name="prompt_compress",
    )(kct, wts["wbd"], wts["pe"], wts["w1"], wts["w2"])


def _softmax_pv(q, kt, vt, bias):
    s = _dot(q, kt) + bias
    e = jnp.exp(s - jnp.max(s, axis=-1, keepdims=True))
    return _dot_nt(e, vt) / jnp.sum(e, axis=-1, keepdims=True)


def _pattn_kernel(q_ref, ckv_ref, ks_ref, kw_ref, small_ref, o_ref, osel_ref, *, T, tq, n_ext):
    n_chunks = T // CMP_STRIDE
    n_cmp = n_chunks - CMP_BLOCK // CMP_STRIDE + 1
    n_sel = -(-T // SEL_BLOCK)
    ns_pad = -(-n_sel // LANES) * LANES
    span = WINDOW + tq
    qi = pl.program_id(1)
    q0 = qi * tq
    q = q_ref[0]
    ckv = ckv_ref[0]
    tcol = q0 + lax.broadcasted_iota(jnp.int32, (tq, 1), 0)
    wstart = pl.multiple_of(jnp.maximum(q0 - WINDOW, 0), LANES)
    keyw = wstart + lax.broadcasted_iota(jnp.int32, (1, span), 1)
    bias_w = jnp.where(keyw <= tcol, jnp.where(keyw > tcol - WINDOW, 0.0, -1e30), -1e30)
    gsig = jax.nn.sigmoid(small_ref[0][:, G_NSA_OFF:G_NSA_OFF + 3 * NSA_HEADS])
    tiles_per_ext = T // tq // n_ext
    o_cmp, o_win = [], []
    for g in range(NSA_KV_HEADS):
        ksl = slice(g * NSA_HD, (g + 1) * NSA_HD)
        vsl = slice(KV_ROW // 2 + g * NSA_HD, KV_ROW // 2 + (g + 1) * NSA_HD)
        qs = [q[:, (g * NSA_GROUP + j) * NSA_HD:(g * NSA_GROUP + j + 1) * NSA_HD] for j in range(NSA_GROUP)]
        psum = None
        for j in range(NSA_GROUP):
            p = _cmp_probs(qs[j], ckv[:, ksl], tcol, n_cmp)
            o_cmp.append(_dot(p, ckv[:, vsl]))
            psum = p if psum is None else psum + p
        sel = _select_blocks(psum, tcol, n_cmp, n_sel, ns_pad).astype(BF16)
        kwg = kw_ref[0, ksl, pl.ds(wstart, span)]
        vwg = kw_ref[0, vsl, pl.ds(wstart, span)]
        for j in range(NSA_GROUP):
            o_win.append(_softmax_pv(qs[j], kwg, vwg, bias_w))
        for v in range(n_ext):
            kext = (v + 1) * (T // n_ext)

            @pl.when((qi >= v * tiles_per_ext) & (qi < (v + 1) * tiles_per_ext))
            def _():
                es = lax.broadcasted_iota(jnp.int32, (ns_pad, kext), 0)
                ek = lax.broadcasted_iota(jnp.int32, (ns_pad, kext), 1)
                expand = jnp.where(ek // SEL_BLOCK == es, 1.0, 0.0).astype(BF16)
                keysel = jnp.dot(sel, expand, preferred_element_type=F32)
                keyi = lax.broadcasted_iota(jnp.int32, (1, kext), 1)
                bias = jnp.where(keyi <= tcol, jnp.where(keysel > 0.5, 0.0, -1e30), -1e30)
                for j in range(NSA_GROUP):
                    h = g * NSA_GROUP + j
                    osel_ref[:, h * NSA_HD:(h + 1) * NSA_HD] = _softmax_pv(qs[j], ks_ref[0, ksl, 0:kext], ks_ref[0, vsl, 0:kext], bias)
    cols = []
    for h in range(NSA_HEADS):
        cols.append(gsig[:, 3 * h:3 * h + 1] * o_cmp[h] + gsig[:, 3 * h + 1:3 * h + 2] * osel_ref[:, h * NSA_HD:(h + 1) * NSA_HD]
                    + gsig[:, 3 * h + 2:3 * h + 3] * o_win[h])
    o_ref[0] = jnp.concatenate(cols, axis=1)


def _pattn(q, ckv, kst, kwt, small, tq=128, n_ext=4):
    B, T, _ = q.shape
    n_chunks = T // CMP_STRIDE
    assert T % (n_ext * tq) == 0
    full = lambda b, t: (b, 0, 0)
    return pl.pallas_call(
        functools.partial(_pattn_kernel, T=T, tq=tq, n_ext=n_ext), grid=(B, T // tq),
        in_specs=[pl.BlockSpec((1, tq, Q_NSA), lambda b, t: (b, t, 0)), pl.BlockSpec((1, n_chunks, KV_ROW), full),
                  pl.BlockSpec((1, KV_ROW, T), full), pl.BlockSpec((1, KV_ROW, T), full),
                  pl.BlockSpec((1, tq, SMALL_W), lambda b, t: (b, t, 0))],
        out_specs=pl.BlockSpec((1, tq, Q_NSA), lambda b, t: (b, t, 0)),
        out_shape=jax.ShapeDtypeStruct((B, T, Q_NSA), F32),
        scratch_shapes=[pltpu.VMEM((tq, Q_NSA), F32)],
        compiler_params=_cparams(("parallel", "parallel")), name="prompt_nsa_attn",
    )(q, ckv, kst, kwt, small)


KEY_TILE = 256


def _select_blocks_t(psum_t, tl, n_cmp, n_sel, ns_rows):
    n_chunks = psum_t.shape[0]
    sj = lax.broadcasted_iota(jnp.int32, (ns_rows, n_chunks), 0)
    ci = lax.broadcasted_iota(jnp.int32, (ns_rows, n_chunks), 1)
    hit = (ci * CMP_STRIDE < (sj + 1) * SEL_BLOCK) & (ci * CMP_STRIDE + CMP_BLOCK > sj * SEL_BLOCK) & (ci < n_cmp) & (sj < n_sel)
    imp = _dot(jnp.where(hit, 1.0, 0.0), psum_t)
    jj = lax.broadcasted_iota(jnp.int32, (ns_rows, 1), 0)
    imp = jnp.where((jj * SEL_BLOCK <= tl) & (jj < n_sel), imp, -jnp.inf)
    imp = jnp.where((jj == 0) | (jj == tl // SEL_BLOCK), FORCE_SCORE, imp)
    cnt = jnp.zeros(imp.shape, F32)
    for j in range(n_sel):
        row = imp[j:j + 1, :]
        later = jnp.where(jj > j, 1.0, 0.0)
        cnt = cnt + jnp.where(row > imp, 1.0, jnp.where(row == imp, later, 0.0))
    return jnp.where(cnt < min(SEL_TOP, n_sel), 1.0, 0.0)


V_AUG = NSA_HD + 8


def _col_attend(n_tiles, tile_w, k_tile, q_list, v_aug, bias_tile, s_ref, e_ref):
    outs = []
    n_keys = n_tiles * tile_w
    n_heads = len(q_list)
    m_prev = None
    for st in range(n_heads + 1):
        m8 = None
        for i in range(n_tiles):
            rows = slice(i * tile_w, (i + 1) * tile_w)
            if st < n_heads:
                s = jnp.dot(k_tile(i), q_list[st], preferred_element_type=F32)
                bt = bias_tile(i)
                if bt is not None:
                    s = s + bt
                s_ref[st, rows, :] = s
                for r in range(tile_w // 8):
                    m8 = s[8 * r:8 * r + 8] if m8 is None else jnp.maximum(m8, s[8 * r:8 * r + 8])
            if st >= 1:
                e_ref[st - 1, rows, :] = jnp.exp2(s_ref[st - 1, rows, :] - m_prev).astype(BF16)
        if st >= 1:
            a = jnp.dot(v_aug, e_ref[st - 1, 0:n_keys, :], preferred_element_type=F32)
            outs.append(a[:NSA_HD] / a[NSA_HD:NSA_HD + 1])
        if st < n_heads:
            m_prev = jnp.max(m8, axis=0, keepdims=True)
    return outs


def _ones_rows(v_t):
    n = v_t.shape[1]
    r = lax.broadcasted_iota(jnp.int32, (V_AUG - NSA_HD, n), 0)
    return jnp.concatenate([v_t, jnp.where(r == 0, 1.0, 0.0)], axis=0).astype(BF16)


def _pattn_t_kernel(qt_ref, gt_ref, ckv_ref, kstok0_ref, kstok1_ref, vst_ref, kwtok_ref, vwt_ref, o_ref, osel_ref, s_ref, e_ref,
                    *, T, tq, n_ext):
    kstok_refs = (kstok0_ref, kstok1_ref)
    n_chunks = T // CMP_STRIDE
    n_cmp = n_chunks - CMP_BLOCK // CMP_STRIDE + 1
    n_sel = -(-T // SEL_BLOCK)
    ns_rows = -(-n_sel // 8) * 8
    span = WINDOW + tq
    qi = pl.program_id(1)
    q0 = qi * tq
    qt = qt_ref[0]
    gt = gt_ref[0]
    ckv = ckv_ref[0]
    tl = q0 + lax.broadcasted_iota(jnp.int32, (1, tq), 1)
    wstart = pl.multiple_of(jnp.maximum(q0 - WINDOW, 0), LANES)
    ci = lax.broadcasted_iota(jnp.int32, (n_chunks, 1), 0)
    bias_c = jnp.where(jnp.where(ci < n_cmp, ci * CMP_STRIDE + CMP_BLOCK - 1, jnp.int32(2 ** 30)) <= tl, 0.0, -1e30)
    zeros_half = jnp.zeros((NSA_HD, tq), BF16)
    tiles_per_ext = T // tq // n_ext
    kw = wstart + lax.broadcasted_iota(jnp.int32, (span, 1), 0)
    bias_w = jnp.where(kw <= tl, jnp.where(kw > tl - WINDOW, 0.0, -1e30), -1e30)
    o_cmp, o_win, sel_bias, q_hs = [], [], [], []
    for g in range(NSA_KV_HEADS):
        ksl = slice(g * NSA_HD, (g + 1) * NSA_HD)
        vsl = slice(KV_ROW // 2 + g * NSA_HD, KV_ROW // 2 + (g + 1) * NSA_HD)
        q_h = [qt[(g * NSA_GROUP + j) * NSA_HD:(g * NSA_GROUP + j + 1) * NSA_HD, :] for j in range(NSA_GROUP)]
        q_pad = [jnp.concatenate([qh, zeros_half] if g == 0 else [zeros_half, qh], axis=0) for qh in q_h]
        q_hs.append(q_h)
        kc = ckv[:, ksl].astype(BF16)
        vc_t = ckv[:, vsl].T.astype(BF16)
        psum = None
        for j in range(NSA_GROUP):
            s = jnp.dot(kc, q_h[j], preferred_element_type=F32) + bias_c
            e = jnp.exp2(s - jnp.max(s, axis=0, keepdims=True))
            p = jnp.where(bias_c == 0.0, e / jnp.sum(e, axis=0, keepdims=True), 0.0)
            o_cmp.append(jnp.dot(vc_t, p.astype(BF16), preferred_element_type=F32))
            psum = p if psum is None else psum + p
        sel = _select_blocks_t(psum, tl, n_cmp, n_sel, ns_rows)
        sel_bias.append(jnp.concatenate([(sel - 1.0) * 1e30, jnp.zeros((LANES - NSA_HD - ns_rows, tq), F32)], axis=0).astype(BF16))
        vw_aug = _ones_rows(vwt_ref[0, ksl, pl.ds(wstart, span)])
        o_win += _col_attend(
            span // LANES, LANES, lambda i: kwtok_ref[0, pl.ds(wstart + i * LANES, LANES), :], q_pad, vw_aug,
            lambda i: bias_w[i * LANES:(i + 1) * LANES, :], s_ref, e_ref)

    ext_w = T // n_ext
    for v in range(n_ext):
        kext = (v + 1) * ext_w
        n_tiles = kext // KEY_TILE
        first_diag = n_tiles - ext_w // KEY_TILE

        @pl.when((qi >= v * tiles_per_ext) & (qi < (v + 1) * tiles_per_ext))
        def _():
            keyd = (kext - ext_w) + lax.broadcasted_iota(jnp.int32, (ext_w, 1), 0)
            causal = jnp.where(keyd <= tl, 0.0, -1e30)
            for g in range(NSA_KV_HEADS):
                ksl = slice(g * NSA_HD, (g + 1) * NSA_HD)
                vs_aug = _ones_rows(vst_ref[0, ksl, 0:kext])
                q_aug = [jnp.concatenate([qh, sel_bias[g]], axis=0) for qh in q_hs[g]]
                outs = _col_attend(
                    n_tiles, KEY_TILE, lambda i: kstok_refs[g][0, i * KEY_TILE:(i + 1) * KEY_TILE, :], q_aug, vs_aug,
                    lambda i: causal[(i - first_diag) * KEY_TILE:(i - first_diag + 1) * KEY_TILE, :] if i >= first_diag else None,
                    s_ref, e_ref)
                for j in range(NSA_GROUP):
                    h = g * NSA_GROUP + j
                    osel_ref[h * NSA_HD:(h + 1) * NSA_HD, :] = outs[j]

    heads = [gt[3 * h:3 * h + 1, :] * o_cmp[h] + gt[3 * h + 1:3 * h + 2, :] * osel_ref[h * NSA_HD:(h + 1) * NSA_HD, :]
             + gt[3 * h + 2:3 * h + 3, :] * o_win[h] for h in range(NSA_HEADS)]
    o_ref[0] = jnp.concatenate(heads, axis=0).T


def _pattn_t(p, ckv, tq=256, n_ext=4):
    B, _, T = p["qt"].shape
    n_chunks = T // CMP_STRIDE
    assert T % (n_ext * tq) == 0 and (T // n_ext) % KEY_TILE == 0 and -(-T // SEL_BLOCK) <= LANES - NSA_HD
    full = lambda b, t: (b, 0, 0)
    vhalf = pl.BlockSpec((1, KV_ROW // 2, T), lambda b, t: (b, 1, 0))
    return pl.pallas_call(
        functools.partial(_pattn_t_kernel, T=T, tq=tq, n_ext=n_ext), grid=(B, T // tq),
        in_specs=[pl.BlockSpec((1, Q_NSA, tq), lambda b, t: (b, 0, t)), pl.BlockSpec((1, G_ROWS, tq), lambda b, t: (b, 0, t)),
                  pl.BlockSpec((1, n_chunks, KV_ROW), full), pl.BlockSpec((1, T, LANES), full), pl.BlockSpec((1, T, LANES), full), vhalf,
                  pl.BlockSpec((1, T, LANES), full), vhalf],
        out_specs=pl.BlockSpec((1, tq, Q_NSA), lambda b, t: (b, t, 0)),
        out_shape=jax.ShapeDtypeStruct((B, T, Q_NSA), F32),
        scratch_shapes=[pltpu.VMEM((Q_NSA, tq), F32), pltpu.VMEM((NSA_GROUP, T, tq), F32), pltpu.VMEM((NSA_GROUP, T, tq), BF16)],
        compiler_params=_cparams(("parallel", "parallel")), name="prompt_nsa_attn",
    )(p["qt"], p["gt"], ckv, p["ks_tok0"], p["ks_tok1"], p["kst"], p["kw_tok"], p["kwt"])


def _s1_kernel(pt_ref, pool_ref, q_ref, wbd_ref, pe_ref, w1_ref, w2_ref, ocmp_ref, idx_ref, pg_ref, sem_ref, xs_ref, hs_ref,
               *, n_pages, q_pos):
    n_chunks = n_pages * PAGE_SIZE // CMP_STRIDE
    n_cmp = n_chunks - CMP_BLOCK // CMP_STRIDE + 1
    n_sel = -(-(q_pos + 1) // SEL_BLOCK)
    ns_pad = -(-n_sel // LANES) * LANES
    b = pl.program_id(0)
    slot = b % 2

    def page_copy(seq, sl, j):
        return pltpu.make_async_copy(pool_ref.at[pt_ref[seq, j]], pg_ref.at[sl, j], sem_ref.at[sl])

    @pl.when(b == 0)
    def _():
        for j in range(n_pages):
            page_copy(0, 0, j).start()

    @pl.when(b + 1 < pl.num_programs(0))
    def _():
        for j in range(n_pages):
            page_copy(b + 1, 1 - slot, j).start()

    for j in range(n_pages):
        page_copy(b, slot, j).wait()
    _transpose_pages(lambda p: pg_ref[slot, p], range(n_pages), xs_ref)
    ckv = _compress(n_chunks, xs_ref, hs_ref, wbd_ref, pe_ref, w1_ref, w2_ref)
    qrow = q_ref[0].astype(F32)
    q8 = jnp.concatenate([qrow[:, h * NSA_HD:(h + 1) * NSA_HD] for h in range(NSA_HEADS)], axis=0)
    row = lax.broadcasted_iota(jnp.int32, (NSA_HEADS, 1), 0)
    tpos = jnp.full((NSA_HEADS, 1), q_pos, jnp.int32)
    o_all = jnp.zeros((NSA_HEADS, NSA_HD), F32)
    psum = jnp.zeros((NSA_HEADS, n_chunks), F32)
    for g in range(NSA_KV_HEADS):
        kc = ckv[:, g * NSA_HD:(g + 1) * NSA_HD]
        vc = ckv[:, KV_ROW // 2 + g * NSA_HD:KV_ROW // 2 + (g + 1) * NSA_HD]
        p = _cmp_probs(q8, kc, tpos, n_cmp)
        mine = (row // NSA_GROUP) == g
        o_all = jnp.where(mine, _dot(p, vc), o_all)
        pg = jnp.sum(jnp.where(mine, p, 0.0), axis=0, keepdims=True)
        psum = jnp.where(row == g, pg, psum)
    sel = _select_blocks(psum, tpos, n_cmp, n_sel, ns_pad)
    a = lax.broadcasted_iota(jnp.int32, (ns_pad, ns_pad), 0)
    b = lax.broadcasted_iota(jnp.int32, (ns_pad, ns_pad), 1)
    before = jnp.dot(sel.astype(BF16), jnp.where(a < b, 1.0, 0.0).astype(BF16), preferred_element_type=F32)
    jj = lax.broadcasted_iota(jnp.int32, (1, ns_pad), 1).astype(F32)
    lane = lax.broadcasted_iota(jnp.int32, (1, LANES), 1)
    idx = jnp.zeros((NSA_HEADS, LANES), F32)
    for k in range(min(SEL_TOP, n_sel)):
        ik = jnp.sum(jnp.where((sel > 0.5) & (before == k), jj, 0.0), axis=1, keepdims=True)
        idx = jnp.where(lane == k, ik, idx)
    idx_ref[0] = idx.astype(jnp.int32)
    ocmp_ref[0] = jnp.concatenate([o_all, jnp.zeros((NSA_HEADS, LANES - NSA_HD), F32)], axis=1)


def _s1(pool_t, page_table, q3, wts, q_pos):
    B, n_pages = page_table.shape
    n_chunks = n_pages * PAGE_SIZE // CMP_STRIDE

    per_b = lambda b, pt: (b, 0, 0)
    cst = lambda shape: pl.BlockSpec(shape, lambda b, pt: (0,) * len(shape), pipeline_mode=pl.Buffered(1))
    grid_spec = pltpu.PrefetchScalarGridSpec(
        num_scalar_prefetch=1, grid=(B,),
        in_specs=[pl.BlockSpec(memory_space=pl.ANY), pl.BlockSpec((1, 1, Q_NSA), per_b), cst(wts["wbd"].shape), cst(wts["pe"].shape),
                  cst(wts["w1"].shape), cst(wts["w2"].shape)],
        out_specs=[pl.BlockSpec((1, NSA_HEADS, LANES), per_b), pl.BlockSpec((1, NSA_HEADS, LANES), per_b)],
        scratch_shapes=[pltpu.VMEM((2, n_pages, KV_ROW, PAGE_SIZE), F32), pltpu.SemaphoreType.DMA((2,)),
                        pltpu.VMEM((2, n_chunks * CHUNK_PITCH, LANES), F32), pltpu.VMEM((n_chunks + 8, CMP_HID), F32)],
    )
    return pl.pallas_call(
        functools.partial(_s1_kernel, n_pages=n_pages, q_pos=q_pos), grid_spec=grid_spec,
        out_shape=[jax.ShapeDtypeStruct((B, NSA_HEADS, LANES), F32), jax.ShapeDtypeStruct((B, NSA_HEADS, LANES), jnp.int32)],
        compiler_params=_cparams(("arbitrary",)), name="sample_compress_select",
    )(page_table, pool_t, q3, wts["wbd"], wts["pe"], wts["w1"], wts["w2"])


def _s2_kernel(phys_ref, meta_ref, pool_ref, q_ref, ocmp_ref, small_ref, nsel_ref, nwin_ref, nwint_ref, cwin_ref, o_ref, swin_ref,
               pg_ref, sem_ref, *, n_top, win_len):
    n_blk = NSA_KV_HEADS * n_top
    b = pl.program_id(0)
    slot = b % 2

    def page_copy(seq, sl, j):
        return pltpu.make_async_copy(pool_ref.at[phys_ref[seq, j]], pg_ref.at[sl, j], sem_ref.at[sl])

    @pl.when(b == 0)
    def _():
        for j in range(n_blk):
            page_copy(0, 0, j).start()

    @pl.when(b + 1 < pl.num_programs(0))
    def _():
        for j in range(n_blk):
            page_copy(b + 1, 1 - slot, j).start()

    for j in range(n_blk):
        page_copy(b, slot, j).wait()
    qrow = q_ref[0].astype(F32)
    q8 = jnp.concatenate([qrow[:, h * NSA_HD:(h + 1) * NSA_HD] for h in range(NSA_HEADS)], axis=0)
    row = lax.broadcasted_iota(jnp.int32, (NSA_HEADS, 1), 0)
    lane = lax.broadcasted_iota(jnp.int32, (1, PAGE_SIZE), 1)
    nsel = nsel_ref[0]
    nwin = nwin_ref[0]
    cwin = cwin_ref[0]
    r = lax.broadcasted_iota(jnp.int32, (1, win_len), 1)
    allow_w = (r > win_len - WINDOW) & (r <= win_len)
    o_sel = jnp.zeros((NSA_HEADS, NSA_HD), F32)
    o_win = jnp.zeros((NSA_HEADS, NSA_HD), F32)
    for g in range(NSA_KV_HEADS):
        mine = (row // NSA_GROUP) == g
        ksl = slice(g * NSA_HD, (g + 1) * NSA_HD)
        vsl = slice(KV_ROW // 2 + g * NSA_HD, KV_ROW // 2 + (g + 1) * NSA_HD)
        kts, vts, masks = [], [], []
        has_new = jnp.zeros((1, 1), F32)
        for k in range(n_top):
            m = meta_ref[b, g * n_top + k]
            kts.append(pg_ref[slot, g * n_top + k, ksl, :])
            vts.append(pg_ref[slot, g * n_top + k, vsl, :])
            masks.append((lane // SEL_BLOCK) == m)
            has_new = has_new + jnp.where(m == 2, 1.0, 0.0)
        kt = jnp.concatenate(kts, axis=1)
        vt = jnp.concatenate(vts, axis=1)
        allow = jnp.concatenate(masks, axis=1)
        s = jnp.where(allow, _dot(q8, kt), -1e30)
        s_new = jnp.where(has_new > 0.5, jnp.sum(q8 * nsel[:, ksl], axis=1, keepdims=True), -1e30)
        mx = jnp.maximum(jnp.max(s, axis=1, keepdims=True), s_new)
        e = jnp.where(allow, jnp.exp(s - mx), 0.0)
        e_new = jnp.where(has_new > 0.5, jnp.exp(s_new - mx), 0.0)
        den = jnp.sum(e, axis=1, keepdims=True) + e_new
        og = (_dot_nt(e, vt) + e_new * nsel[:, vsl]) / den
        o_sel = jnp.where(mine, og, o_sel)
        s = jnp.where(allow_w, _dot(q8, cwin[ksl, :]), -1e30)
        s_new = jnp.sum(q8 * nwin[:, ksl], axis=1, keepdims=True)
        mx = jnp.maximum(jnp.max(s, axis=1, keepdims=True), s_new)
        e = jnp.where(allow_w, jnp.exp(s - mx), 0.0)
        e_new = jnp.exp(s_new - mx)
        den = jnp.sum(e, axis=1, keepdims=True) + e_new
        og = (_dot_nt(e, cwin[vsl, :]) + e_new * nwin[:, vsl]) / den
        o_win = jnp.where(mine, og, o_win)
    o_cmp = ocmp_ref[0][:, :NSA_HD]
    gsig = jax.nn.sigmoid(small_ref[0][:, G_NSA_OFF:G_NSA_OFF + 3 * NSA_HEADS])
    cols = []
    for h in range(NSA_HEADS):
        cols.append(gsig[:, 3 * h:3 * h + 1] * o_cmp[h:h + 1] + gsig[:, 3 * h + 1:3 * h + 2] * o_sel[h:h + 1]
                    + gsig[:, 3 * h + 2:3 * h + 3] * o_win[h:h + 1])
    o_ref[0] = jnp.concatenate(cols, axis=1)
    blane = lax.broadcasted_iota(jnp.int32, (1, nwint_ref.shape[1]), 1)
    newcol = jnp.sum(jnp.where(blane == b, nwint_ref[...], 0.0), axis=1, keepdims=True)
    swin_ref[0] = jnp.concatenate([cwin[:, 1:], newcol], axis=1)


def _s2(pool_t, phys, meta, q3, ocmp, small3, nsel3, nwin3, nwint, cwint, n_top):
    B = q3.shape[0]
    win_len = cwint.shape[2]
    n_blk = NSA_KV_HEADS * n_top

    per_b = lambda shape: pl.BlockSpec(shape, lambda b, ph, me: (b,) + (0,) * (len(shape) - 1))
    grid_spec = pltpu.PrefetchScalarGridSpec(
        num_scalar_prefetch=2, grid=(B,),
        in_specs=[pl.BlockSpec(memory_space=pl.ANY), per_b((1, 1, Q_NSA)), per_b((1, NSA_HEADS, LANES)), per_b((1, 1, SMALL_W)),
                  per_b((1, 1, KV_ROW)), per_b((1, 1, KV_ROW)), pl.BlockSpec(nwint.shape, lambda b, ph, me: (0, 0)),
                  per_b((1, KV_ROW, win_len))],
        out_specs=[per_b((1, 1, Q_NSA)), per_b((1, KV_ROW, win_len))],
        scratch_shapes=[pltpu.VMEM((2, n_blk, KV_ROW, PAGE_SIZE), F32), pltpu.SemaphoreType.DMA((2,))],
    )
    return pl.pallas_call(
        functools.partial(_s2_kernel, n_top=n_top, win_len=win_len), grid_spec=grid_spec,
        out_shape=[jax.ShapeDtypeStruct((B, 1, Q_NSA), F32), jax.ShapeDtypeStruct((B, KV_ROW, win_len), F32)],
        compiler_params=_cparams(("arbitrary",)), name="sample_sel_win_attn",
    )(phys, meta, pool_t, q3, ocmp, small3, nsel3, nwin3, nwint, cwint)


def _gdn_gates(b_raw, a_raw, alog, dtb):
    beta = jax.nn.sigmoid(b_raw)
    g = -jnp.exp(alog) * jax.nn.softplus(a_raw + dtb)
    return beta, g


def _l2n(x):
    return x * lax.rsqrt(jnp.sum(x * x, axis=-1, keepdims=True) + EPS)


def _gdn_prompt_kernel(qkv_ref, z_ref, small_ref, smallt_ref, cw_ref, alog_ref, dtb_ref, alogt_ref, dtbt_ref, on_ref,
                       o_ref, sfin_ref, conv_ref, s_ref, xx_ref):
    ci = pl.program_id(1)
    tc = qkv_ref.shape[1]
    C = GDN_CHUNK

    @pl.when(ci == 0)
    def _():
        s_ref[...] = jnp.zeros(s_ref.shape, F32)
        xx_ref[0:8, :] = jnp.zeros((8, CONV_DIM), F32)

    xx_ref[8:8 + tc, :] = qkv_ref[0]
    y = xx_ref[pl.ds(8 - (CONV_W - 1), tc), :] * cw_ref[0:1, :]
    for j in range(1, CONV_W):
        y = y + xx_ref[pl.ds(8 - (CONV_W - 1) + j, tc), :] * cw_ref[j:j + 1, :]
    c = jax.nn.silu(y)
    tail = xx_ref[tc:tc + 8, :]
    conv_ref[0] = tail
    xx_ref[0:8, :] = tail
    small = small_ref[0]
    beta, gcol = _gdn_gates(small[:, B_OFF:B_OFF + GDN_HEADS], small[:, A_OFF:A_OFF + GDN_HEADS], alog_ref[...], dtb_ref[...])
    _, grow = _gdn_gates(smallt_ref[0][0:GDN_HEADS], smallt_ref[0][GDN_HEADS:2 * GDN_HEADS], alogt_ref[...], dtbt_ref[...])
    z = z_ref[0]
    ii = lax.broadcasted_iota(jnp.int32, (C, C), 0)
    jj = lax.broadcasted_iota(jnp.int32, (C, C), 1)
    tril = ii >= jj
    strict = ii > jj
    eye = jnp.where(ii == jj, 1.0, 0.0)
    hcs = [(cc, h) for cc in range(tc // C) for h in range(GDN_HEADS)]
    loc = {}
    for cc, h in hcs:
        rs = slice(cc * C, (cc + 1) * C)
        qh = _l2n(c[rs, h * GDN_DK:(h + 1) * GDN_DK]) * (GDN_DK ** -0.5)
        kh = _l2n(c[rs, GDN_QK + h * GDN_DK:GDN_QK + (h + 1) * GDN_DK])
        vh = c[rs, 2 * GDN_QK + h * GDN_DV:2 * GDN_QK + (h + 1) * GDN_DV]
        bcol = beta[rs, h:h + 1]
        g_c = gcol[rs, h:h + 1]
        g_r = grow[h:h + 1, rs]
        dec_c = jnp.sum(jnp.where(tril, g_r, 0.0), axis=1, keepdims=True)
        dec_r = jnp.sum(jnp.where(ii <= jj, g_c, 0.0), axis=0, keepdims=True)
        lmask = jnp.where(tril, jnp.exp(jnp.where(tril, dec_c - dec_r, 0.0)), 0.0)
        kb = kh * bcol
        edec = jnp.exp(dec_c)
        dlast = dec_c[C - 1:C, :]
        loc[cc, h] = dict(
            m=-jnp.where(strict, _dot_nt(kb, kh) * lmask, 0.0), rhs=jnp.concatenate([vh * bcol, kb * edec], axis=1),
            attn=_dot_nt(qh, kh) * lmask, qe=qh * edec, kdt=(kh * jnp.exp(dlast - dec_c)).T, elast=jnp.exp(dlast))
    tinv = {k: eye + loc[k]["m"] for k in hcs}
    mpow = {k: _dot3(loc[k]["m"], loc[k]["m"]) for k in hcs}
    n_steps = (C - 1).bit_length() - 1
    for step in range(n_steps):
        for k in hcs:
            if step < n_steps - 1:
                r = _dot3(mpow[k], jnp.concatenate([mpow[k], tinv[k]], axis=1))
                mpow[k], tinv[k] = r[:, :C], tinv[k] + r[:, C:]
            else:
                tinv[k] = tinv[k] + _dot3(mpow[k], tinv[k])
    uw = {k: _dot3(tinv[k], loc[k]["rhs"]) for k in hcs}
    for cc in range(tc // C):
        rs = slice(cc * C, (cc + 1) * C)
        for h in range(GDN_HEADS):
            d = loc[cc, h]
            s_old = s_ref[h]
            ws_qs = _dot(jnp.concatenate([uw[cc, h][:, GDN_DV:], d["qe"]], axis=0), s_old)
            v_new = uw[cc, h][:, :GDN_DV] - ws_qs[:C]
            o = ws_qs[C:] + _dot(d["attn"], v_new)
            s_ref[h] = s_old * d["elast"] + _dot(d["kdt"], v_new)
            o = _rms(o) * on_ref[...] * jax.nn.silu(z[rs, h * GDN_DV:(h + 1) * GDN_DV])
            o_ref[0, rs, h * GDN_DV:(h + 1) * GDN_DV] = o

    @pl.when(ci == pl.num_programs(1) - 1)
    def _():
        sfin_ref[0] = s_ref[...]


def _gdn_prompt(qkv, z, small, wts, tc=256):
    B, T, _ = qkv.shape
    smallt = jnp.transpose(small[:, :, B_OFF:B_OFF + 2 * GDN_HEADS], (0, 2, 1))
    tokb = lambda w: pl.BlockSpec((1, tc, w), lambda b, c: (b, c, 0))
    return pl.pallas_call(
        _gdn_prompt_kernel, grid=(B, T // tc),
        in_specs=[tokb(CONV_DIM), tokb(GDN_V), tokb(SMALL_W), pl.BlockSpec((1, 2 * GDN_HEADS, tc), lambda b, c: (b, 0, c)),
                  _const_spec((CONV_W, CONV_DIM)), _const_spec((1, GDN_HEADS)), _const_spec((1, GDN_HEADS)),
                  _const_spec((GDN_HEADS, 1)), _const_spec((GDN_HEADS, 1)), _const_spec((1, GDN_DV))],
        out_specs=[tokb(GDN_V), pl.BlockSpec((1, GDN_HEADS, GDN_DK, GDN_DV), lambda b, c: (b, 0, 0, 0)),
                   pl.BlockSpec((1, 8, CONV_DIM), lambda b, c: (b, 0, 0))],
        out_shape=[jax.ShapeDtypeStruct((B, T, GDN_V), F32), jax.ShapeDtypeStruct((B, GDN_HEADS, GDN_DK, GDN_DV), F32),
                   jax.ShapeDtypeStruct((B, 8, CONV_DIM), F32)],
        scratch_shapes=[pltpu.VMEM((GDN_HEADS, GDN_DK, GDN_DV), F32), pltpu.VMEM((tc + 8, CONV_DIM), F32)],
        compiler_params=_cparams(("parallel", "arbitrary")), name="gdn_prompt",
    )(qkv, z, small, smallt, wts["conv_w"], wts["alog"], wts["dtb"], wts["alog_t"], wts["dtb_t"], wts["onorm"])


def _gdn_sample_kernel(xx_ref, z_ref, small_ref, cw_ref, alog_ref, dtb_ref, on_ref, s_ref,
                       o_ref, sout_ref, qt_ref, kt_ref, wt_ref, u_ref, sc_ref):
    b = pl.program_id(0)
    nb = z_ref.shape[0]

    @pl.when(b == 0)
    def _():
        y = xx_ref[0] * cw_ref[0:1, :]
        for j in range(1, CONV_W):
            y = y + xx_ref[j] * cw_ref[j:j + 1, :]
        c = jax.nn.silu(y)
        small = small_ref[...]
        beta, g = _gdn_gates(small[:, B_OFF:B_OFF + GDN_HEADS], small[:, A_OFF:A_OFF + GDN_HEADS], alog_ref[...], dtb_ref[...])
        a = jnp.exp(g)
        attn = []
        for h in range(GDN_HEADS):
            qh = _l2n(c[:, h * GDN_DK:(h + 1) * GDN_DK]) * (GDN_DK ** -0.5)
            kh = _l2n(c[:, GDN_QK + h * GDN_DK:GDN_QK + (h + 1) * GDN_DK])
            vh = c[:, 2 * GDN_QK + h * GDN_DV:2 * GDN_QK + (h + 1) * GDN_DV]
            bh, ah = beta[:, h:h + 1], a[:, h:h + 1]
            qt_ref[h] = (qh * ah).T
            kt_ref[h] = kh.T
            wt_ref[h] = (kh * bh * ah).T
            u_ref[h] = vh * bh
            attn.append(jnp.sum(qh * kh, axis=1, keepdims=True))
        sc_ref[...] = jnp.concatenate([a] + attn + [jnp.zeros((nb, LANES - 2 * GDN_HEADS), F32)], axis=1)

    lane = lax.broadcasted_iota(jnp.int32, (1, nb), 1)
    pick = lane == b
    sc = sc_ref[pl.ds(b, 1), :]
    zrow = z_ref[pl.ds(b, 1), :]
    for h in range(GDN_HEADS):
        wcol = jnp.sum(jnp.where(pick, wt_ref[h], 0.0), axis=1, keepdims=True)
        qcol = jnp.sum(jnp.where(pick, qt_ref[h], 0.0), axis=1, keepdims=True)
        kcol = jnp.sum(jnp.where(pick, kt_ref[h], 0.0), axis=1, keepdims=True)
        s_old = s_ref[0, h]
        v_new = u_ref[h, pl.ds(b, 1), :] - jnp.sum(s_old * wcol, axis=0, keepdims=True)
        o = jnp.sum(s_old * qcol, axis=0, keepdims=True) + sc[:, GDN_HEADS + h:GDN_HEADS + h + 1] * v_new
        sout_ref[0, h] = s_old * sc[:, h:h + 1] + kcol * v_new
        o = _rms(o) * on_ref[...] * jax.nn.silu(zrow[:, h * GDN_DV:(h + 1) * GDN_DV])
        o_ref[0, :, h * GDN_DV:(h + 1) * GDN_DV] = o


def _gdn_sample(xx4, z2, small2, state, wts):
    nb = z2.shape[0]
    cst = lambda shape: pl.BlockSpec(shape, lambda b: (0,) * len(shape))
    sspec = pl.BlockSpec((1, GDN_HEADS, GDN_DK, GDN_DV), lambda b: (b, 0, 0, 0))
    return pl.pallas_call(
        _gdn_sample_kernel, grid=(nb,),
        in_specs=[cst(xx4.shape), cst(z2.shape), cst(small2.shape), cst((CONV_W, CONV_DIM)), cst((1, GDN_HEADS)),
                  cst((1, GDN_HEADS)), cst((1, GDN_DV)), sspec],
        out_specs=[pl.BlockSpec((1, 1, GDN_V), lambda b: (b, 0, 0)), sspec],
        out_shape=[jax.ShapeDtypeStruct((nb, 1, GDN_V), F32), jax.ShapeDtypeStruct(state.shape, F32)],
        scratch_shapes=[pltpu.VMEM((GDN_HEADS, GDN_DK, nb), F32)] * 3
        + [pltpu.VMEM((GDN_HEADS, nb, GDN_DV), F32), pltpu.VMEM((nb, LANES), F32)],
        compiler_params=_cparams(("arbitrary",)), name="gdn_sample",
    )(xx4, z2, small2, wts["conv_w"], wts["alog"], wts["dtb"], wts["onorm"], state)


def _mem_prompt_kernel(qm_ref, kv_ref, o_ref):
    qm = qm_ref[0]
    kv = kv_ref[0]
    for h in range(MEM_HEADS):
        sl = slice(h * MEM_HD, (h + 1) * MEM_HD)
        s = _dot_nt(qm[:, sl], kv[:, sl]) * MEM_SCALE
        e = jnp.exp(s - jnp.max(s, axis=-1, keepdims=True))
        p = e / jnp.sum(e, axis=-1, keepdims=True)
        o_ref[0, :, sl] = _dot(p, kv[:, MEM_Q + h * MEM_HD:MEM_Q + (h + 1) * MEM_HD])


def _mem_prompt(qm, mkv, tq=256):
    B, T, _ = qm.shape
    M = mkv.shape[1]
    return pl.pallas_call(
        _mem_prompt_kernel, grid=(B, T // tq),
        in_specs=[pl.BlockSpec((1, tq, MEM_Q), lambda b, t: (b, t, 0)), pl.BlockSpec((1, M, 2 * MEM_Q), lambda b, t: (b, 0, 0))],
        out_specs=pl.BlockSpec((1, tq, MEM_Q), lambda b, t: (b, t, 0)),
        out_shape=jax.ShapeDtypeStruct((B, T, MEM_Q), F32),
        compiler_params=_cparams(("parallel", "parallel")), name="mem_attn_prompt",
    )(qm, mkv)


def _mem_sample_kernel(qm_ref, kv_ref, o_ref):
    q = qm_ref[0].astype(F32)
    for h in range(MEM_HEADS):
        sl = slice(h * MEM_HD, (h + 1) * MEM_HD)
        k = kv_ref[0, :, 0, h, :]
        v = kv_ref[0, :, 1, h, :]
        s = jnp.sum(k * q[:, sl], axis=1, keepdims=True) * MEM_SCALE
        e = jnp.exp(s - jnp.max(s, axis=0, keepdims=True))
        p = e / jnp.sum(e, axis=0, keepdims=True)
        o_ref[0, :, sl] = jnp.sum(v * p, axis=0, keepdims=True)


def _mem_sample(qm3, cache):
    B, M = cache.shape[0], cache.shape[1]
    return pl.pallas_call(
        _mem_sample_kernel, grid=(B,),
        in_specs=[pl.BlockSpec((1, 1, MEM_Q), lambda b: (b, 0, 0)),
                  pl.BlockSpec((1, M, 2, MEM_HEADS, MEM_HD), lambda b: (b, 0, 0, 0, 0))],
        out_specs=pl.BlockSpec((1, 1, MEM_Q), lambda b: (b, 0, 0)),
        out_shape=jax.ShapeDtypeStruct((B, 1, MEM_Q), F32),
        compiler_params=_cparams(("parallel",)), name="mem_attn_sample",
    )(qm3, cache)


def _merge_ffn_kernel(on_ref, og_ref, om_ref, gate_ref, x_ref, wbn_ref, wbg_ref, wbm_ref, wout_ref, gffn_ref, w1_ref, w2_ref, y_ref):
    merged = (gate_ref[:, 0:D_MODEL] * _dot(on_ref[...], wbn_ref[...])
              + gate_ref[:, D_MODEL:2 * D_MODEL] * _dot(og_ref[...], wbg_ref[...])
              + gate_ref[:, 2 * D_MODEL:3 * D_MODEL] * _dot(om_ref[...], wbm_ref[...]))
    h = x_ref[...] + _dot(merged, wout_ref[...])
    f = _dot(_rms(h) * gffn_ref[...], w1_ref[...])
    f = jnp.square(jnp.maximum(f, 0.0))
    y_ref[...] = h + _dot(f, w2_ref[...])


def _merge_ffn(o_nsa, o_gdn, o_mem, gates, x2d, wts, tm):
    n = x2d.shape[0]
    tok = lambda w: pl.BlockSpec((tm, w), lambda i: (i, 0))
    return pl.pallas_call(
        _merge_ffn_kernel, grid=(n // tm,),
        in_specs=[tok(Q_NSA), tok(GDN_V), tok(MEM_Q), tok(N_BRANCH * D_MODEL), tok(D_MODEL),
                  _const_spec((Q_NSA, D_MODEL)), _const_spec((GDN_V, D_MODEL)), _const_spec((MEM_Q, D_MODEL)),
                  _const_spec((D_MODEL, D_MODEL)), _const_spec((1, D_MODEL)), _const_spec((D_MODEL, D_FF)), _const_spec((D_FF, D_MODEL))],
        out_specs=tok(D_MODEL),
        out_shape=jax.ShapeDtypeStruct((n, D_MODEL), F32),
        compiler_params=_cparams(("parallel",)), name="merge_ffn",
    )(o_nsa, o_gdn, o_mem, gates, x2d, wts["w_br_nsa"], wts["w_br_gdn"], wts["w_br_mem"], wts["w_out"], wts["g_ffn"],
      wts["w_ff1"], wts["w_ff2"])


def _prep_weights(g_mix, w_in, nsa_q_norm, nsa_k_norm, cmp_pe, cmp_w1, cmp_w2, gdn_conv_w, gdn_A_log, gdn_dt_bias, gdn_o_norm,
                  g_mem, w_mem_kv, mem_q_norm, mem_k_norm, w_br_nsa, w_br_gdn, w_br_mem, w_out, g_ffn, w_ff1, w_ff2):
    offs = [0]
    for s in IN_SPLITS:
        offs.append(offs[-1] + s)
    wt = w_in.T
    seg = lambda i: wt[offs[i]:offs[i + 1]]
    small = jnp.concatenate([seg(2), seg(4), seg(5)], axis=0)
    small = jnp.pad(small, ((0, SMALL_W - small.shape[0]), (0, 0)))
    kv = seg(1)
    k_of = lambda c: kv[c * KV_ROW:c * KV_ROW + KV_ROW // 2]
    w_tok = jnp.concatenate([seg(0), seg(3), seg(6), seg(7), seg(8), small, k_of(1), k_of(2)], axis=0).T.astype(BF16)
    w_ft = jnp.concatenate([kv, seg(0), jnp.pad(seg(2), ((0, G_ROWS - 3 * NSA_HEADS), (0, 0)))], axis=0).astype(BF16)
    R = CMP_BLOCK // CMP_STRIDE
    w1r = cmp_w1.reshape(2, R, CMP_STRIDE, NSA_HD, CMP_HID)
    wbd = jnp.einsum("krsdf,gh->ksgdhrf", w1r, jnp.eye(NSA_KV_HEADS, dtype=F32))
    wbd = wbd.reshape(2, CMP_STRIDE * NSA_KV_HEADS * NSA_HD, NSA_KV_HEADS * R * CMP_HID).astype(BF16)
    row = lambda v: v.reshape(1, -1)
    return dict(
        g_mix=row(g_mix), w_tok=w_tok, w_ft=w_ft, qn=row(jnp.tile(nsa_q_norm, LANES // NSA_HD)), qn_col=nsa_q_norm.reshape(NSA_HD, 1),
        kn=nsa_k_norm.reshape(3, NSA_HD, 1), kn_row=jnp.tile(nsa_k_norm, (1, LANES // NSA_HD)), mqn=row(mem_q_norm), mkn=row(mem_k_norm), g_mem=row(g_mem),
        w_mem_kv=w_mem_kv.astype(BF16), wbd=wbd,
        pe=jnp.broadcast_to(cmp_pe.reshape(2, 1, CMP_BLOCK * NSA_HD), (2, 8, CMP_BLOCK * NSA_HD)),
        w1=cmp_w1.astype(BF16), w2=cmp_w2.astype(BF16), conv_w=gdn_conv_w,
        alog=row(gdn_A_log), dtb=row(gdn_dt_bias), alog_t=gdn_A_log.reshape(-1, 1), dtb_t=gdn_dt_bias.reshape(-1, 1),
        onorm=row(gdn_o_norm), w_br_nsa=w_br_nsa.astype(BF16), w_br_gdn=w_br_gdn.astype(BF16), w_br_mem=w_br_mem.astype(BF16),
        w_out=w_out.astype(BF16), g_ffn=row(g_ffn), w_ff1=w_ff1.astype(BF16), w_ff2=w_ff2.astype(BF16))


def _rows_5d(kt):
    B, _, N = kt.shape
    return jnp.transpose(kt.reshape(B, 2, NSA_KV_HEADS, NSA_HD, N), (0, 4, 1, 2, 3))


def _feature_major(rows):
    B, N = rows.shape[0], rows.shape[1]
    return jnp.transpose(rows, (0, 2, 3, 4, 1)).reshape(B, KV_ROW, N)


def kernel(x_prompt, x_sample, cache_cmp_kv, cache_sel_kv, cache_win_kv, state_gdn, state_gdn_conv, cache_mem_kv, page_table, mem_prompt, g_mix, w_in, nsa_q_norm, nsa_k_norm, cmp_pe, cmp_w1, cmp_w2, gdn_conv_w, gdn_A_log, gdn_dt_bias, gdn_o_norm, g_mem, w_mem_kv, mem_q_norm, mem_k_norm, w_br_nsa, w_br_gdn, w_br_mem, w_out, g_ffn, w_ff1, w_ff2):
    wts = _prep_weights(g_mix, w_in, nsa_q_norm, nsa_k_norm, cmp_pe, cmp_w1, cmp_w2, gdn_conv_w, gdn_A_log, gdn_dt_bias,
                        gdn_o_norm, g_mem, w_mem_kv, mem_q_norm, mem_k_norm, w_br_nsa, w_br_gdn, w_br_mem, w_out, g_ffn, w_ff1, w_ff2)
    B, T, D = x_prompt.shape
    nb = x_sample.shape[0]
    assert x_sample.shape[1] == 1 and T % 256 == 0 and T >= WINDOW + 128
    n_pages = page_table.shape[1]
    past = n_pages * PAGE_SIZE

    p = _inproj(x_prompt, jnp.arange(T, dtype=jnp.int32), wts, tm=256, decode=False)
    mkv = _memkv(mem_prompt.reshape(-1, D), wts).reshape(B, -1, 2 * MEM_Q)
    ckv = _pcompress(p["kct"], wts)
    o_nsa = _pattn_t(p, ckv)
    o_gdn, p_state, conv_tail = _gdn_prompt(p["qkv"], p["z"], p["small"], wts)
    o_mem = _mem_prompt(p["qm"], mkv)
    y_prompt = _merge_ffn(o_nsa.reshape(-1, Q_NSA), o_gdn.reshape(-1, GDN_V), o_mem.reshape(-1, MEM_Q),
                          p["gates"].reshape(-1, N_BRANCH * D), x_prompt.reshape(-1, D), wts, tm=256).reshape(B, T, D)
    p_cmp, p_sel = _rows_5d(p["kct"]), _rows_5d(p["kst"])
    p_win = _rows_5d(p["kwt"][:, :, T - min(WINDOW, T):])
    p_conv = conv_tail[:, 8 - (CONV_W - 1):, :]
    p_mem_kv = mkv.reshape(B, -1, 2, MEM_HEADS, MEM_HD)

    xs = x_sample.reshape(1, nb, D)
    s = _inproj(xs, jnp.full((nb,), past, jnp.int32), wts, tm=nb, decode=True)
    sq, sqkv, sz, sqm, sgates, ssmall, skct, skst, skwt = (s[k] for k in ("q", "qkv", "z", "qm", "gates", "small", "kct", "kst", "kwt"))
    pool_cmp = _feature_major(cache_cmp_kv)
    pool_sel = _feature_major(cache_sel_kv)
    q3 = sq.reshape(nb, 1, Q_NSA)
    ocmp, idx = _s1(pool_cmp, page_table, q3, wts, q_pos=past)
    n_sel = -(-(past + 1) // SEL_BLOCK)
    n_top = min(SEL_TOP, n_sel)
    idx = idx[:, :NSA_KV_HEADS, :n_top].reshape(nb, NSA_KV_HEADS * n_top)
    blk_per_page = PAGE_SIZE // SEL_BLOCK
    is_new = idx * SEL_BLOCK >= past
    page = jnp.take_along_axis(page_table, jnp.where(is_new, 0, idx // blk_per_page), axis=1)
    meta = jnp.where(is_new, blk_per_page, idx % blk_per_page).astype(jnp.int32)
    tokrow = lambda kt: jnp.transpose(kt[0], (1, 0)).reshape(nb, 1, KV_ROW)
    cwint = _feature_major(cache_win_kv)
    o_nsa_s, swin = _s2(pool_sel, page.astype(jnp.int32), meta, q3, ocmp, ssmall.reshape(nb, 1, SMALL_W), tokrow(skst), tokrow(skwt),
                        skwt[0], cwint, n_top)
    xx4 = jnp.concatenate([jnp.transpose(state_gdn_conv, (1, 0, 2)), sqkv], axis=0)
    o_gdn_s, s_state = _gdn_sample(xx4, sz[0], ssmall[0], state_gdn, wts)
    o_mem_s = _mem_sample(sqm.reshape(nb, 1, MEM_Q), cache_mem_kv)
    y_sample = _merge_ffn(o_nsa_s.reshape(nb, Q_NSA), o_gdn_s.reshape(nb, GDN_V), o_mem_s.reshape(nb, MEM_Q), sgates[0], x_sample.reshape(nb, D),
                          wts, tm=nb).reshape(nb, 1, D)
    s_cmp = jnp.transpose(skct[0], (1, 0)).reshape(nb, 1, 2, NSA_KV_HEADS, NSA_HD)
    s_sel = jnp.transpose(skst[0], (1, 0)).reshape(nb, 1, 2, NSA_KV_HEADS, NSA_HD)
    s_win = _rows_5d(swin)
    s_conv = jnp.transpose(xx4[1:], (1, 0, 2))
    return (y_prompt, y_sample, p_cmp, p_sel, p_win, p_state, p_conv, p_mem_kv, s_cmp, s_sel, s_win, s_state, s_conv)
```

```python
import functools

import jax
import jax.numpy as jnp
from jax import lax
from jax.experimental import pallas as pl
from jax.experimental.pallas import tpu as pltpu

F32 = jnp.float32
BF16 = jnp.bfloat16
HI = lax.Precision.HIGHEST

D_MODEL = 1024
PAGE_SIZE = 128
NSA_HEADS = 8
NSA_KV_HEADS = 2
NSA_HD = 64
NSA_GROUP = NSA_HEADS // NSA_KV_HEADS
NSA_SCALE = NSA_HD ** -0.5
LOG2_E = 1.4426950408889634
CMP_BLOCK = 32
CMP_STRIDE = 16
CMP_HID = 128
SEL_BLOCK = 64
SEL_TOP = 16
WINDOW = 512
FORCE_SCORE = 1e9
GDN_HEADS = 4
GDN_DK = 128
GDN_DV = 128
CONV_W = 4
GDN_CHUNK = 64
MEM_HEADS = 4
MEM_HD = 128
MEM_SCALE = MEM_HD ** -0.5
D_FF = 4 * D_MODEL
ROPE_THETA = 10000.0
EPS = 1e-6

Q_NSA = NSA_HEADS * NSA_HD
KV_ROW = 2 * NSA_KV_HEADS * NSA_HD
GDN_QK = GDN_HEADS * GDN_DK
GDN_V = GDN_HEADS * GDN_DV
CONV_DIM = 2 * GDN_QK + GDN_V
MEM_Q = MEM_HEADS * MEM_HD
N_BRANCH = 3
IN_SPLITS = (Q_NSA, 3 * KV_ROW, 3 * NSA_HEADS, CONV_DIM, GDN_HEADS, GDN_HEADS, GDN_V, MEM_Q, N_BRANCH * D_MODEL)

LANES = 128
SMALL_W = LANES
G_NSA_OFF, B_OFF, A_OFF = 0, 3 * NSA_HEADS, 3 * NSA_HEADS + GDN_HEADS
TOK_Q, TOK_QKV, TOK_Z, TOK_QM, TOK_GBR, TOK_SMALL, TOK_KTOK = 0, 512, 2048, 2560, 3072, 6144, 6272
TOK_W = TOK_KTOK + 2 * LANES
FT_KV, FT_Q, FT_G = 0, 3 * KV_ROW, 3 * KV_ROW + Q_NSA
G_ROWS = 32
FT_W = FT_G + G_ROWS
V7X_VMEM_LIMIT = 56 * 1024 * 1024


def _cparams(sem):
    return pltpu.CompilerParams(dimension_semantics=sem, vmem_limit_bytes=V7X_VMEM_LIMIT)


def _dot(a, b):
    return jnp.dot(a.astype(BF16), b.astype(BF16), preferred_element_type=F32)


def _dot_nt(a, b):
    return lax.dot_general(a.astype(BF16), b.astype(BF16), (((1,), (1,)), ((), ())), preferred_element_type=F32)


def _dot_hi(a, b):
    return jnp.dot(a, b, precision=HI, preferred_element_type=F32)


def _split_bf16(a):
    hi = a.astype(BF16)
    return hi, (a - hi.astype(F32)).astype(BF16)


def _dot3(a, b):
    ah, al = _split_bf16(a)
    bh, bl = _split_bf16(b)
    d = lambda x, y: jnp.dot(x, y, preferred_element_type=F32)
    return d(ah, bh) + (d(ah, bl) + d(al, bh))


def _rms(x, axis=-1):
    return x * lax.rsqrt(jnp.mean(x * x, axis=axis, keepdims=True) + EPS)


def _const_spec(shape):
    nd = len(shape)
    return pl.BlockSpec(shape, lambda *_: (0,) * nd, pipeline_mode=pl.Buffered(1))


def _inproj_kernel(x_ref, gmix_ref, wtok_ref, wft_ref, qn_ref, qnc_ref, kn_ref, knr_ref, cq_ref, sq_ref, ck_ref, sk_ref, mqn_ref, blk_ref,
                   qkv_ref, z_ref, qm_ref, gate_ref, small_ref, kc_ref, ks_ref, kw_ref, *extra_refs, decode):
    x = x_ref[0]
    ub = (_rms(x) * gmix_ref[...]).astype(BF16)
    tm = x.shape[0]
    lane = lax.broadcasted_iota(jnp.int32, (tm, LANES), 1)
    lo = lane < NSA_HD
    first_half = (lane % NSA_HD) < (NSA_HD // 2)

    def norm_rope_slab(col0, gain):
        qs = jnp.dot(ub, wtok_ref[:, col0:col0 + LANES], preferred_element_type=F32)
        sq = qs * qs
        ss_lo = jnp.sum(jnp.where(lo, sq, 0.0), axis=-1, keepdims=True)
        ss_hi = jnp.sum(jnp.where(lo, 0.0, sq), axis=-1, keepdims=True)
        r = jnp.where(lo, lax.rsqrt(ss_lo / NSA_HD + EPS), lax.rsqrt(ss_hi / NSA_HD + EPS))
        qs = qs * r * gain
        rot = jnp.where(first_half, -pltpu.roll(qs, LANES - NSA_HD // 2, axis=1), pltpu.roll(qs, NSA_HD // 2, axis=1))
        return qs * cq_ref[...] + rot * sq_ref[...]

    if decode:
        (q_ref,) = extra_refs
        for i in range(Q_NSA // LANES):
            q_ref[0, :, i * LANES:(i + 1) * LANES] = (norm_rope_slab(TOK_Q + i * LANES, qn_ref[...]) * NSA_SCALE).astype(BF16)
    else:
        kstok0_ref, kstok1_ref, kwtok_ref, qt_ref, gt_ref = extra_refs
        ks = norm_rope_slab(TOK_KTOK, knr_ref[1:2, :])
        kstok0_ref[0] = jnp.where(lo, ks, blk_ref[...]).astype(BF16)
        kstok1_ref[0] = jnp.where(lo, pltpu.roll(ks, NSA_HD, axis=1), blk_ref[...]).astype(BF16)
        kwtok_ref[0] = norm_rope_slab(TOK_KTOK + LANES, knr_ref[2:3, :]).astype(BF16)
    qkv_ref[0] = jnp.dot(ub, wtok_ref[:, TOK_QKV:TOK_Z], preferred_element_type=F32)
    z_ref[0] = jnp.dot(ub, wtok_ref[:, TOK_Z:TOK_QM], preferred_element_type=F32)
    for h in range(MEM_HEADS):
        qm = jnp.dot(ub, wtok_ref[:, TOK_QM + h * MEM_HD:TOK_QM + (h + 1) * MEM_HD], preferred_element_type=F32)
        qm_ref[0, :, h * MEM_HD:(h + 1) * MEM_HD] = (_rms(qm) * mqn_ref[...]).astype(BF16)
    for i in range(N_BRANCH):
        gb = jnp.dot(ub, wtok_ref[:, TOK_GBR + i * D_MODEL:TOK_GBR + (i + 1) * D_MODEL], preferred_element_type=F32)
        gate_ref[0, :, i * D_MODEL:(i + 1) * D_MODEL] = jax.nn.sigmoid(gb)
    small_ref[0] = jnp.dot(ub, wtok_ref[:, TOK_SMALL:TOK_SMALL + SMALL_W], preferred_element_type=F32)
    ft_rows = FT_Q if decode else FT_W
    ft = lax.dot_general(wft_ref[0:ft_rows, :], ub, (((1,), (1,)), ((), ())), preferred_element_type=F32)
    cos = ck_ref[...]
    sin = sk_ref[...]
    half = NSA_HD // 2

    def norm_rope_rows(row0, gain_col):
        kh = _rms(ft[row0:row0 + NSA_HD, :], axis=0) * gain_col
        x1, x2 = kh[:half], kh[half:]
        return x1 * cos - x2 * sin, x2 * cos + x1 * sin

    for c, out_ref in enumerate((kc_ref, ks_ref, kw_ref)):
        base = FT_KV + c * KV_ROW
        for g in range(NSA_KV_HEADS):
            r1, r2 = norm_rope_rows(base + g * NSA_HD, kn_ref[c])
            out_ref[0, g * NSA_HD:g * NSA_HD + half, :] = r1
            out_ref[0, g * NSA_HD + half:(g + 1) * NSA_HD, :] = r2
        out_ref[0, KV_ROW // 2:, :] = ft[base + KV_ROW // 2:base + KV_ROW, :]
    if not decode:
        for h in range(NSA_HEADS):
            r1, r2 = norm_rope_rows(FT_Q + h * NSA_HD, qnc_ref[...])
            qt_ref[0, h * NSA_HD:(h + 1) * NSA_HD, :] = (jnp.concatenate([r1, r2], axis=0) * (NSA_SCALE * LOG2_E)).astype(BF16)
        gt_ref[0] = jax.nn.sigmoid(ft[FT_G:FT_G + G_ROWS, :])


def _inproj(x, pos, wts, tm, decode):
    B, T, _ = x.shape
    half = NSA_HD // 2
    inv = ROPE_THETA ** (-jnp.arange(half, dtype=F32) / half)
    ang = pos.astype(F32)[:, None] * inv[None, :]
    cos, sin = jnp.cos(ang), jnp.sin(ang)
    cq, sq = jnp.tile(cos, (1, LANES // half)), jnp.tile(sin, (1, LANES // half))
    ck, sk = cos.T, sin.T
    tok = lambda w: pl.BlockSpec((1, tm, w), lambda b, t: (b, t, 0))
    ftm = lambda r: pl.BlockSpec((1, r, tm), lambda b, t: (b, 0, t))
    tok_shape = lambda w, dt: jax.ShapeDtypeStruct((B, T, w), dt)
    ft_shape = lambda r, dt: jax.ShapeDtypeStruct((B, r, T), dt)
    names = ["qkv", "z", "qm", "gates", "small", "kct", "kst", "kwt"]
    out_specs = [tok(CONV_DIM), tok(GDN_V), tok(MEM_Q), tok(N_BRANCH * D_MODEL), tok(SMALL_W), ftm(KV_ROW), ftm(KV_ROW), ftm(KV_ROW)]
    out_shape = [tok_shape(CONV_DIM, F32), tok_shape(GDN_V, F32), tok_shape(MEM_Q, BF16), tok_shape(N_BRANCH * D_MODEL, F32),
                 tok_shape(SMALL_W, F32), ft_shape(KV_ROW, F32), ft_shape(KV_ROW, F32), ft_shape(KV_ROW, F32)]
    if decode:
        names += ["q"]
        out_specs += [tok(Q_NSA)]
        out_shape += [tok_shape(Q_NSA, BF16)]
    else:
        names += ["ks_tok0", "ks_tok1", "kw_tok", "qt", "gt"]
        out_specs += [tok(LANES), tok(LANES), tok(LANES), ftm(Q_NSA), ftm(G_ROWS)]
        out_shape += [tok_shape(LANES, BF16)] * 3 + [ft_shape(Q_NSA, BF16), ft_shape(G_ROWS, F32)]
    blk = (pos[:, None] // SEL_BLOCK + NSA_HD == jnp.arange(LANES)[None, :]).astype(F32)
    per_t = pl.BlockSpec((tm, LANES), lambda b, t: (t, 0))
    outs = pl.pallas_call(
        functools.partial(_inproj_kernel, decode=decode),
        grid=(B, T // tm),
        in_specs=[tok(D_MODEL), _const_spec((1, D_MODEL)), _const_spec((D_MODEL, TOK_W)), _const_spec((FT_W, D_MODEL)),
                  _const_spec((1, LANES)), _const_spec((NSA_HD, 1)), _const_spec((3, NSA_HD, 1)), _const_spec((3, LANES)),
                  per_t, per_t,
                  pl.BlockSpec((half, tm), lambda b, t: (0, t)), pl.BlockSpec((half, tm), lambda b, t: (0, t)),
                  _const_spec((1, MEM_HD)), per_t],
        out_specs=out_specs, out_shape=out_shape,
        compiler_params=_cparams(("parallel", "parallel")),
        name="inproj",
    )(x, wts["g_mix"], wts["w_tok"], wts["w_ft"], wts["qn"], wts["qn_col"], wts["kn"], wts["kn_row"], cq, sq, ck, sk, wts["mqn"], blk)
    return dict(zip(names, outs))


def _memkv_kernel(m_ref, g_ref, w_ref, kn_ref, o_ref):
    u = _rms(m_ref[...]) * g_ref[...]
    kv = _dot(u, w_ref[...])
    for h in range(MEM_HEADS):
        sl = slice(h * MEM_HD, (h + 1) * MEM_HD)
        o_ref[:, sl] = _rms(kv[:, sl]) * kn_ref[...]
    o_ref[:, MEM_Q:] = kv[:, MEM_Q:]


def _memkv(mem2d, wts, tm=512):
    n = mem2d.shape[0]
    tm = min(tm, n)
    return pl.pallas_call(
        _memkv_kernel, grid=(n // tm,),
        in_specs=[pl.BlockSpec((tm, D_MODEL), lambda i: (i, 0)), _const_spec((1, D_MODEL)),
                  _const_spec((D_MODEL, 2 * MEM_Q)), _const_spec((1, MEM_HD))],
        out_specs=pl.BlockSpec((tm, 2 * MEM_Q), lambda i: (i, 0)),
        out_shape=jax.ShapeDtypeStruct((n, 2 * MEM_Q), F32),
        compiler_params=_cparams(("parallel",)), name="memkv",
    )(mem2d, wts["g_mem"], wts["w_mem_kv"], wts["mkn"])


CHUNK_PITCH = CMP_STRIDE + 8


def _transpose_pages(get_page, pages, xs_ref):
    cpp = PAGE_SIZE // CMP_STRIDE
    for p in pages:
        xt = get_page(p).T
        for c in range(cpp):
            r0 = (p * cpp + c) * CHUNK_PITCH
            xs_ref[0, r0:r0 + CMP_STRIDE, :] = xt[c * CMP_STRIDE:(c + 1) * CMP_STRIDE, :LANES]
            xs_ref[1, r0:r0 + CMP_STRIDE, :] = xt[c * CMP_STRIDE:(c + 1) * CMP_STRIDE, LANES:]


def _compress(n_chunks, xs_ref, hs_ref, wbd_ref, pe_ref, w1_ref, w2_ref, between=None):
    hs_ref[n_chunks:, :] = jnp.zeros((8, CMP_HID), F32)
    parts = []
    for kv in range(2):
        lhs = jnp.concatenate([xs_ref[kv, pl.ds(s, n_chunks, stride=CHUNK_PITCH), :].astype(BF16) for s in range(CMP_STRIDE)], axis=1)
        if between is not None:
            between(kv)
        h = jnp.dot(lhs, wbd_ref[kv], preferred_element_type=F32)
        pe_h = jnp.dot(pe_ref[kv].astype(BF16), w1_ref[kv], preferred_element_type=F32)[0:1]
        for g in range(NSA_KV_HEADS):
            h0 = h[:, g * 2 * CMP_HID:g * 2 * CMP_HID + CMP_HID]
            hs_ref[0:n_chunks, :] = h[:, g * 2 * CMP_HID + CMP_HID:(g + 1) * 2 * CMP_HID]
            hh = h0 + hs_ref[pl.ds(1, n_chunks), :] + pe_h
            parts.append(jnp.dot(jax.nn.gelu(hh).astype(BF16), w2_ref[kv], preferred_element_type=F32))
    return jnp.concatenate(parts, axis=1)


def _masked_softmax(s, allow):
    s = jnp.where(allow, s, -1e30)
    e = jnp.exp(s - jnp.max(s, axis=-1, keepdims=True))
    p = e / jnp.sum(e, axis=-1, keepdims=True)
    return jnp.where(allow, p, 0.0)


def _cmp_probs(qg, kc, tpos, n_cmp):
    n_chunks = kc.shape[0]
    s = _dot_nt(qg, kc)
    i = lax.broadcasted_iota(jnp.int32, (1, n_chunks), 1)
    allow = jnp.where(i < n_cmp, i * CMP_STRIDE + CMP_BLOCK - 1, jnp.int32(2 ** 30)) <= tpos
    return _masked_softmax(s, allow)


def _select_blocks(psum, tpos, n_cmp, n_sel, ns_pad):
    n_chunks = psum.shape[1]
    ci = lax.broadcasted_iota(jnp.int32, (n_chunks, ns_pad), 0)
    sj = lax.broadcasted_iota(jnp.int32, (n_chunks, ns_pad), 1)
    hit = (ci * CMP_STRIDE < (sj + 1) * SEL_BLOCK) & (ci * CMP_STRIDE + CMP_BLOCK > sj * SEL_BLOCK) & (ci < n_cmp) & (sj < n_sel)
    imp = _dot(psum, jnp.where(hit, 1.0, 0.0))
    jj = lax.broadcasted_iota(jnp.int32, (1, ns_pad), 1)
    imp = jnp.where((jj * SEL_BLOCK <= tpos) & (jj < n_sel), imp, -jnp.inf)
    imp = jnp.where((jj == 0) | (jj == tpos // SEL_BLOCK), FORCE_SCORE, imp)
    cnt = jnp.zeros(imp.shape, F32)
    for j in range(n_sel):
        col = imp[:, j:j + 1]
        later = jnp.where(jj > j, 1.0, 0.0)
        cnt = cnt + jnp.where(col > imp, 1.0, jnp.where(col == imp, later, 0.0))
    return jnp.where(cnt < min(SEL_TOP, n_sel), 1.0, 0.0)


def _head_rows(q, g):
    return jnp.concatenate([q[:, (g * NSA_GROUP + j) * NSA_HD:(g * NSA_GROUP + j + 1) * NSA_HD] for j in range(NSA_GROUP)], axis=0)


def _gate_merge(gsig, o_cmp, o_sel, o_win, tq):
    cols = []
    for h in range(NSA_HEADS):
        g, j = divmod(h, NSA_GROUP)
        rows = slice(j * tq, (j + 1) * tq)
        cols.append(gsig[:, 3 * h:3 * h + 1] * o_cmp[g][rows] + gsig[:, 3 * h + 1:3 * h + 2] * o_sel[g][rows]
                    + gsig[:, 3 * h + 2:3 * h + 3] * o_win[g][rows])
    return jnp.concatenate(cols, axis=1)


def _pcompress_kernel(kc_ref, wbd_ref, pe_ref, w1_ref, w2_ref, o_ref, xs_ref, hs_ref):
    n_pages = kc_ref.shape[2] // PAGE_SIZE
    _transpose_pages(lambda p: kc_ref[0, :, p * PAGE_SIZE:(p + 1) * PAGE_SIZE], range(n_pages), xs_ref)
    o_ref[0] = _compress(n_pages * PAGE_SIZE // CMP_STRIDE, xs_ref, hs_ref, wbd_ref, pe_ref, w1_ref, w2_ref)


def _pcompress(kct, wts):
    B, _, T = kct.shape
    n_chunks = T // CMP_STRIDE
    return pl.pallas_call(
        _pcompress_kernel, grid=(B,),
        in_specs=[pl.BlockSpec((1, KV_ROW, T), lambda b: (b, 0, 0)), _const_spec(wts["wbd"].shape), _const_spec(wts["pe"].shape),
                  _const_spec(wts["w1"].shape), _const_spec(wts["w2"].shape)],
        out_specs=pl.BlockSpec((1, n_chunks, KV_ROW), lambda b: (b, 0, 0)),
        out_shape=jax.ShapeDtypeStruct((B, n_chunks, KV_ROW), F32),
        scratch_shapes=[pltpu.VMEM((2, n_chunks * CHUNK_PITCH, LANES), F32), pltpu.VMEM((n_chunks + 8, CMP_HID), F32)],
        compiler_params=_cparams(("parallel",)), name="prompt_compress",
    )(kct, wts["wbd"], wts["pe"], wts["w1"], wts["w2"])


def _softmax_pv(q, kt, vt, bias):
    s = _dot(q, kt) + bias
    e = jnp.exp(s - jnp.max(s, axis=-1, keepdims=True))
    return _dot_nt(e, vt) / jnp.sum(e, axis=-1, keepdims=True)


def _pattn_kernel(q_ref, ckv_ref, ks_ref, kw_ref, small_ref, o_ref, osel_ref, *, T, tq, n_ext):
    n_chunks = T // CMP_STRIDE
    n_cmp = n_chunks - CMP_BLOCK // CMP_STRIDE + 1
    n_sel = -(-T // SEL_BLOCK)
    ns_pad = -(-n_sel // LANES) * LANES
    span = WINDOW + tq
    qi = pl.program_id(1)
    q0 = qi * tq
    q = q_ref[0]
    ckv = ckv_ref[0]
    tcol = q0 + lax.broadcasted_iota(jnp.int32, (tq, 1), 0)
    wstart = pl.multiple_of(jnp.maximum(q0 - WINDOW, 0), LANES)
    keyw = wstart + lax.broadcasted_iota(jnp.int32, (1, span), 1)
    bias_w = jnp.where(keyw <= tcol, jnp.where(keyw > tcol - WINDOW, 0.0, -1e30), -1e30)
    gsig = jax.nn.sigmoid(small_ref[0][:, G_NSA_OFF:G_NSA_OFF + 3 * NSA_HEADS])
    tiles_per_ext = T // tq // n_ext
    o_cmp, o_win = [], []
    for g in range(NSA_KV_HEADS):
        ksl = slice(g * NSA_HD, (g + 1) * NSA_HD)
        vsl = slice(KV_ROW // 2 + g * NSA_HD, KV_ROW // 2 + (g + 1) * NSA_HD)
        qs = [q[:, (g * NSA_GROUP + j) * NSA_HD:(g * NSA_GROUP + j + 1) * NSA_HD] for j in range(NSA_GROUP)]
        psum = None
        for j in range(NSA_GROUP):
            p = _cmp_probs(qs[j], ckv[:, ksl], tcol, n_cmp)
            o_cmp.append(_dot(p, ckv[:, vsl]))
            psum = p if psum is None else psum + p
        sel = _select_blocks(psum, tcol, n_cmp, n_sel, ns_pad).astype(BF16)
        kwg = kw_ref[0, ksl, pl.ds(wstart, span)]
        vwg = kw_ref[0, vsl, pl.ds(wstart, span)]
        for j in range(NSA_GROUP):
            o_win.append(_softmax_pv(qs[j], kwg, vwg, bias_w))
        for v in range(n_ext):
            kext = (v + 1) * (T // n_ext)

            @pl.when((qi >= v * tiles_per_ext) & (qi < (v + 1) * tiles_per_ext))
            def _():
                es = lax.broadcasted_iota(jnp.int32, (ns_pad, kext), 0)
                ek = lax.broadcasted_iota(jnp.int32, (ns_pad, kext), 1)
                expand = jnp.where(ek // SEL_BLOCK == es, 1.0, 0.0).astype(BF16)
                keysel = jnp.dot(sel, expand, preferred_element_type=F32)
                keyi = lax.broadcasted_iota(jnp.int32, (1, kext), 1)
                bias = jnp.where(keyi <= tcol, jnp.where(keysel > 0.5, 0.0, -1e30), -1e30)
                for j in range(NSA_GROUP):
                    h = g * NSA_GROUP + j
                    osel_ref[:, h * NSA_HD:(h + 1) * NSA_HD] = _softmax_pv(qs[j], ks_ref[0, ksl, 0:kext], ks_ref[0, vsl, 0:kext], bias)
    cols = []
    for h in range(NSA_HEADS):
        cols.append(gsig[:, 3 * h:3 * h + 1] * o_cmp[h] + gsig[:, 3 * h + 1:3 * h + 2] * osel_ref[:, h * NSA_HD:(h + 1) * NSA_HD]
                    + gsig[:, 3 * h + 2:3 * h + 3] * o_win[h])
    o_ref[0] = jnp.concatenate(cols, axis=1)


def _pattn(q, ckv, kst, kwt, small, tq=128, n_ext=4):
    B, T, _ = q.shape
    n_chunks = T // CMP_STRIDE
    assert T % (n_ext * tq) == 0
    full = lambda b, t: (b, 0, 0)
    return pl.pallas_call(
        functools.partial(_pattn_kernel, T=T, tq=tq, n_ext=n_ext), grid=(B, T // tq),
        in_specs=[pl.BlockSpec((1, tq, Q_NSA), lambda b, t: (b, t, 0)), pl.BlockSpec((1, n_chunks, KV_ROW), full),
                  pl.BlockSpec((1, KV_ROW, T), full), pl.BlockSpec((1, KV_ROW, T), full),
                  pl.BlockSpec((1, tq, SMALL_W), lambda b, t: (b, t, 0))],
        out_specs=pl.BlockSpec((1, tq, Q_NSA), lambda b, t: (b, t, 0)),
        out_shape=jax.ShapeDtypeStruct((B, T, Q_NSA), F32),
        scratch_shapes=[pltpu.VMEM((tq, Q_NSA), F32)],
        compiler_params=_cparams(("parallel", "parallel")), name="prompt_nsa_attn",
    )(q, ckv, kst, kwt, small)


KEY_TILE = 256


def _select_blocks_t(psum_t, tl, n_cmp, n_sel, ns_rows):
    n_chunks = psum_t.shape[0]
    sj = lax.broadcasted_iota(jnp.int32, (ns_rows, n_chunks), 0)
    ci = lax.broadcasted_iota(jnp.int32, (ns_rows, n_chunks), 1)
    hit = (ci * CMP_STRIDE < (sj + 1) * SEL_BLOCK) & (ci * CMP_STRIDE + CMP_BLOCK > sj * SEL_BLOCK) & (ci < n_cmp) & (sj < n_sel)
    imp = _dot(jnp.where(hit, 1.0, 0.0), psum_t)
    jj = lax.broadcasted_iota(jnp.int32, (ns_rows, 1), 0)
    imp = jnp.where((jj * SEL_BLOCK <= tl) & (jj < n_sel), imp, -jnp.inf)
    imp = jnp.where((jj == 0) | (jj == tl // SEL_BLOCK), FORCE_SCORE, imp)
    cnt = jnp.zeros(imp.shape, F32)
    for j in range(n_sel):
        row = imp[j:j + 1, :]
        later = jnp.where(jj > j, 1.0, 0.0)
        cnt = cnt + jnp.where(row > imp, 1.0, jnp.where(row == imp, later, 0.0))
    return jnp.where(cnt < min(SEL_TOP, n_sel), 1.0, 0.0)


V_AUG = NSA_HD + 8


def _col_attend(n_tiles, tile_w, k_tile, q_list, v_aug, bias_tile, s_ref, e_ref):
    outs = []
    n_keys = n_tiles * tile_w
    n_heads = len(q_list)
    m_prev = None
    for st in range(n_heads + 1):
        m8 = None
        for i in range(n_tiles):
            rows = slice(i * tile_w, (i + 1) * tile_w)
            if st < n_heads:
                s = jnp.dot(k_tile(i), q_list[st], preferred_element_type=F32)
                bt = bias_tile(i)
                if bt is not None:
                    s = s + bt
                s_ref[st, rows, :] = s
                for r in range(tile_w // 8):
                    m8 = s[8 * r:8 * r + 8] if m8 is None else jnp.maximum(m8, s[8 * r:8 * r + 8])
            if st >= 1:
                e_ref[st - 1, rows, :] = jnp.exp2(s_ref[st - 1, rows, :] - m_prev).astype(BF16)
        if st >= 1:
            a = jnp.dot(v_aug, e_ref[st - 1, 0:n_keys, :], preferred_element_type=F32)
            outs.append(a[:NSA_HD] / a[NSA_HD:NSA_HD + 1])
        if st < n_heads:
            m_prev = jnp.max(m8, axis=0, keepdims=True)
    return outs


def _ones_rows(v_t):
    n = v_t.shape[1]
    r = lax.broadcasted_iota(jnp.int32, (V_AUG - NSA_HD, n), 0)
    return jnp.concatenate([v_t, jnp.where(r == 0, 1.0, 0.0)], axis=0).astype(BF16)


def _pattn_t_kernel(qt_ref, gt_ref, ckv_ref, kstok0_ref, kstok1_ref, vst_ref, kwtok_ref, vwt_ref, o_ref, osel_ref, s_ref, e_ref,
                    *, T, tq, n_ext):
    kstok_refs = (kstok0_ref, kstok1_ref)
    n_chunks = T // CMP_STRIDE
    n_cmp = n_chunks - CMP_BLOCK // CMP_STRIDE + 1
    n_sel = -(-T // SEL_BLOCK)
    ns_rows = -(-n_sel // 8) * 8
    span = WINDOW + tq
    qi = pl.program_id(1)
    q0 = qi * tq
    qt = qt_ref[0]
    gt = gt_ref[0]
    ckv = ckv_ref[0]
    tl = q0 + lax.broadcasted_iota(jnp.int32, (1, tq), 1)
    wstart = pl.multiple_of(jnp.maximum(q0 - WINDOW, 0), LANES)
    ci = lax.broadcasted_iota(jnp.int32, (n_chunks, 1), 0)
    bias_c = jnp.where(jnp.where(ci < n_cmp, ci * CMP_STRIDE + CMP_BLOCK - 1, jnp.int32(2 ** 30)) <= tl, 0.0, -1e30)
    zeros_half = jnp.zeros((NSA_HD, tq), BF16)
    tiles_per_ext = T // tq // n_ext
    kw = wstart + lax.broadcasted_iota(jnp.int32, (span, 1), 0)
    bias_w = jnp.where(kw <= tl, jnp.where(kw > tl - WINDOW, 0.0, -1e30), -1e30)
    o_cmp, o_win, sel_bias, q_hs = [], [], [], []
    for g in range(NSA_KV_HEADS):
        ksl = slice(g * NSA_HD, (g + 1) * NSA_HD)
        vsl = slice(KV_ROW // 2 + g * NSA_HD, KV_ROW // 2 + (g + 1) * NSA_HD)
        q_h = [qt[(g * NSA_GROUP + j) * NSA_HD:(g * NSA_GROUP + j + 1) * NSA_HD, :] for j in range(NSA_GROUP)]
        q_pad = [jnp.concatenate([qh, zeros_half] if g == 0 else [zeros_half, qh], axis=0) for qh in q_h]
        q_hs.append(q_h)
        kc = ckv[:, ksl].astype(BF16)
        vc_t = ckv[:, vsl].T.astype(BF16)
        psum = None
        for j in range(NSA_GROUP):
            s = jnp.dot(kc, q_h[j], preferred_element_type=F32) + bias_c
            e = jnp.exp2(s - jnp.max(s, axis=0, keepdims=True))
            p = jnp.where(bias_c == 0.0, e / jnp.sum(e, axis=0, keepdims=True), 0.0)
            o_cmp.append(jnp.dot(vc_t, p.astype(BF16), preferred_element_type=F32))
            psum = p if psum is None else psum + p
        sel = _select_blocks_t(psum, tl, n_cmp, n_sel, ns_rows)
        sel_bias.append(jnp.concatenate([(sel - 1.0) * 1e30, jnp.zeros((LANES - NSA_HD - ns_rows, tq), F32)], axis=0).astype(BF16))
        vw_aug = _ones_rows(vwt_ref[0, ksl, pl.ds(wstart, span)])
        o_win += _col_attend(
            span // LANES, LANES, lambda i: kwtok_ref[0, pl.ds(wstart + i * LANES, LANES), :], q_pad, vw_aug,
            lambda i: bias_w[i * LANES:(i + 1) * LANES, :], s_ref, e_ref)

    ext_w = T // n_ext
    for v in range(n_ext):
        kext = (v + 1) * ext_w
        n_tiles = kext // KEY_TILE
        first_diag = n_tiles - ext_w // KEY_TILE

        @pl.when((qi >= v * tiles_per_ext) & (qi < (v + 1) * tiles_per_ext))
        def _():
            keyd = (kext - ext_w) + lax.broadcasted_iota(jnp.int32, (ext_w, 1), 0)
            causal = jnp.where(keyd <= tl, 0.0, -1e30)
            for g in range(NSA_KV_HEADS):
                ksl = slice(g * NSA_HD, (g + 1) * NSA_HD)
                vs_aug = _ones_rows(vst_ref[0, ksl, 0:kext])
                q_aug = [jnp.concatenate([qh, sel_bias[g]], axis=0) for qh in q_hs[g]]
                outs = _col_attend(
                    n_tiles, KEY_TILE, lambda i: kstok_refs[g][0, i * KEY_TILE:(i + 1) * KEY_TILE, :], q_aug, vs_aug,
                    lambda i: causal[(i - first_diag) * KEY_TILE:(i - first_diag + 1) * KEY_TILE, :] if i >= first_diag else None,
                    s_ref, e_ref)
                for j in range(NSA_GROUP):
                    h = g * NSA_GROUP + j
                    osel_ref[h * NSA_HD:(h + 1) * NSA_HD, :] = outs[j]

    heads = [gt[3 * h:3 * h + 1, :] * o_cmp[h] + gt[3 * h + 1:3 * h + 2, :] * osel_ref[h * NSA_HD:(h + 1) * NSA_HD, :]
             + gt[3 * h + 2:3 * h + 3, :] * o_win[h] for h in range(NSA_HEADS)]
    o_ref[0] = jnp.concatenate(heads, axis=0).T


def _pattn_t(p, ckv, tq=256, n_ext=8):
    B, _, T = p["qt"].shape
    n_chunks = T // CMP_STRIDE
    assert T % (n_ext * tq) == 0 and (T // n_ext) % KEY_TILE == 0 and -(-T // SEL_BLOCK) <= LANES - NSA_HD
    full = lambda b, t: (b, 0, 0)
    vhalf = pl.BlockSpec((1, KV_ROW // 2, T), lambda b, t: (b, 1, 0))
    return pl.pallas_call(
        functools.partial(_pattn_t_kernel, T=T, tq=tq, n_ext=n_ext), grid=(B, T // tq),
        in_specs=[pl.BlockSpec((1, Q_NSA, tq), lambda b, t: (b, 0, t)), pl.BlockSpec((1, G_ROWS, tq), lambda b, t: (b, 0, t)),
                  pl.BlockSpec((1, n_chunks, KV_ROW), full), pl.BlockSpec((1, T, LANES), full), pl.BlockSpec((1, T, LANES), full), vhalf,
                  pl.BlockSpec((1, T, LANES), full), vhalf],
        out_specs=pl.BlockSpec((1, tq, Q_NSA), lambda b, t: (b, t, 0)),
        out_shape=jax.ShapeDtypeStruct((B, T, Q_NSA), F32),
        scratch_shapes=[pltpu.VMEM((Q_NSA, tq), F32), pltpu.VMEM((NSA_GROUP, T, tq), F32), pltpu.VMEM((NSA_GROUP, T, tq), BF16)],
        compiler_params=_cparams(("parallel", "parallel")), name="prompt_nsa_attn",
    )(p["qt"], p["gt"], ckv, p["ks_tok0"], p["ks_tok1"], p["kst"], p["kw_tok"], p["kwt"])


def _s1_kernel(pt_ref, pool_ref, q_ref, wbd_ref, pe_ref, w1_ref, w2_ref, ocmp_ref, idx_ref, pg_ref, sem_ref, xs_ref, hs_ref,
               *, n_pages, q_pos):
    n_chunks = n_pages * PAGE_SIZE // CMP_STRIDE
    n_cmp = n_chunks - CMP_BLOCK // CMP_STRIDE + 1
    n_sel = -(-(q_pos + 1) // SEL_BLOCK)
    ns_pad = -(-n_sel // LANES) * LANES
    b = pl.program_id(0)
    slot = b % 2

    def page_copy(seq, sl, j):
        return pltpu.make_async_copy(pool_ref.at[pt_ref[seq, j]], pg_ref.at[sl, j], sem_ref.at[sl])

    @pl.when(b == 0)
    def _():
        for j in range(n_pages):
            page_copy(0, 0, j).start()

    @pl.when(b + 1 < pl.num_programs(0))
    def _():
        for j in range(n_pages):
            page_copy(b + 1, 1 - slot, j).start()

    for j in range(n_pages):
        page_copy(b, slot, j).wait()
    _transpose_pages(lambda p: pg_ref[slot, p], range(n_pages), xs_ref)
    ckv = _compress(n_chunks, xs_ref, hs_ref, wbd_ref, pe_ref, w1_ref, w2_ref)
    qrow = q_ref[0].astype(F32)
    q8 = jnp.concatenate([qrow[:, h * NSA_HD:(h + 1) * NSA_HD] for h in range(NSA_HEADS)], axis=0)
    row = lax.broadcasted_iota(jnp.int32, (NSA_HEADS, 1), 0)
    tpos = jnp.full((NSA_HEADS, 1), q_pos, jnp.int32)
    o_all = jnp.zeros((NSA_HEADS, NSA_HD), F32)
    psum = jnp.zeros((NSA_HEADS, n_chunks), F32)
    for g in range(NSA_KV_HEADS):
        kc = ckv[:, g * NSA_HD:(g + 1) * NSA_HD]
        vc = ckv[:, KV_ROW // 2 + g * NSA_HD:KV_ROW // 2 + (g + 1) * NSA_HD]
        p = _cmp_probs(q8, kc, tpos, n_cmp)
        mine = (row // NSA_GROUP) == g
        o_all = jnp.where(mine, _dot(p, vc), o_all)
        pg = jnp.sum(jnp.where(mine, p, 0.0), axis=0, keepdims=True)
        psum = jnp.where(row == g, pg, psum)
    sel = _select_blocks(psum, tpos, n_cmp, n_sel, ns_pad)
    a = lax.broadcasted_iota(jnp.int32, (ns_pad, ns_pad), 0)
    b = lax.broadcasted_iota(jnp.int32, (ns_pad, ns_pad), 1)
    before = jnp.dot(sel.astype(BF16), jnp.where(a < b, 1.0, 0.0).astype(BF16), preferred_element_type=F32)
    jj = lax.broadcasted_iota(jnp.int32, (1, ns_pad), 1).astype(F32)
    lane = lax.broadcasted_iota(jnp.int32, (1, LANES), 1)
    idx = jnp.zeros((NSA_HEADS, LANES), F32)
    for k in range(min(SEL_TOP, n_sel)):
        ik = jnp.sum(jnp.where((sel > 0.5) & (before == k), jj, 0.0), axis=1, keepdims=True)
        idx = jnp.where(lane == k, ik, idx)
    idx_ref[0] = idx.astype(jnp.int32)
    ocmp_ref[0] = jnp.concatenate([o_all, jnp.zeros((NSA_HEADS, LANES - NSA_HD), F32)], axis=1)


def _s1(pool_t, page_table, q3, wts, q_pos):
    B, n_pages = page_table.shape
    n_chunks = n_pages * PAGE_SIZE // CMP_STRIDE

    per_b = lambda b, pt: (b, 0, 0)
    cst = lambda shape: pl.BlockSpec(shape, lambda b, pt: (0,) * len(shape), pipeline_mode=pl.Buffered(1))
    grid_spec = pltpu.PrefetchScalarGridSpec(
        num_scalar_prefetch=1, grid=(B,),
        in_specs=[pl.BlockSpec(memory_space=pl.ANY), pl.BlockSpec((1, 1, Q_NSA), per_b), cst(wts["wbd"].shape), cst(wts["pe"].shape),
                  cst(wts["w1"].shape), cst(wts["w2"].shape)],
        out_specs=[pl.BlockSpec((1, NSA_HEADS, LANES), per_b), pl.BlockSpec((1, NSA_HEADS, LANES), per_b)],
        scratch_shapes=[pltpu.VMEM((2, n_pages, KV_ROW, PAGE_SIZE), F32), pltpu.SemaphoreType.DMA((2,)),
                        pltpu.VMEM((2, n_chunks * CHUNK_PITCH, LANES), F32), pltpu.VMEM((n_chunks + 8, CMP_HID), F32)],
    )
    return pl.pallas_call(
        functools.partial(_s1_kernel, n_pages=n_pages, q_pos=q_pos), grid_spec=grid_spec,
        out_shape=[jax.ShapeDtypeStruct((B, NSA_HEADS, LANES), F32), jax.ShapeDtypeStruct((B, NSA_HEADS, LANES), jnp.int32)],
        compiler_params=_cparams(("arbitrary",)), name="sample_compress_select",
    )(page_table, pool_t, q3, wts["wbd"], wts["pe"], wts["w1"], wts["w2"])


def _s2_kernel(phys_ref, meta_ref, pool_ref, q_ref, ocmp_ref, small_ref, nsel_ref, nwin_ref, nwint_ref, cwin_ref, o_ref, swin_ref,
               pg_ref, sem_ref, *, n_top, win_len):
    n_blk = NSA_KV_HEADS * n_top
    b = pl.program_id(0)
    slot = b % 2

    def slab_copy(seq, sl, j, is_value):
        row0 = (NSA_KV_HEADS * is_value + j // n_top) * NSA_HD
        return pltpu.make_async_copy(pool_ref.at[phys_ref[seq, j], pl.ds(row0, NSA_HD)], pg_ref.at[sl, 2 * j + is_value], sem_ref.at[sl])

    def all_copies(seq, sl):
        return [slab_copy(seq, sl, j, v) for j in range(n_blk) for v in (0, 1)]

    @pl.when(b == 0)
    def _():
        for c in all_copies(0, 0):
            c.start()

    @pl.when(b + 1 < pl.num_programs(0))
    def _():
        for c in all_copies(b + 1, 1 - slot):
            c.start()

    for c in all_copies(b, slot):
        c.wait()
    qrow = q_ref[0].astype(F32)
    q8 = jnp.concatenate([qrow[:, h * NSA_HD:(h + 1) * NSA_HD] for h in range(NSA_HEADS)], axis=0)
    row = lax.broadcasted_iota(jnp.int32, (NSA_HEADS, 1), 0)
    lane = lax.broadcasted_iota(jnp.int32, (1, PAGE_SIZE), 1)
    nsel = nsel_ref[0]
    nwin = nwin_ref[0]
    cwin = cwin_ref[0]
    r = lax.broadcasted_iota(jnp.int32, (1, win_len), 1)
    allow_w = (r > win_len - WINDOW) & (r <= win_len)
    o_sel = jnp.zeros((NSA_HEADS, NSA_HD), F32)
    o_win = jnp.zeros((NSA_HEADS, NSA_HD), F32)
    for g in range(NSA_KV_HEADS):
        mine = (row // NSA_GROUP) == g
        ksl = slice(g * NSA_HD, (g + 1) * NSA_HD)
        vsl = slice(KV_ROW // 2 + g * NSA_HD, KV_ROW // 2 + (g + 1) * NSA_HD)
        kts, vts, masks = [], [], []
        has_new = jnp.zeros((1, 1), F32)
        for k in range(n_top):
            m = meta_ref[b, g * n_top + k]
            kts.append(pg_ref[slot, 2 * (g * n_top + k)])
            vts.append(pg_ref[slot, 2 * (g * n_top + k) + 1])
            masks.append((lane // SEL_BLOCK) == m)
            has_new = has_new + jnp.where(m == 2, 1.0, 0.0)
        kt = jnp.concatenate(kts, axis=1)
        vt = jnp.concatenate(vts, axis=1)
        allow = jnp.concatenate(masks, axis=1)
        s = jnp.where(allow, _dot(q8, kt), -1e30)
        s_new = jnp.where(has_new > 0.5, jnp.sum(q8 * nsel[:, ksl], axis=1, keepdims=True), -1e30)
        mx = jnp.maximum(jnp.max(s, axis=1, keepdims=True), s_new)
        e = jnp.where(allow, jnp.exp(s - mx), 0.0)
        e_new = jnp.where(has_new > 0.5, jnp.exp(s_new - mx), 0.0)
        den = jnp.sum(e, axis=1, keepdims=True) + e_new
        og = (_dot_nt(e, vt) + e_new * nsel[:, vsl]) / den
        o_sel = jnp.where(mine, og, o_sel)
        s = jnp.where(allow_w, _dot(q8, cwin[ksl, :]), -1e30)
        s_new = jnp.sum(q8 * nwin[:, ksl], axis=1, keepdims=True)
        mx = jnp.maximum(jnp.max(s, axis=1, keepdims=True), s_new)
        e = jnp.where(allow_w, jnp.exp(s - mx), 0.0)
        e_new = jnp.exp(s_new - mx)
        den = jnp.sum(e, axis=1, keepdims=True) + e_new
        og = (_dot_nt(e, cwin[vsl, :]) + e_new * nwin[:, vsl]) / den
        o_win = jnp.where(mine, og, o_win)
    o_cmp = ocmp_ref[0][:, :NSA_HD]
    gsig = jax.nn.sigmoid(small_ref[0][:, G_NSA_OFF:G_NSA_OFF + 3 * NSA_HEADS])
    cols = []
    for h in range(NSA_HEADS):
        cols.append(gsig[:, 3 * h:3 * h + 1] * o_cmp[h:h + 1] + gsig[:, 3 * h + 1:3 * h + 2] * o_sel[h:h + 1]
                    + gsig[:, 3 * h + 2:3 * h + 3] * o_win[h:h + 1])
    o_ref[0] = jnp.concatenate(cols, axis=1)
    blane = lax.broadcasted_iota(jnp.int32, (1, nwint_ref.shape[1]), 1)
    newcol = jnp.sum(jnp.where(blane == b, nwint_ref[...], 0.0), axis=1, keepdims=True)
    swin_ref[0] = jnp.concatenate([cwin[:, 1:], newcol], axis=1)


def _s2(pool_t, phys, meta, q3, ocmp, small3, nsel3, nwin3, nwint, cwint, n_top):
    B = q3.shape[0]
    win_len = cwint.shape[2]
    n_blk = NSA_KV_HEADS * n_top

    per_b = lambda shape: pl.BlockSpec(shape, lambda b, ph, me: (b,) + (0,) * (len(shape) - 1))
    grid_spec = pltpu.PrefetchScalarGridSpec(
        num_scalar_prefetch=2, grid=(B,),
        in_specs=[pl.BlockSpec(memory_space=pl.ANY), per_b((1, 1, Q_NSA)), per_b((1, NSA_HEADS, LANES)), per_b((1, 1, SMALL_W)),
                  per_b((1, 1, KV_ROW)), per_b((1, 1, KV_ROW)), pl.BlockSpec(nwint.shape, lambda b, ph, me: (0, 0)),
                  per_b((1, KV_ROW, win_len))],
        out_specs=[per_b((1, 1, Q_NSA)), per_b((1, KV_ROW, win_len))],
        scratch_shapes=[pltpu.VMEM((2, 2 * n_blk, NSA_HD, PAGE_SIZE), F32), pltpu.SemaphoreType.DMA((2,))],
    )
    return pl.pallas_call(
        functools.partial(_s2_kernel, n_top=n_top, win_len=win_len), grid_spec=grid_spec,
        out_shape=[jax.ShapeDtypeStruct((B, 1, Q_NSA), F32), jax.ShapeDtypeStruct((B, KV_ROW, win_len), F32)],
        compiler_params=_cparams(("arbitrary",)), name="sample_sel_win_attn",
    )(phys, meta, pool_t, q3, ocmp, small3, nsel3, nwin3, nwint, cwint)


def _gdn_gates(b_raw, a_raw, alog, dtb):
    beta = jax.nn.sigmoid(b_raw)
    g = -jnp.exp(alog) * jax.nn.softplus(a_raw + dtb)
    return beta, g


def _l2n(x):
    return x * lax.rsqrt(jnp.sum(x * x, axis=-1, keepdims=True) + EPS)


def _gdn_prompt_kernel(qkv_ref, z_ref, small_ref, smallt_ref, cw_ref, alog_ref, dtb_ref, alogt_ref, dtbt_ref, on_ref,
                       o_ref, sfin_ref, conv_ref, s_ref, xx_ref):
    ci = pl.program_id(1)
    tc = qkv_ref.shape[1]
    C = GDN_CHUNK

    @pl.when(ci == 0)
    def _():
        s_ref[...] = jnp.zeros(s_ref.shape, F32)
        xx_ref[0:8, :] = jnp.zeros((8, CONV_DIM), F32)

    xx_ref[8:8 + tc, :] = qkv_ref[0]
    y = xx_ref[pl.ds(8 - (CONV_W - 1), tc), :] * cw_ref[0:1, :]
    for j in range(1, CONV_W):
        y = y + xx_ref[pl.ds(8 - (CONV_W - 1) + j, tc), :] * cw_ref[j:j + 1, :]
    c = jax.nn.silu(y)
    tail = xx_ref[tc:tc + 8, :]
    conv_ref[0] = tail
    xx_ref[0:8, :] = tail
    small = small_ref[0]
    beta, gcol = _gdn_gates(small[:, B_OFF:B_OFF + GDN_HEADS], small[:, A_OFF:A_OFF + GDN_HEADS], alog_ref[...], dtb_ref[...])
    _, grow = _gdn_gates(smallt_ref[0][0:GDN_HEADS], smallt_ref[0][GDN_HEADS:2 * GDN_HEADS], alogt_ref[...], dtbt_ref[...])
    z = z_ref[0]
    ii = lax.broadcasted_iota(jnp.int32, (C, C), 0)
    jj = lax.broadcasted_iota(jnp.int32, (C, C), 1)
    tril = ii >= jj
    strict = ii > jj
    eye = jnp.where(ii == jj, 1.0, 0.0)
    hcs = [(cc, h) for cc in range(tc // C) for h in range(GDN_HEADS)]
    loc = {}
    for cc, h in hcs:
        rs = slice(cc * C, (cc + 1) * C)
        qh = _l2n(c[rs, h * GDN_DK:(h + 1) * GDN_DK]) * (GDN_DK ** -0.5)
        kh = _l2n(c[rs, GDN_QK + h * GDN_DK:GDN_QK + (h + 1) * GDN_DK])
        vh = c[rs, 2 * GDN_QK + h * GDN_DV:2 * GDN_QK + (h + 1) * GDN_DV]
        bcol = beta[rs, h:h + 1]
        g_c = gcol[rs, h:h + 1]
        g_r = grow[h:h + 1, rs]
        dec_c = jnp.sum(jnp.where(tril, g_r, 0.0), axis=1, keepdims=True)
        dec_r = jnp.sum(jnp.where(ii <= jj, g_c, 0.0), axis=0, keepdims=True)
        lmask = jnp.where(tril, jnp.exp(jnp.where(tril, dec_c - dec_r, 0.0)), 0.0)
        kb = kh * bcol
        edec = jnp.exp(dec_c)
        dlast = dec_c[C - 1:C, :]
        loc[cc, h] = dict(
            m=-jnp.where(strict, _dot_nt(kb, kh) * lmask, 0.0), rhs=jnp.concatenate([vh * bcol, kb * edec], axis=1),
            attn=_dot_nt(qh, kh) * lmask, qe=qh * edec, kdt=(kh * jnp.exp(dlast - dec_c)).T, elast=jnp.exp(dlast))
    tinv = {k: eye + loc[k]["m"] for k in hcs}
    mpow = {k: _dot3(loc[k]["m"], loc[k]["m"]) for k in hcs}
    n_steps = (C - 1).bit_length() - 1
    for step in range(n_steps):
        for k in hcs:
            if step < n_steps - 1:
                r = _dot3(mpow[k], jnp.concatenate([mpow[k], tinv[k]], axis=1))
                mpow[k], tinv[k] = r[:, :C], tinv[k] + r[:, C:]
            else:
                tinv[k] = tinv[k] + _dot3(mpow[k], tinv[k])
    uw = {k: _dot3(tinv[k], loc[k]["rhs"]) for k in hcs}
    for cc in range(tc // C):
        rs = slice(cc * C, (cc + 1) * C)
        for h in range(GDN_HEADS):
            d = loc[cc, h]
            s_old = s_ref[h]
            ws_qs = _dot(jnp.concatenate([uw[cc, h][:, GDN_DV:], d["qe"]], axis=0), s_old)
            v_new = uw[cc, h][:, :GDN_DV] - ws_qs[:C]
            o = ws_qs[C:] + _dot(d["attn"], v_new)
            s_ref[h] = s_old * d["elast"] + _dot(d["kdt"], v_new)
            o = _rms(o) * on_ref[...] * jax.nn.silu(z[rs, h * GDN_DV:(h + 1) * GDN_DV])
            o_ref[0, rs, h * GDN_DV:(h + 1) * GDN_DV] = o

    @pl.when(ci == pl.num_programs(1) - 1)
    def _():
        sfin_ref[0] = s_ref[...]


def _gdn_prompt(qkv, z, small, wts, tc=256):
    B, T, _ = qkv.shape
    smallt = jnp.transpose(small[:, :, B_OFF:B_OFF + 2 * GDN_HEADS], (0, 2, 1))
    tokb = lambda w: pl.BlockSpec((1, tc, w), lambda b, c: (b, c, 0))
    return pl.pallas_call(
        _gdn_prompt_kernel, grid=(B, T // tc),
        in_specs=[tokb(CONV_DIM), tokb(GDN_V), tokb(SMALL_W), pl.BlockSpec((1, 2 * GDN_HEADS, tc), lambda b, c: (b, 0, c)),
                  _const_spec((CONV_W, CONV_DIM)), _const_spec((1, GDN_HEADS)), _const_spec((1, GDN_HEADS)),
                  _const_spec((GDN_HEADS, 1)), _const_spec((GDN_HEADS, 1)), _const_spec((1, GDN_DV))],
        out_specs=[tokb(GDN_V), pl.BlockSpec((1, GDN_HEADS, GDN_DK, GDN_DV), lambda b, c: (b, 0, 0, 0)),
                   pl.BlockSpec((1, 8, CONV_DIM), lambda b, c: (b, 0, 0))],
        out_shape=[jax.ShapeDtypeStruct((B, T, GDN_V), F32), jax.ShapeDtypeStruct((B, GDN_HEADS, GDN_DK, GDN_DV), F32),
                   jax.ShapeDtypeStruct((B, 8, CONV_DIM), F32)],
        scratch_shapes=[pltpu.VMEM((GDN_HEADS, GDN_DK, GDN_DV), F32), pltpu.VMEM((tc + 8, CONV_DIM), F32)],
        compiler_params=_cparams(("parallel", "arbitrary")), name="gdn_prompt",
    )(qkv, z, small, smallt, wts["conv_w"], wts["alog"], wts["dtb"], wts["alog_t"], wts["dtb_t"], wts["onorm"])


def _gdn_sample_kernel(xx_ref, z_ref, small_ref, cw_ref, alog_ref, dtb_ref, on_ref, s_ref,
                       o_ref, sout_ref, qt_ref, kt_ref, wt_ref, u_ref, sc_ref):
    b = pl.program_id(0)
    nb = z_ref.shape[0]

    @pl.when(b == 0)
    def _():
        y = xx_ref[0] * cw_ref[0:1, :]
        for j in range(1, CONV_W):
            y = y + xx_ref[j] * cw_ref[j:j + 1, :]
        c = jax.nn.silu(y)
        small = small_ref[...]
        beta, g = _gdn_gates(small[:, B_OFF:B_OFF + GDN_HEADS], small[:, A_OFF:A_OFF + GDN_HEADS], alog_ref[...], dtb_ref[...])
        a = jnp.exp(g)
        attn = []
        for h in range(GDN_HEADS):
            qh = _l2n(c[:, h * GDN_DK:(h + 1) * GDN_DK]) * (GDN_DK ** -0.5)
            kh = _l2n(c[:, GDN_QK + h * GDN_DK:GDN_QK + (h + 1) * GDN_DK])
            vh = c[:, 2 * GDN_QK + h * GDN_DV:2 * GDN_QK + (h + 1) * GDN_DV]
            bh, ah = beta[:, h:h + 1], a[:, h:h + 1]
            qt_ref[h] = (qh * ah).T
            kt_ref[h] = kh.T
            wt_ref[h] = (kh * bh * ah).T
            u_ref[h] = vh * bh
            attn.append(jnp.sum(qh * kh, axis=1, keepdims=True))
        sc_ref[...] = jnp.concatenate([a] + attn + [jnp.zeros((nb, LANES - 2 * GDN_HEADS), F32)], axis=1)

    lane = lax.broadcasted_iota(jnp.int32, (1, nb), 1)
    pick = lane == b
    sc = sc_ref[pl.ds(b, 1), :]
    zrow = z_ref[pl.ds(b, 1), :]
    for h in range(GDN_HEADS):
        wcol = jnp.sum(jnp.where(pick, wt_ref[h], 0.0), axis=1, keepdims=True)
        qcol = jnp.sum(jnp.where(pick, qt_ref[h], 0.0), axis=1, keepdims=True)
        kcol = jnp.sum(jnp.where(pick, kt_ref[h], 0.0), axis=1, keepdims=True)
        s_old = s_ref[0, h]
        v_new = u_ref[h, pl.ds(b, 1), :] - jnp.sum(s_old * wcol, axis=0, keepdims=True)
        o = jnp.sum(s_old * qcol, axis=0, keepdims=True) + sc[:, GDN_HEADS + h:GDN_HEADS + h + 1] * v_new
        sout_ref[0, h] = s_old * sc[:, h:h + 1] + kcol * v_new
        o = _rms(o) * on_ref[...] * jax.nn.silu(zrow[:, h * GDN_DV:(h + 1) * GDN_DV])
        o_ref[0, :, h * GDN_DV:(h + 1) * GDN_DV] = o


def _gdn_sample(xx4, z2, small2, state, wts):
    nb = z2.shape[0]
    cst = lambda shape: pl.BlockSpec(shape, lambda b: (0,) * len(shape))
    sspec = pl.BlockSpec((1, GDN_HEADS, GDN_DK, GDN_DV), lambda b: (b, 0, 0, 0))
    return pl.pallas_call(
        _gdn_sample_kernel, grid=(nb,),
        in_specs=[cst(xx4.shape), cst(z2.shape), cst(small2.shape), cst((CONV_W, CONV_DIM)), cst((1, GDN_HEADS)),
                  cst((1, GDN_HEADS)), cst((1, GDN_DV)), sspec],
        out_specs=[pl.BlockSpec((1, 1, GDN_V), lambda b: (b, 0, 0)), sspec],
        out_shape=[jax.ShapeDtypeStruct((nb, 1, GDN_V), F32), jax.ShapeDtypeStruct(state.shape, F32)],
        scratch_shapes=[pltpu.VMEM((GDN_HEADS, GDN_DK, nb), F32)] * 3
        + [pltpu.VMEM((GDN_HEADS, nb, GDN_DV), F32), pltpu.VMEM((nb, LANES), F32)],
        compiler_params=_cparams(("arbitrary",)), name="gdn_sample",
    )(xx4, z2, small2, wts["conv_w"], wts["alog"], wts["dtb"], wts["onorm"], state)


def _mem_prompt_kernel(qm_ref, kv_ref, o_ref):
    qm = qm_ref[0]
    kv = kv_ref[0]
    for h in range(MEM_HEADS):
        sl = slice(h * MEM_HD, (h + 1) * MEM_HD)
        s = _dot_nt(qm[:, sl], kv[:, sl]) * MEM_SCALE
        e = jnp.exp(s - jnp.max(s, axis=-1, keepdims=True))
        p = e / jnp.sum(e, axis=-1, keepdims=True)
        o_ref[0, :, sl] = _dot(p, kv[:, MEM_Q + h * MEM_HD:MEM_Q + (h + 1) * MEM_HD])


def _mem_prompt(qm, mkv, tq=512):
    B, T, _ = qm.shape
    M = mkv.shape[1]
    return pl.pallas_call(
        _mem_prompt_kernel, grid=(B, T // tq),
        in_specs=[pl.BlockSpec((1, tq, MEM_Q), lambda b, t: (b, t, 0)), pl.BlockSpec((1, M, 2 * MEM_Q), lambda b, t: (b, 0, 0))],
        out_specs=pl.BlockSpec((1, tq, MEM_Q), lambda b, t: (b, t, 0)),
        out_shape=jax.ShapeDtypeStruct((B, T, MEM_Q), F32),
        compiler_params=_cparams(("parallel", "parallel")), name="mem_attn_prompt",
    )(qm, mkv)


def _mem_sample_kernel(qm_ref, kv_ref, o_ref):
    q = qm_ref[0].astype(F32)
    for h in range(MEM_HEADS):
        sl = slice(h * MEM_HD, (h + 1) * MEM_HD)
        k = kv_ref[0, :, 0, h, :]
        v = kv_ref[0, :, 1, h, :]
        s = jnp.sum(k * q[:, sl], axis=1, keepdims=True) * MEM_SCALE
        e = jnp.exp(s - jnp.max(s, axis=0, keepdims=True))
        p = e / jnp.sum(e, axis=0, keepdims=True)
        o_ref[0, :, sl] = jnp.sum(v * p, axis=0, keepdims=True)


def _mem_sample(qm3, cache):
    B, M = cache.shape[0], cache.shape[1]
    return pl.pallas_call(
        _mem_sample_kernel, grid=(B,),
        in_specs=[pl.BlockSpec((1, 1, MEM_Q), lambda b: (b, 0, 0)),
                  pl.BlockSpec((1, M, 2, MEM_HEADS, MEM_HD), lambda b: (b, 0, 0, 0, 0))],
        out_specs=pl.BlockSpec((1, 1, MEM_Q), lambda b: (b, 0, 0)),
        out_shape=jax.ShapeDtypeStruct((B, 1, MEM_Q), F32),
        compiler_params=_cparams(("parallel",)), name="mem_attn_sample",
    )(qm3, cache)


def _merge_ffn_kernel(on_ref, og_ref, om_ref, gate_ref, x_ref, wbn_ref, wbg_ref, wbm_ref, wout_ref, gffn_ref, w1_ref, w2_ref, y_ref):
    merged = (gate_ref[:, 0:D_MODEL] * _dot(on_ref[...], wbn_ref[...])
              + gate_ref[:, D_MODEL:2 * D_MODEL] * _dot(og_ref[...], wbg_ref[...])
              + gate_ref[:, 2 * D_MODEL:3 * D_MODEL] * _dot(om_ref[...], wbm_ref[...]))
    h = x_ref[...] + _dot(merged, wout_ref[...])
    f = _dot(_rms(h) * gffn_ref[...], w1_ref[...])
    f = jnp.square(jnp.maximum(f, 0.0))
    y_ref[...] = h + _dot(f, w2_ref[...])


def _merge_ffn(o_nsa, o_gdn, o_mem, gates, x2d, wts, tm):
    n = x2d.shape[0]
    tok = lambda w: pl.BlockSpec((tm, w), lambda i: (i, 0))
    return pl.pallas_call(
        _merge_ffn_kernel, grid=(n // tm,),
        in_specs=[tok(Q_NSA), tok(GDN_V), tok(MEM_Q), tok(N_BRANCH * D_MODEL), tok(D_MODEL),
                  _const_spec((Q_NSA, D_MODEL)), _const_spec((GDN_V, D_MODEL)), _const_spec((MEM_Q, D_MODEL)),
                  _const_spec((D_MODEL, D_MODEL)), _const_spec((1, D_MODEL)), _const_spec((D_MODEL, D_FF)), _const_spec((D_FF, D_MODEL))],
        out_specs=tok(D_MODEL),
        out_shape=jax.ShapeDtypeStruct((n, D_MODEL), F32),
        compiler_params=_cparams(("parallel",)), name="merge_ffn",
    )(o_nsa, o_gdn, o_mem, gates, x2d, wts["w_br_nsa"], wts["w_br_gdn"], wts["w_br_mem"], wts["w_out"], wts["g_ffn"],
      wts["w_ff1"], wts["w_ff2"])


def _prep_weights(g_mix, w_in, nsa_q_norm, nsa_k_norm, cmp_pe, cmp_w1, cmp_w2, gdn_conv_w, gdn_A_log, gdn_dt_bias, gdn_o_norm,
                  g_mem, w_mem_kv, mem_q_norm, mem_k_norm, w_br_nsa, w_br_gdn, w_br_mem, w_out, g_ffn, w_ff1, w_ff2):
    offs = [0]
    for s in IN_SPLITS:
        offs.append(offs[-1] + s)
    wt = w_in.T
    seg = lambda i: wt[offs[i]:offs[i + 1]]
    small = jnp.concatenate([seg(2), seg(4), seg(5)], axis=0)
    small = jnp.pad(small, ((0, SMALL_W - small.shape[0]), (0, 0)))
    kv = seg(1)
    k_of = lambda c: kv[c * KV_ROW:c * KV_ROW + KV_ROW // 2]
    w_tok = jnp.concatenate([seg(0), seg(3), seg(6), seg(7), seg(8), small, k_of(1), k_of(2)], axis=0).T.astype(BF16)
    w_ft = jnp.concatenate([kv, seg(0), jnp.pad(seg(2), ((0, G_ROWS - 3 * NSA_HEADS), (0, 0)))], axis=0).astype(BF16)
    R = CMP_BLOCK // CMP_STRIDE
    w1r = cmp_w1.reshape(2, R, CMP_STRIDE, NSA_HD, CMP_HID)
    wbd = jnp.einsum("krsdf,gh->ksgdhrf", w1r, jnp.eye(NSA_KV_HEADS, dtype=F32))
    wbd = wbd.reshape(2, CMP_STRIDE * NSA_KV_HEADS * NSA_HD, NSA_KV_HEADS * R * CMP_HID).astype(BF16)
    row = lambda v: v.reshape(1, -1)
    return dict(
        g_mix=row(g_mix), w_tok=w_tok, w_ft=w_ft, qn=row(jnp.tile(nsa_q_norm, LANES // NSA_HD)), qn_col=nsa_q_norm.reshape(NSA_HD, 1),
        kn=nsa_k_norm.reshape(3, NSA_HD, 1), kn_row=jnp.tile(nsa_k_norm, (1, LANES // NSA_HD)), mqn=row(mem_q_norm), mkn=row(mem_k_norm), g_mem=row(g_mem),
        w_mem_kv=w_mem_kv.astype(BF16), wbd=wbd,
        pe=jnp.broadcast_to(cmp_pe.reshape(2, 1, CMP_BLOCK * NSA_HD), (2, 8, CMP_BLOCK * NSA_HD)),
        w1=cmp_w1.astype(BF16), w2=cmp_w2.astype(BF16), conv_w=gdn_conv_w,
        alog=row(gdn_A_log), dtb=row(gdn_dt_bias), alog_t=gdn_A_log.reshape(-1, 1), dtb_t=gdn_dt_bias.reshape(-1, 1),
        onorm=row(gdn_o_norm), w_br_nsa=w_br_nsa.astype(BF16), w_br_gdn=w_br_gdn.astype(BF16), w_br_mem=w_br_mem.astype(BF16),
        w_out=w_out.astype(BF16), g_ffn=row(g_ffn), w_ff1=w_ff1.astype(BF16), w_ff2=w_ff2.astype(BF16))


def _rows_5d(kt):
    B, _, N = kt.shape
    return jnp.transpose(kt.reshape(B, 2, NSA_KV_HEADS, NSA_HD, N), (0, 4, 1, 2, 3))


def _feature_major(rows):
    B, N = rows.shape[0], rows.shape[1]
    return jnp.transpose(rows, (0, 2, 3, 4, 1)).reshape(B, KV_ROW, N)


def kernel(x_prompt, x_sample, cache_cmp_kv, cache_sel_kv, cache_win_kv, state_gdn, state_gdn_conv, cache_mem_kv, page_table, mem_prompt, g_mix, w_in, nsa_q_norm, nsa_k_norm, cmp_pe, cmp_w1, cmp_w2, gdn_conv_w, gdn_A_log, gdn_dt_bias, gdn_o_norm, g_mem, w_mem_kv, mem_q_norm, mem_k_norm, w_br_nsa, w_br_gdn, w_br_mem, w_out, g_ffn, w_ff1, w_ff2):
    wts = _prep_weights(g_mix, w_in, nsa_q_norm, nsa_k_norm, cmp_pe, cmp_w1, cmp_w2, gdn_conv_w, gdn_A_log, gdn_dt_bias,
                        gdn_o_norm, g_mem, w_mem_kv, mem_q_norm, mem_k_norm, w_br_nsa, w_br_gdn, w_br_mem, w_out, g_ffn, w_ff1, w_ff2)
    B, T, D = x_prompt.shape
    nb = x_sample.shape[0]
    assert x_sample.shape[1] == 1 and T % 256 == 0 and T >= WINDOW + 128
    n_pages = page_table.shape[1]
    past = n_pages * PAGE_SIZE

    p = _inproj(x_prompt, jnp.arange(T, dtype=jnp.int32), wts, tm=256, decode=False)
    mkv = _memkv(mem_prompt.reshape(-1, D), wts).reshape(B, -1, 2 * MEM_Q)
    ckv = _pcompress(p["kct"], wts)
    o_nsa = _pattn_t(p, ckv)
    o_gdn, p_state, conv_tail = _gdn_prompt(p["qkv"], p["z"], p["small"], wts)
    o_mem = _mem_prompt(p["qm"], mkv)
    y_prompt = _merge_ffn(o_nsa.reshape(-1, Q_NSA), o_gdn.reshape(-1, GDN_V), o_mem.reshape(-1, MEM_Q),
                          p["gates"].reshape(-1, N_BRANCH * D), x_prompt.reshape(-1, D), wts, tm=256).reshape(B, T, D)
    p_cmp, p_sel = _rows_5d(p["kct"]), _rows_5d(p["kst"])
    p_win = _rows_5d(p["kwt"][:, :, T - min(WINDOW, T):])
    p_conv = conv_tail[:, 8 - (CONV_W - 1):, :]
    p_mem_kv = mkv.reshape(B, -1, 2, MEM_HEADS, MEM_HD)

    xs = x_sample.reshape(1, nb, D)
    s = _inproj(xs, jnp.full((nb,), past, jnp.int32), wts, tm=nb, decode=True)
    sq, sqkv, sz, sqm, sgates, ssmall, skct, skst, skwt = (s[k] for k in ("q", "qkv", "z", "qm", "gates", "small", "kct", "kst", "kwt"))
    pool_cmp = _feature_major(cache_cmp_kv)
    pool_sel = _feature_major(cache_sel_kv)
    q3 = sq.reshape(nb, 1, Q_NSA)
    ocmp, idx = _s1(pool_cmp, page_table, q3, wts, q_pos=past)
    n_sel = -(-(past + 1) // SEL_BLOCK)
    n_top = min(SEL_TOP, n_sel)
    idx = idx[:, :NSA_KV_HEADS, :n_top].reshape(nb, NSA_KV_HEADS * n_top)
    blk_per_page = PAGE_SIZE // SEL_BLOCK
    is_new = idx * SEL_BLOCK >= past
    page = jnp.take_along_axis(page_table, jnp.where(is_new, 0, idx // blk_per_page), axis=1)
    meta = jnp.where(is_new, blk_per_page, idx % blk_per_page).astype(jnp.int32)
    tokrow = lambda kt: jnp.transpose(kt[0], (1, 0)).reshape(nb, 1, KV_ROW)
    cwint = _feature_major(cache_win_kv)
    o_nsa_s, swin = _s2(pool_sel, page.astype(jnp.int32), meta, q3, ocmp, ssmall.reshape(nb, 1, SMALL_W), tokrow(skst), tokrow(skwt),
                        skwt[0], cwint, n_top)
    xx4 = jnp.concatenate([jnp.transpose(state_gdn_conv, (1, 0, 2)), sqkv], axis=0)
    o_gdn_s, s_state = _gdn_sample(xx4, sz[0], ssmall[0], state_gdn, wts)
    o_mem_s = _mem_sample(sqm.reshape(nb, 1, MEM_Q), cache_mem_kv)
    y_sample = _merge_ffn(o_nsa_s.reshape(nb, Q_NSA), o_gdn_s.reshape(nb, GDN_V), o_mem_s.reshape(nb, MEM_Q), sgates[0], x_sample.reshape(nb, D),
                          wts, tm=nb).reshape(nb, 1, D)
    s_cmp = jnp.transpose(skct[0], (1, 0)).reshape(nb, 1, 2, NSA_KV_HEADS, NSA_HD)
    s_sel = jnp.transpose(skst[0], (1, 0)).reshape(nb, 1, 2, NSA_KV_HEADS, NSA_HD)
    s_win = _rows_5d(swin)
    s_conv = jnp.transpose(xx4[1:], (1, 0, 2))
    return (y_prompt, y_sample, p_cmp, p_sel, p_win, p_state, p_conv, p_mem_kv, s_cmp, s_sel, s_win, s_state, s_conv)
```

```python
import functools

import jax
import jax.numpy as jnp
from jax import lax
from jax.experimental import pallas as pl
from jax.experimental.pallas import tpu as pltpu

F32 = jnp.float32
BF16 = jnp.bfloat16

D_MODEL = 1024
PAGE_SIZE = 128
NSA_HEADS = 8
NSA_KV_HEADS = 2
NSA_HD = 64
NSA_GROUP = NSA_HEADS // NSA_KV_HEADS
NSA_SCALE = NSA_HD ** -0.5
LOG2_E = 1.4426950408889634
CMP_BLOCK = 32
CMP_STRIDE = 16
CMP_HID = 128
SEL_BLOCK = 64
SEL_TOP = 16
WINDOW = 512
FORCE_SCORE = 1e9
GDN_HEADS = 4
GDN_DK = 128
GDN_DV = 128
CONV_W = 4
GDN_CHUNK = 64
MEM_HEADS = 4
MEM_HD = 128
MEM_SCALE = MEM_HD ** -0.5
D_FF = 4 * D_MODEL
ROPE_THETA = 10000.0
EPS = 1e-6

Q_NSA = NSA_HEADS * NSA_HD
KV_ROW = 2 * NSA_KV_HEADS * NSA_HD
GDN_QK = GDN_HEADS * GDN_DK
GDN_V = GDN_HEADS * GDN_DV
CONV_DIM = 2 * GDN_QK + GDN_V
MEM_Q = MEM_HEADS * MEM_HD
N_BRANCH = 3
IN_SPLITS = (Q_NSA, 3 * KV_ROW, 3 * NSA_HEADS, CONV_DIM, GDN_HEADS, GDN_HEADS, GDN_V, MEM_Q, N_BRANCH * D_MODEL)

LANES = 128
SMALL_W = LANES
G_NSA_OFF, B_OFF, A_OFF = 0, 3 * NSA_HEADS, 3 * NSA_HEADS + GDN_HEADS
TOK_Q, TOK_QKV, TOK_Z, TOK_QM, TOK_GBR, TOK_SMALL, TOK_KTOK = 0, 512, 2048, 2560, 3072, 6144, 6272
TOK_W = TOK_KTOK + 2 * LANES
FT_KV, FT_Q, FT_G = 0, 3 * KV_ROW, 3 * KV_ROW + Q_NSA
G_ROWS = 32
FT_W = FT_G + G_ROWS
V7X_VMEM_LIMIT = 56 * 1024 * 1024


def _cparams(sem):
    return pltpu.CompilerParams(dimension_semantics=sem, vmem_limit_bytes=V7X_VMEM_LIMIT)


def _dot(a, b):
    return jnp.dot(a.astype(BF16), b.astype(BF16), preferred_element_type=F32)


def _dot_nt(a, b):
    return lax.dot_general(a.astype(BF16), b.astype(BF16), (((1,), (1,)), ((), ())), preferred_element_type=F32)


def _split_bf16(a):
    hi = a.astype(BF16)
    return hi, (a - hi.astype(F32)).astype(BF16)


def _dot3(a, b):
    ah, al = _split_bf16(a)
    bh, bl = _split_bf16(b)
    d = lambda x, y: jnp.dot(x, y, preferred_element_type=F32)
    return d(ah, bh) + (d(ah, bl) + d(al, bh))


def _rms(x, axis=-1):
    return x * lax.rsqrt(jnp.mean(x * x, axis=axis, keepdims=True) + EPS)


def _const_spec(shape):
    nd = len(shape)
    return pl.BlockSpec(shape, lambda *_: (0,) * nd, pipeline_mode=pl.Buffered(1))


def _inproj_kernel(x_ref, gmix_ref, wtok_ref, wft_ref, qn_ref, qnc_ref, kn_ref, knr_ref, cq_ref, sq_ref, ck_ref, sk_ref, mqn_ref, blk_ref,
                   qkv_ref, z_ref, qm_ref, gate_ref, small_ref, kc_ref, ks_ref, kw_ref, *extra_refs, decode):
    x = x_ref[0]
    ub = (_rms(x) * gmix_ref[...]).astype(BF16)
    tm = x.shape[0]
    lane = lax.broadcasted_iota(jnp.int32, (tm, LANES), 1)
    lo = lane < NSA_HD
    first_half = (lane % NSA_HD) < (NSA_HD // 2)

    def norm_rope_slab(col0, gain):
        qs = jnp.dot(ub, wtok_ref[:, col0:col0 + LANES], preferred_element_type=F32)
        sq = qs * qs
        ss_lo = jnp.sum(jnp.where(lo, sq, 0.0), axis=-1, keepdims=True)
        ss_hi = jnp.sum(jnp.where(lo, 0.0, sq), axis=-1, keepdims=True)
        r = jnp.where(lo, lax.rsqrt(ss_lo / NSA_HD + EPS), lax.rsqrt(ss_hi / NSA_HD + EPS))
        qs = qs * r * gain
        rot = jnp.where(first_half, -pltpu.roll(qs, LANES - NSA_HD // 2, axis=1), pltpu.roll(qs, NSA_HD // 2, axis=1))
        return qs * cq_ref[...] + rot * sq_ref[...]

    if decode:
        (q_ref,) = extra_refs
        for i in range(Q_NSA // LANES):
            q_ref[0, :, i * LANES:(i + 1) * LANES] = (norm_rope_slab(TOK_Q + i * LANES, qn_ref[...]) * NSA_SCALE).astype(BF16)
    else:
        kstok0_ref, kstok1_ref, kwtok_ref, qt_ref, gt_ref = extra_refs
        ks = norm_rope_slab(TOK_KTOK, knr_ref[1:2, :])
        kstok0_ref[0] = jnp.where(lo, ks, blk_ref[...]).astype(BF16)
        kstok1_ref[0] = jnp.where(lo, pltpu.roll(ks, NSA_HD, axis=1), blk_ref[...]).astype(BF16)
        kwtok_ref[0] = norm_rope_slab(TOK_KTOK + LANES, knr_ref[2:3, :]).astype(BF16)
    qkv_ref[0] = jnp.dot(ub, wtok_ref[:, TOK_QKV:TOK_Z], preferred_element_type=F32)
    z_ref[0] = jnp.dot(ub, wtok_ref[:, TOK_Z:TOK_QM], preferred_element_type=F32)
    for h in range(MEM_HEADS):
        qm = jnp.dot(ub, wtok_ref[:, TOK_QM + h * MEM_HD:TOK_QM + (h + 1) * MEM_HD], preferred_element_type=F32)
        qm_ref[0, :, h * MEM_HD:(h + 1) * MEM_HD] = (_rms(qm) * mqn_ref[...]).astype(BF16)
    for i in range(N_BRANCH):
        gb = jnp.dot(ub, wtok_ref[:, TOK_GBR + i * D_MODEL:TOK_GBR + (i + 1) * D_MODEL], preferred_element_type=F32)
        gate_ref[0, :, i * D_MODEL:(i + 1) * D_MODEL] = jax.nn.sigmoid(gb)
    small_ref[0] = jnp.dot(ub, wtok_ref[:, TOK_SMALL:TOK_SMALL + SMALL_W], preferred_element_type=F32)
    ft_rows = FT_Q if decode else FT_W
    ft = lax.dot_general(wft_ref[0:ft_rows, :], ub, (((1,), (1,)), ((), ())), preferred_element_type=F32)
    cos = ck_ref[...]
    sin = sk_ref[...]
    half = NSA_HD // 2

    def norm_rope_rows(row0, gain_col):
        kh = _rms(ft[row0:row0 + NSA_HD, :], axis=0) * gain_col
        x1, x2 = kh[:half], kh[half:]
        return x1 * cos - x2 * sin, x2 * cos + x1 * sin

    for c, out_ref in enumerate((kc_ref, ks_ref, kw_ref)):
        base = FT_KV + c * KV_ROW
        for g in range(NSA_KV_HEADS):
            r1, r2 = norm_rope_rows(base + g * NSA_HD, kn_ref[c])
            out_ref[0, g * NSA_HD:g * NSA_HD + half, :] = r1
            out_ref[0, g * NSA_HD + half:(g + 1) * NSA_HD, :] = r2
        out_ref[0, KV_ROW // 2:, :] = ft[base + KV_ROW // 2:base + KV_ROW, :]
    if not decode:
        for h in range(NSA_HEADS):
            r1, r2 = norm_rope_rows(FT_Q + h * NSA_HD, qnc_ref[...])
            qt_ref[0, h * NSA_HD:(h + 1) * NSA_HD, :] = (jnp.concatenate([r1, r2], axis=0) * (NSA_SCALE * LOG2_E)).astype(BF16)
        gt_ref[0] = jax.nn.sigmoid(ft[FT_G:FT_G + G_ROWS, :])


def _inproj(x, pos, wts, tm, decode):
    B, T, _ = x.shape
    half = NSA_HD // 2
    inv = ROPE_THETA ** (-jnp.arange(half, dtype=F32) / half)
    ang = pos.astype(F32)[:, None] * inv[None, :]
    cos, sin = jnp.cos(ang), jnp.sin(ang)
    cq, sq = jnp.tile(cos, (1, LANES // half)), jnp.tile(sin, (1, LANES // half))
    ck, sk = cos.T, sin.T
    tok = lambda w: pl.BlockSpec((1, tm, w), lambda b, t: (b, t, 0))
    ftm = lambda r: pl.BlockSpec((1, r, tm), lambda b, t: (b, 0, t))
    tok_shape = lambda w, dt: jax.ShapeDtypeStruct((B, T, w), dt)
    ft_shape = lambda r, dt: jax.ShapeDtypeStruct((B, r, T), dt)
    names = ["qkv", "z", "qm", "gates", "small", "kct", "kst", "kwt"]
    out_specs = [tok(CONV_DIM), tok(GDN_V), tok(MEM_Q), tok(N_BRANCH * D_MODEL), tok(SMALL_W), ftm(KV_ROW), ftm(KV_ROW), ftm(KV_ROW)]
    out_shape = [tok_shape(CONV_DIM, F32), tok_shape(GDN_V, F32), tok_shape(MEM_Q, BF16), tok_shape(N_BRANCH * D_MODEL, F32),
                 tok_shape(SMALL_W, F32), ft_shape(KV_ROW, F32), ft_shape(KV_ROW, F32), ft_shape(KV_ROW, F32)]
    if decode:
        names += ["q"]
        out_specs += [tok(Q_NSA)]
        out_shape += [tok_shape(Q_NSA, BF16)]
    else:
        names += ["ks_tok0", "ks_tok1", "kw_tok", "qt", "gt"]
        out_specs += [tok(LANES), tok(LANES), tok(LANES), ftm(Q_NSA), ftm(G_ROWS)]
        out_shape += [tok_shape(LANES, BF16)] * 3 + [ft_shape(Q_NSA, BF16), ft_shape(G_ROWS, F32)]
    blk = (pos[:, None] // SEL_BLOCK + NSA_HD == jnp.arange(LANES)[None, :]).astype(F32)
    per_t = pl.BlockSpec((tm, LANES), lambda b, t: (t, 0))
    outs = pl.pallas_call(
        functools.partial(_inproj_kernel, decode=decode),
        grid=(B, T // tm),
        in_specs=[tok(D_MODEL), _const_spec((1, D_MODEL)), _const_spec((D_MODEL, TOK_W)), _const_spec((FT_W, D_MODEL)),
                  _const_spec((1, LANES)), _const_spec((NSA_HD, 1)), _const_spec((3, NSA_HD, 1)), _const_spec((3, LANES)),
                  per_t, per_t,
                  pl.BlockSpec((half, tm), lambda b, t: (0, t)), pl.BlockSpec((half, tm), lambda b, t: (0, t)),
                  _const_spec((1, MEM_HD)), per_t],
        out_specs=out_specs, out_shape=out_shape,
        compiler_params=_cparams(("parallel", "parallel")),
        name="inproj",
    )(x, wts["g_mix"], wts["w_tok"], wts["w_ft"], wts["qn"], wts["qn_col"], wts["kn"], wts["kn_row"], cq, sq, ck, sk, wts["mqn"], blk)
    return dict(zip(names, outs))


def _memkv_kernel(m_ref, g_ref, w_ref, kn_ref, o_ref):
    u = _rms(m_ref[...]) * g_ref[...]
    kv = _dot(u, w_ref[...])
    for h in range(MEM_HEADS):
        sl = slice(h * MEM_HD, (h + 1) * MEM_HD)
        o_ref[:, sl] = _rms(kv[:, sl]) * kn_ref[...]
    o_ref[:, MEM_Q:] = kv[:, MEM_Q:]


def _memkv(mem2d, wts, tm=512):
    n = mem2d.shape[0]
    tm = min(tm, n)
    return pl.pallas_call(
        _memkv_kernel, grid=(n // tm,),
        in_specs=[pl.BlockSpec((tm, D_MODEL), lambda i: (i, 0)), _const_spec((1, D_MODEL)),
                  _const_spec((D_MODEL, 2 * MEM_Q)), _const_spec((1, MEM_HD))],
        out_specs=pl.BlockSpec((tm, 2 * MEM_Q), lambda i: (i, 0)),
        out_shape=jax.ShapeDtypeStruct((n, 2 * MEM_Q), F32),
        compiler_params=_cparams(("parallel",)), name="memkv",
    )(mem2d, wts["g_mem"], wts["w_mem_kv"], wts["mkn"])


CHUNK_PITCH = CMP_STRIDE + 8


def _transpose_pages(get_page, pages, xs_ref):
    cpp = PAGE_SIZE // CMP_STRIDE
    for p in pages:
        xt = get_page(p).T
        for c in range(cpp):
            r0 = (p * cpp + c) * CHUNK_PITCH
            xs_ref[0, r0:r0 + CMP_STRIDE, :] = xt[c * CMP_STRIDE:(c + 1) * CMP_STRIDE, :LANES]
            xs_ref[1, r0:r0 + CMP_STRIDE, :] = xt[c * CMP_STRIDE:(c + 1) * CMP_STRIDE, LANES:]


def _compress(n_chunks, xs_ref, hs_ref, wbd_ref, pe_ref, w1_ref, w2_ref, between=None):
    hs_ref[n_chunks:, :] = jnp.zeros((8, CMP_HID), F32)
    parts = []
    for kv in range(2):
        lhs = jnp.concatenate([xs_ref[kv, pl.ds(s, n_chunks, stride=CHUNK_PITCH), :].astype(BF16) for s in range(CMP_STRIDE)], axis=1)
        if between is not None:
            between(kv)
        h = jnp.dot(lhs, wbd_ref[kv], preferred_element_type=F32)
        pe_h = jnp.dot(pe_ref[kv].astype(BF16), w1_ref[kv], preferred_element_type=F32)[0:1]
        for g in range(NSA_KV_HEADS):
            h0 = h[:, g * 2 * CMP_HID:g * 2 * CMP_HID + CMP_HID]
            hs_ref[0:n_chunks, :] = h[:, g * 2 * CMP_HID + CMP_HID:(g + 1) * 2 * CMP_HID]
            hh = h0 + hs_ref[pl.ds(1, n_chunks), :] + pe_h
            parts.append(jnp.dot(jax.nn.gelu(hh).astype(BF16), w2_ref[kv], preferred_element_type=F32))
    return jnp.concatenate(parts, axis=1)


def _masked_softmax(s, allow):
    s = jnp.where(allow, s, -1e30)
    e = jnp.exp(s - jnp.max(s, axis=-1, keepdims=True))
    p = e / jnp.sum(e, axis=-1, keepdims=True)
    return jnp.where(allow, p, 0.0)


def _cmp_probs(qg, kc, tpos, n_cmp):
    n_chunks = kc.shape[0]
    s = _dot_nt(qg, kc)
    i = lax.broadcasted_iota(jnp.int32, (1, n_chunks), 1)
    allow = jnp.where(i < n_cmp, i * CMP_STRIDE + CMP_BLOCK - 1, jnp.int32(2 ** 30)) <= tpos
    return _masked_softmax(s, allow)


def _select_blocks(psum, tpos, n_cmp, n_sel, ns_pad):
    n_chunks = psum.shape[1]
    ci = lax.broadcasted_iota(jnp.int32, (n_chunks, ns_pad), 0)
    sj = lax.broadcasted_iota(jnp.int32, (n_chunks, ns_pad), 1)
    hit = (ci * CMP_STRIDE < (sj + 1) * SEL_BLOCK) & (ci * CMP_STRIDE + CMP_BLOCK > sj * SEL_BLOCK) & (ci < n_cmp) & (sj < n_sel)
    imp = _dot(psum, jnp.where(hit, 1.0, 0.0))
    jj = lax.broadcasted_iota(jnp.int32, (1, ns_pad), 1)
    imp = jnp.where((jj * SEL_BLOCK <= tpos) & (jj < n_sel), imp, -jnp.inf)
    imp = jnp.where((jj == 0) | (jj == tpos // SEL_BLOCK), FORCE_SCORE, imp)
    cnt = jnp.zeros(imp.shape, F32)
    for j in range(n_sel):
        col = imp[:, j:j + 1]
        later = jnp.where(jj > j, 1.0, 0.0)
        cnt = cnt + jnp.where(col > imp, 1.0, jnp.where(col == imp, later, 0.0))
    return jnp.where(cnt < min(SEL_TOP, n_sel), 1.0, 0.0)


def _pcompress_kernel(kc_ref, wbd_ref, pe_ref, w1_ref, w2_ref, o_ref, xs_ref, hs_ref):
    n_pages = kc_ref.shape[2] // PAGE_SIZE
    _transpose_pages(lambda p: kc_ref[0, :, p * PAGE_SIZE:(p + 1) * PAGE_SIZE], range(n_pages), xs_ref)
    o_ref[0] = _compress(n_pages * PAGE_SIZE // CMP_STRIDE, xs_ref, hs_ref, wbd_ref, pe_ref, w1_ref, w2_ref)


def _pcompress(kct, wts):
    B, _, T = kct.shape
    n_chunks = T // CMP_STRIDE
    return pl.pallas_call(
        _pcompress_kernel, grid=(B,),
        in_specs=[pl.BlockSpec((1, KV_ROW, T), lambda b: (b, 0, 0)), _const_spec(wts["wbd"].shape), _const_spec(wts["pe"].shape),
                  _const_spec(wts["w1"].shape), _const_spec(wts["w2"].shape)],
        out_specs=pl.BlockSpec((1, n_chunks, KV_ROW), lambda b: (b, 0, 0)),
        out_shape=jax.ShapeDtypeStruct((B, n_chunks, KV_ROW), F32),
        scratch_shapes=[pltpu.VMEM((2, n_chunks * CHUNK_PITCH, LANES), F32), pltpu.VMEM((n_chunks + 8, CMP_HID), F32)],
        compiler_params=_cparams(("parallel",)), name="prompt_compress",
    )(kct, wts["wbd"], wts["pe"], wts["w1"], wts["w2"])


KEY_TILE = 256


def _select_blocks_t(psum_t, tl, n_cmp, n_sel, ns_rows):
    n_chunks = psum_t.shape[0]
    sj = lax.broadcasted_iota(jnp.int32, (ns_rows, n_chunks), 0)
    ci = lax.broadcasted_iota(jnp.int32, (ns_rows, n_chunks), 1)
    hit = (ci * CMP_STRIDE < (sj + 1) * SEL_BLOCK) & (ci * CMP_STRIDE + CMP_BLOCK > sj * SEL_BLOCK) & (ci < n_cmp) & (sj < n_sel)
    imp = _dot(jnp.where(hit, 1.0, 0.0), psum_t)
    jj = lax.broadcasted_iota(jnp.int32, (ns_rows, 1), 0)
    imp = jnp.where((jj * SEL_BLOCK <= tl) & (jj < n_sel), imp, -jnp.inf)
    imp = jnp.where((jj == 0) | (jj == tl // SEL_BLOCK), FORCE_SCORE, imp)
    cnt = jnp.zeros(imp.shape, F32)
    for j in range(n_sel):
        row = imp[j:j + 1, :]
        later = jnp.where(jj > j, 1.0, 0.0)
        cnt = cnt + jnp.where(row > imp, 1.0, jnp.where(row == imp, later, 0.0))
    return jnp.where(cnt < min(SEL_TOP, n_sel), 1.0, 0.0)


V_AUG = NSA_HD + 8


def _col_attend(n_tiles, tile_w, k_tile, q_list, v_aug, bias_tile, s_ref, e_ref):
    outs = []
    n_keys = n_tiles * tile_w
    n_heads = len(q_list)
    m_prev = None
    for st in range(n_heads + 1):
        m8 = None
        for i in range(n_tiles):
            rows = slice(i * tile_w, (i + 1) * tile_w)
            if st < n_heads:
                s = jnp.dot(k_tile(i), q_list[st], preferred_element_type=F32)
                bt = bias_tile(i)
                if bt is not None:
                    s = s + bt
                s_ref[st, rows, :] = s
                for r in range(tile_w // 8):
                    m8 = s[8 * r:8 * r + 8] if m8 is None else jnp.maximum(m8, s[8 * r:8 * r + 8])
            if st >= 1:
                e_ref[st - 1, rows, :] = jnp.exp2(s_ref[st - 1, rows, :] - m_prev).astype(BF16)
        if st >= 1:
            a = jnp.dot(v_aug, e_ref[st - 1, 0:n_keys, :], preferred_element_type=F32)
            outs.append(a[:NSA_HD] / a[NSA_HD:NSA_HD + 1])
        if st < n_heads:
            m_prev = jnp.max(m8, axis=0, keepdims=True)
    return outs


def _ones_rows(v_t):
    n = v_t.shape[1]
    r = lax.broadcasted_iota(jnp.int32, (V_AUG - NSA_HD, n), 0)
    return jnp.concatenate([v_t, jnp.where(r == 0, 1.0, 0.0)], axis=0).astype(BF16)


def _pattn_t_kernel(qt_ref, gt_ref, ckv_ref, kstok0_ref, kstok1_ref, vst_ref, kwtok_ref, vwt_ref, o_ref, osel_ref, s_ref, e_ref,
                    *, T, tq, n_ext):
    kstok_refs = (kstok0_ref, kstok1_ref)
    n_chunks = T // CMP_STRIDE
    n_cmp = n_chunks - CMP_BLOCK // CMP_STRIDE + 1
    n_sel = -(-T // SEL_BLOCK)
    ns_rows = -(-n_sel // 8) * 8
    span = WINDOW + tq
    qi = pl.program_id(1)
    q0 = qi * tq
    qt = qt_ref[0]
    gt = gt_ref[0]
    ckv = ckv_ref[0]
    tl = q0 + lax.broadcasted_iota(jnp.int32, (1, tq), 1)
    wstart = pl.multiple_of(jnp.maximum(q0 - WINDOW, 0), LANES)
    ci = lax.broadcasted_iota(jnp.int32, (n_chunks, 1), 0)
    bias_c = jnp.where(jnp.where(ci < n_cmp, ci * CMP_STRIDE + CMP_BLOCK - 1, jnp.int32(2 ** 30)) <= tl, 0.0, -1e30)
    zeros_half = jnp.zeros((NSA_HD, tq), BF16)
    tiles_per_ext = T // tq // n_ext
    kw = wstart + lax.broadcasted_iota(jnp.int32, (span, 1), 0)
    bias_w = jnp.where(kw <= tl, jnp.where(kw > tl - WINDOW, 0.0, -1e30), -1e30)
    o_cmp, o_win, sel_bias, q_hs = [], [], [], []
    for g in range(NSA_KV_HEADS):
        ksl = slice(g * NSA_HD, (g + 1) * NSA_HD)
        vsl = slice(KV_ROW // 2 + g * NSA_HD, KV_ROW // 2 + (g + 1) * NSA_HD)
        q_h = [qt[(g * NSA_GROUP + j) * NSA_HD:(g * NSA_GROUP + j + 1) * NSA_HD, :] for j in range(NSA_GROUP)]
        q_pad = [jnp.concatenate([qh, zeros_half] if g == 0 else [zeros_half, qh], axis=0) for qh in q_h]
        q_hs.append(q_h)
        kc = ckv[:, ksl].astype(BF16)
        vc_t = ckv[:, vsl].T.astype(BF16)
        psum = None
        for j in range(NSA_GROUP):
            s = jnp.dot(kc, q_h[j], preferred_element_type=F32) + bias_c
            e = jnp.exp2(s - jnp.max(s, axis=0, keepdims=True))
            p = jnp.where(bias_c == 0.0, e / jnp.sum(e, axis=0, keepdims=True), 0.0)
            o_cmp.append(jnp.dot(vc_t, p.astype(BF16), preferred_element_type=F32))
            psum = p if psum is None else psum + p
        sel = _select_blocks_t(psum, tl, n_cmp, n_sel, ns_rows)
        sel_bias.append(jnp.concatenate([(sel - 1.0) * 1e30, jnp.zeros((LANES - NSA_HD - ns_rows, tq), F32)], axis=0).astype(BF16))
        vw_aug = _ones_rows(vwt_ref[0, ksl, pl.ds(wstart, span)])
        o_win += _col_attend(
            span // LANES, LANES, lambda i: kwtok_ref[0, pl.ds(wstart + i * LANES, LANES), :], q_pad, vw_aug,
            lambda i: bias_w[i * LANES:(i + 1) * LANES, :], s_ref, e_ref)

    ext_w = T // n_ext
    for v in range(n_ext):
        kext = (v + 1) * ext_w
        n_tiles = kext // KEY_TILE
        first_diag = n_tiles - ext_w // KEY_TILE

        @pl.when((qi >= v * tiles_per_ext) & (qi < (v + 1) * tiles_per_ext))
        def _():
            keyd = (kext - ext_w) + lax.broadcasted_iota(jnp.int32, (ext_w, 1), 0)
            causal = jnp.where(keyd <= tl, 0.0, -1e30)
            for g in range(NSA_KV_HEADS):
                ksl = slice(g * NSA_HD, (g + 1) * NSA_HD)
                vs_aug = _ones_rows(vst_ref[0, ksl, 0:kext])
                q_aug = [jnp.concatenate([qh, sel_bias[g]], axis=0) for qh in q_hs[g]]
                outs = _col_attend(
                    n_tiles, KEY_TILE, lambda i: kstok_refs[g][0, i * KEY_TILE:(i + 1) * KEY_TILE, :], q_aug, vs_aug,
                    lambda i: causal[(i - first_diag) * KEY_TILE:(i - first_diag + 1) * KEY_TILE, :] if i >= first_diag else None,
                    s_ref, e_ref)
                for j in range(NSA_GROUP):
                    h = g * NSA_GROUP + j
                    osel_ref[h * NSA_HD:(h + 1) * NSA_HD, :] = outs[j]

    heads = [gt[3 * h:3 * h + 1, :] * o_cmp[h] + gt[3 * h + 1:3 * h + 2, :] * osel_ref[h * NSA_HD:(h + 1) * NSA_HD, :]
             + gt[3 * h + 2:3 * h + 3, :] * o_win[h] for h in range(NSA_HEADS)]
    o_ref[0] = jnp.concatenate(heads, axis=0).T


def _pattn_t(p, ckv, tq=256, n_ext=8):
    B, _, T = p["qt"].shape
    n_chunks = T // CMP_STRIDE
    assert T % (n_ext * tq) == 0 and (T // n_ext) % KEY_TILE == 0 and -(-T // SEL_BLOCK) <= LANES - NSA_HD
    full = lambda b, t: (b, 0, 0)
    vhalf = pl.BlockSpec((1, KV_ROW // 2, T), lambda b, t: (b, 1, 0))
    return pl.pallas_call(
        functools.partial(_pattn_t_kernel, T=T, tq=tq, n_ext=n_ext), grid=(B, T // tq),
        in_specs=[pl.BlockSpec((1, Q_NSA, tq), lambda b, t: (b, 0, t)), pl.BlockSpec((1, G_ROWS, tq), lambda b, t: (b, 0, t)),
                  pl.BlockSpec((1, n_chunks, KV_ROW), full), pl.BlockSpec((1, T, LANES), full), pl.BlockSpec((1, T, LANES), full), vhalf,
                  pl.BlockSpec((1, T, LANES), full), vhalf],
        out_specs=pl.BlockSpec((1, tq, Q_NSA), lambda b, t: (b, t, 0)),
        out_shape=jax.ShapeDtypeStruct((B, T, Q_NSA), F32),
        scratch_shapes=[pltpu.VMEM((Q_NSA, tq), F32), pltpu.VMEM((NSA_GROUP, T, tq), F32), pltpu.VMEM((NSA_GROUP, T, tq), BF16)],
        compiler_params=_cparams(("parallel", "parallel")), name="prompt_nsa_attn",
    )(p["qt"], p["gt"], ckv, p["ks_tok0"], p["ks_tok1"], p["kst"], p["kw_tok"], p["kwt"])


def _s1_kernel(pt_ref, pool_ref, q_ref, wbd_ref, pe_ref, w1_ref, w2_ref, ocmp_ref, idx_ref, pg_ref, sem_ref, xs_ref, hs_ref,
               *, n_pages, q_pos):
    n_chunks = n_pages * PAGE_SIZE // CMP_STRIDE
    n_cmp = n_chunks - CMP_BLOCK // CMP_STRIDE + 1
    n_sel = -(-(q_pos + 1) // SEL_BLOCK)
    ns_pad = -(-n_sel // LANES) * LANES
    b = pl.program_id(0)
    slot = b % 2

    def page_copy(seq, sl, j):
        return pltpu.make_async_copy(pool_ref.at[pt_ref[seq, j]], pg_ref.at[sl, j], sem_ref.at[sl])

    @pl.when(b == 0)
    def _():
        for j in range(n_pages):
            page_copy(0, 0, j).start()

    @pl.when(b + 1 < pl.num_programs(0))
    def _():
        for j in range(n_pages):
            page_copy(b + 1, 1 - slot, j).start()

    for j in range(n_pages):
        page_copy(b, slot, j).wait()
    _transpose_pages(lambda p: pg_ref[slot, p], range(n_pages), xs_ref)
    ckv = _compress(n_chunks, xs_ref, hs_ref, wbd_ref, pe_ref, w1_ref, w2_ref)
    qrow = q_ref[0].astype(F32)
    q8 = jnp.concatenate([qrow[:, h * NSA_HD:(h + 1) * NSA_HD] for h in range(NSA_HEADS)], axis=0)
    row = lax.broadcasted_iota(jnp.int32, (NSA_HEADS, 1), 0)
    tpos = jnp.full((NSA_HEADS, 1), q_pos, jnp.int32)
    o_all = jnp.zeros((NSA_HEADS, NSA_HD), F32)
    psum = jnp.zeros((NSA_HEADS, n_chunks), F32)
    for g in range(NSA_KV_HEADS):
        kc = ckv[:, g * NSA_HD:(g + 1) * NSA_HD]
        vc = ckv[:, KV_ROW // 2 + g * NSA_HD:KV_ROW // 2 + (g + 1) * NSA_HD]
        p = _cmp_probs(q8, kc, tpos, n_cmp)
        mine = (row // NSA_GROUP) == g
        o_all = jnp.where(mine, _dot(p, vc), o_all)
        pg = jnp.sum(jnp.where(mine, p, 0.0), axis=0, keepdims=True)
        psum = jnp.where(row == g, pg, psum)
    sel = _select_blocks(psum, tpos, n_cmp, n_sel, ns_pad)
    a = lax.broadcasted_iota(jnp.int32, (ns_pad, ns_pad), 0)
    b = lax.broadcasted_iota(jnp.int32, (ns_pad, ns_pad), 1)
    before = jnp.dot(sel.astype(BF16), jnp.where(a < b, 1.0, 0.0).astype(BF16), preferred_element_type=F32)
    jj = lax.broadcasted_iota(jnp.int32, (1, ns_pad), 1).astype(F32)
    lane = lax.broadcasted_iota(jnp.int32, (1, LANES), 1)
    idx = jnp.zeros((NSA_HEADS, LANES), F32)
    for k in range(min(SEL_TOP, n_sel)):
        ik = jnp.sum(jnp.where((sel > 0.5) & (before == k), jj, 0.0), axis=1, keepdims=True)
        idx = jnp.where(lane == k, ik, idx)
    idx_ref[0] = idx.astype(jnp.int32)
    ocmp_ref[0] = jnp.concatenate([o_all, jnp.zeros((NSA_HEADS, LANES - NSA_HD), F32)], axis=1)


def _s1(pool_t, page_table, q3, wts, q_pos):
    B, n_pages = page_table.shape
    n_chunks = n_pages * PAGE_SIZE // CMP_STRIDE

    per_b = lambda b, pt: (b, 0, 0)
    cst = lambda shape: pl.BlockSpec(shape, lambda b, pt: (0,) * len(shape), pipeline_mode=pl.Buffered(1))
    grid_spec = pltpu.PrefetchScalarGridSpec(
        num_scalar_prefetch=1, grid=(B,),
        in_specs=[pl.BlockSpec(memory_space=pl.ANY), pl.BlockSpec((1, 1, Q_NSA), per_b), cst(wts["wbd"].shape), cst(wts["pe"].shape),
                  cst(wts["w1"].shape), cst(wts["w2"].shape)],
        out_specs=[pl.BlockSpec((1, NSA_HEADS, LANES), per_b), pl.BlockSpec((1, NSA_HEADS, LANES), per_b)],
        scratch_shapes=[pltpu.VMEM((2, n_pages, KV_ROW, PAGE_SIZE), F32), pltpu.SemaphoreType.DMA((2,)),
                        pltpu.VMEM((2, n_chunks * CHUNK_PITCH, LANES), F32), pltpu.VMEM((n_chunks + 8, CMP_HID), F32)],
    )
    return pl.pallas_call(
        functools.partial(_s1_kernel, n_pages=n_pages, q_pos=q_pos), grid_spec=grid_spec,
        out_shape=[jax.ShapeDtypeStruct((B, NSA_HEADS, LANES), F32), jax.ShapeDtypeStruct((B, NSA_HEADS, LANES), jnp.int32)],
        compiler_params=_cparams(("arbitrary",)), name="sample_compress_select",
    )(page_table, pool_t, q3, wts["wbd"], wts["pe"], wts["w1"], wts["w2"])


def _s2_kernel(phys_ref, meta_ref, pool_ref, q_ref, ocmp_ref, small_ref, nsel_ref, nwin_ref, nwint_ref, cwin_ref, o_ref, swin_ref,
               pg_ref, sem_ref, *, n_top, win_len):
    n_blk = NSA_KV_HEADS * n_top
    b = pl.program_id(0)
    slot = b % 2

    def slab_copy(seq, sl, j, is_value):
        row0 = (NSA_KV_HEADS * is_value + j // n_top) * NSA_HD
        return pltpu.make_async_copy(pool_ref.at[phys_ref[seq, j], pl.ds(row0, NSA_HD)], pg_ref.at[sl, 2 * j + is_value], sem_ref.at[sl])

    def all_copies(seq, sl):
        return [slab_copy(seq, sl, j, v) for j in range(n_blk) for v in (0, 1)]

    @pl.when(b == 0)
    def _():
        for c in all_copies(0, 0):
            c.start()

    @pl.when(b + 1 < pl.num_programs(0))
    def _():
        for c in all_copies(b + 1, 1 - slot):
            c.start()

    for c in all_copies(b, slot):
        c.wait()
    qrow = q_ref[0].astype(F32)
    q8 = jnp.concatenate([qrow[:, h * NSA_HD:(h + 1) * NSA_HD] for h in range(NSA_HEADS)], axis=0)
    row = lax.broadcasted_iota(jnp.int32, (NSA_HEADS, 1), 0)
    lane = lax.broadcasted_iota(jnp.int32, (1, PAGE_SIZE), 1)
    nsel = nsel_ref[0]
    nwin = nwin_ref[0]
    cwin = cwin_ref[0]
    r = lax.broadcasted_iota(jnp.int32, (1, win_len), 1)
    allow_w = (r > win_len - WINDOW) & (r <= win_len)
    o_sel = jnp.zeros((NSA_HEADS, NSA_HD), F32)
    o_win = jnp.zeros((NSA_HEADS, NSA_HD), F32)
    for g in range(NSA_KV_HEADS):
        mine = (row // NSA_GROUP) == g
        ksl = slice(g * NSA_HD, (g + 1) * NSA_HD)
        vsl = slice(KV_ROW // 2 + g * NSA_HD, KV_ROW // 2 + (g + 1) * NSA_HD)
        kts, vts, masks = [], [], []
        has_new = jnp.zeros((1, 1), F32)
        for k in range(n_top):
            m = meta_ref[b, g * n_top + k]
            kts.append(pg_ref[slot, 2 * (g * n_top + k)])
            vts.append(pg_ref[slot, 2 * (g * n_top + k) + 1])
            masks.append((lane // SEL_BLOCK) == m)
            has_new = has_new + jnp.where(m == 2, 1.0, 0.0)
        kt = jnp.concatenate(kts, axis=1)
        vt = jnp.concatenate(vts, axis=1)
        allow = jnp.concatenate(masks, axis=1)
        s = jnp.where(allow, _dot(q8, kt), -1e30)
        s_new = jnp.where(has_new > 0.5, jnp.sum(q8 * nsel[:, ksl], axis=1, keepdims=True), -1e30)
        mx = jnp.maximum(jnp.max(s, axis=1, keepdims=True), s_new)
        e = jnp.where(allow, jnp.exp(s - mx), 0.0)
        e_new = jnp.where(has_new > 0.5, jnp.exp(s_new - mx), 0.0)
        den = jnp.sum(e, axis=1, keepdims=True) + e_new
        og = (_dot_nt(e, vt) + e_new * nsel[:, vsl]) / den
        o_sel = jnp.where(mine, og, o_sel)
        s = jnp.where(allow_w, _dot(q8, cwin[ksl, :]), -1e30)
        s_new = jnp.sum(q8 * nwin[:, ksl], axis=1, keepdims=True)
        mx = jnp.maximum(jnp.max(s, axis=1, keepdims=True), s_new)
        e = jnp.where(allow_w, jnp.exp(s - mx), 0.0)
        e_new = jnp.exp(s_new - mx)
        den = jnp.sum(e, axis=1, keepdims=True) + e_new
        og = (_dot_nt(e, cwin[vsl, :]) + e_new * nwin[:, vsl]) / den
        o_win = jnp.where(mine, og, o_win)
    o_cmp = ocmp_ref[0][:, :NSA_HD]
    gsig = jax.nn.sigmoid(small_ref[0][:, G_NSA_OFF:G_NSA_OFF + 3 * NSA_HEADS])
    cols = []
    for h in range(NSA_HEADS):
        cols.append(gsig[:, 3 * h:3 * h + 1] * o_cmp[h:h + 1] + gsig[:, 3 * h + 1:3 * h + 2] * o_sel[h:h + 1]
                    + gsig[:, 3 * h + 2:3 * h + 3] * o_win[h:h + 1])
    o_ref[0] = jnp.concatenate(cols, axis=1)
    blane = lax.broadcasted_iota(jnp.int32, (1, nwint_ref.shape[1]), 1)
    newcol = jnp.sum(jnp.where(blane == b, nwint_ref[...], 0.0), axis=1, keepdims=True)
    swin_ref[0] = jnp.concatenate([cwin[:, 1:], newcol], axis=1)


def _s2(pool_t, phys, meta, q3, ocmp, small3, nsel3, nwin3, nwint, cwint, n_top):
    B = q3.shape[0]
    win_len = cwint.shape[2]
    n_blk = NSA_KV_HEADS * n_top

    per_b = lambda shape: pl.BlockSpec(shape, lambda b, ph, me: (b,) + (0,) * (len(shape) - 1))
    grid_spec = pltpu.PrefetchScalarGridSpec(
        num_scalar_prefetch=2, grid=(B,),
        in_specs=[pl.BlockSpec(memory_space=pl.ANY), per_b((1, 1, Q_NSA)), per_b((1, NSA_HEADS, LANES)), per_b((1, 1, SMALL_W)),
                  per_b((1, 1, KV_ROW)), per_b((1, 1, KV_ROW)), pl.BlockSpec(nwint.shape, lambda b, ph, me: (0, 0)),
                  per_b((1, KV_ROW, win_len))],
        out_specs=[per_b((1, 1, Q_NSA)), per_b((1, KV_ROW, win_len))],
        scratch_shapes=[pltpu.VMEM((2, 2 * n_blk, NSA_HD, PAGE_SIZE), F32), pltpu.SemaphoreType.DMA((2,))],
    )
    return pl.pallas_call(
        functools.partial(_s2_kernel, n_top=n_top, win_len=win_len), grid_spec=grid_spec,
        out_shape=[jax.ShapeDtypeStruct((B, 1, Q_NSA), F32), jax.ShapeDtypeStruct((B, KV_ROW, win_len), F32)],
        compiler_params=_cparams(("arbitrary",)), name="sample_sel_win_attn",
    )(phys, meta, pool_t, q3, ocmp, small3, nsel3, nwin3, nwint, cwint)


def _gdn_gates(b_raw, a_raw, alog, dtb):
    beta = jax.nn.sigmoid(b_raw)
    g = -jnp.exp(alog) * jax.nn.softplus(a_raw + dtb)
    return beta, g


def _l2n(x):
    return x * lax.rsqrt(jnp.sum(x * x, axis=-1, keepdims=True) + EPS)


def _gdn_prompt_kernel(qkv_ref, z_ref, small_ref, smallt_ref, cw_ref, alog_ref, dtb_ref, alogt_ref, dtbt_ref, on_ref,
                       o_ref, sfin_ref, conv_ref, s_ref, xx_ref):
    ci = pl.program_id(1)
    tc = qkv_ref.shape[1]
    C = GDN_CHUNK

    @pl.when(ci == 0)
    def _():
        s_ref[...] = jnp.zeros(s_ref.shape, F32)
        xx_ref[0:8, :] = jnp.zeros((8, CONV_DIM), F32)

    xx_ref[8:8 + tc, :] = qkv_ref[0]
    y = xx_ref[pl.ds(8 - (CONV_W - 1), tc), :] * cw_ref[0:1, :]
    for j in range(1, CONV_W):
        y = y + xx_ref[pl.ds(8 - (CONV_W - 1) + j, tc), :] * cw_ref[j:j + 1, :]
    c = jax.nn.silu(y)
    tail = xx_ref[tc:tc + 8, :]
    conv_ref[0] = tail
    xx_ref[0:8, :] = tail
    small = small_ref[0]
    beta, gcol = _gdn_gates(small[:, B_OFF:B_OFF + GDN_HEADS], small[:, A_OFF:A_OFF + GDN_HEADS], alog_ref[...], dtb_ref[...])
    _, grow = _gdn_gates(smallt_ref[0][0:GDN_HEADS], smallt_ref[0][GDN_HEADS:2 * GDN_HEADS], alogt_ref[...], dtbt_ref[...])
    z = z_ref[0]
    ii = lax.broadcasted_iota(jnp.int32, (C, C), 0)
    jj = lax.broadcasted_iota(jnp.int32, (C, C), 1)
    tril = ii >= jj
    strict = ii > jj
    eye = jnp.where(ii == jj, 1.0, 0.0)
    hcs = [(cc, h) for cc in range(tc // C) for h in range(GDN_HEADS)]
    loc = {}
    for cc, h in hcs:
        rs = slice(cc * C, (cc + 1) * C)
        qh = _l2n(c[rs, h * GDN_DK:(h + 1) * GDN_DK]) * (GDN_DK ** -0.5)
        kh = _l2n(c[rs, GDN_QK + h * GDN_DK:GDN_QK + (h + 1) * GDN_DK])
        vh = c[rs, 2 * GDN_QK + h * GDN_DV:2 * GDN_QK + (h + 1) * GDN_DV]
        bcol = beta[rs, h:h + 1]
        g_c = gcol[rs, h:h + 1]
        g_r = grow[h:h + 1, rs]
        dec_c = jnp.sum(jnp.where(tril, g_r, 0.0), axis=1, keepdims=True)
        dec_r = jnp.sum(jnp.where(ii <= jj, g_c, 0.0), axis=0, keepdims=True)
        lmask = jnp.where(tril, jnp.exp(jnp.where(tril, dec_c - dec_r, 0.0)), 0.0)
        kb = kh * bcol
        edec = jnp.exp(dec_c)
        dlast = dec_c[C - 1:C, :]
        loc[cc, h] = dict(
            m=-jnp.where(strict, _dot_nt(kb, kh) * lmask, 0.0), rhs=jnp.concatenate([vh * bcol, kb * edec], axis=1),
            attn=_dot_nt(qh, kh) * lmask, qe=qh * edec, kdt=(kh * jnp.exp(dlast - dec_c)).T, elast=jnp.exp(dlast))
    tinv = {k: eye + loc[k]["m"] for k in hcs}
    mpow = {k: _dot3(loc[k]["m"], loc[k]["m"]) for k in hcs}
    n_steps = (C - 1).bit_length() - 1
    for step in range(n_steps):
        for k in hcs:
            if step < n_steps - 1:
                r = _dot3(mpow[k], jnp.concatenate([mpow[k], tinv[k]], axis=1))
                mpow[k], tinv[k] = r[:, :C], tinv[k] + r[:, C:]
            else:
                tinv[k] = tinv[k] + _dot3(mpow[k], tinv[k])
    uw = {k: _dot3(tinv[k], loc[k]["rhs"]) for k in hcs}
    for cc in range(tc // C):
        rs = slice(cc * C, (cc + 1) * C)
        for h in range(GDN_HEADS):
            d = loc[cc, h]
            s_old = s_ref[h]
            ws_qs = _dot(jnp.concatenate([uw[cc, h][:, GDN_DV:], d["qe"]], axis=0), s_old)
            v_new = uw[cc, h][:, :GDN_DV] - ws_qs[:C]
            o = ws_qs[C:] + _dot(d["attn"], v_new)
            s_ref[h] = s_old * d["elast"] + _dot(d["kdt"], v_new)
            o = _rms(o) * on_ref[...] * jax.nn.silu(z[rs, h * GDN_DV:(h + 1) * GDN_DV])
            o_ref[0, rs, h * GDN_DV:(h + 1) * GDN_DV] = o

    @pl.when(ci == pl.num_programs(1) - 1)
    def _():
        sfin_ref[0] = s_ref[...]


def _gdn_prompt(qkv, z, small, wts, tc=256):
    B, T, _ = qkv.shape
    smallt = jnp.transpose(small[:, :, B_OFF:B_OFF + 2 * GDN_HEADS], (0, 2, 1))
    tokb = lambda w: pl.BlockSpec((1, tc, w), lambda b, c: (b, c, 0))
    return pl.pallas_call(
        _gdn_prompt_kernel, grid=(B, T // tc),
        in_specs=[tokb(CONV_DIM), tokb(GDN_V), tokb(SMALL_W), pl.BlockSpec((1, 2 * GDN_HEADS, tc), lambda b, c: (b, 0, c)),
                  _const_spec((CONV_W, CONV_DIM)), _const_spec((1, GDN_HEADS)), _const_spec((1, GDN_HEADS)),
                  _const_spec((GDN_HEADS, 1)), _const_spec((GDN_HEADS, 1)), _const_spec((1, GDN_DV))],
        out_specs=[tokb(GDN_V), pl.BlockSpec((1, GDN_HEADS, GDN_DK, GDN_DV), lambda b, c: (b, 0, 0, 0)),
                   pl.BlockSpec((1, 8, CONV_DIM), lambda b, c: (b, 0, 0))],
        out_shape=[jax.ShapeDtypeStruct((B, T, GDN_V), F32), jax.ShapeDtypeStruct((B, GDN_HEADS, GDN_DK, GDN_DV), F32),
                   jax.ShapeDtypeStruct((B, 8, CONV_DIM), F32)],
        scratch_shapes=[pltpu.VMEM((GDN_HEADS, GDN_DK, GDN_DV), F32), pltpu.VMEM((tc + 8, CONV_DIM), F32)],
        compiler_params=_cparams(("parallel", "arbitrary")), name="gdn_prompt",
    )(qkv, z, small, smallt, wts["conv_w"], wts["alog"], wts["dtb"], wts["alog_t"], wts["dtb_t"], wts["onorm"])


def _gdn_sample_kernel(xx_ref, z_ref, small_ref, cw_ref, alog_ref, dtb_ref, on_ref, s_ref,
                       o_ref, sout_ref, qt_ref, kt_ref, wt_ref, u_ref, sc_ref):
    b = pl.program_id(0)
    nb = z_ref.shape[0]

    @pl.when(b == 0)
    def _():
        y = xx_ref[0] * cw_ref[0:1, :]
        for j in range(1, CONV_W):
            y = y + xx_ref[j] * cw_ref[j:j + 1, :]
        c = jax.nn.silu(y)
        small = small_ref[...]
        beta, g = _gdn_gates(small[:, B_OFF:B_OFF + GDN_HEADS], small[:, A_OFF:A_OFF + GDN_HEADS], alog_ref[...], dtb_ref[...])
        a = jnp.exp(g)
        attn = []
        for h in range(GDN_HEADS):
            qh = _l2n(c[:, h * GDN_DK:(h + 1) * GDN_DK]) * (GDN_DK ** -0.5)
            kh = _l2n(c[:, GDN_QK + h * GDN_DK:GDN_QK + (h + 1) * GDN_DK])
            vh = c[:, 2 * GDN_QK + h * GDN_DV:2 * GDN_QK + (h + 1) * GDN_DV]
            bh, ah = beta[:, h:h + 1], a[:, h:h + 1]
            qt_ref[h] = (qh * ah).T
            kt_ref[h] = kh.T
            wt_ref[h] = (kh * bh * ah).T
            u_ref[h] = vh * bh
            attn.append(jnp.sum(qh * kh, axis=1, keepdims=True))
        sc_ref[...] = jnp.concatenate([a] + attn + [jnp.zeros((nb, LANES - 2 * GDN_HEADS), F32)], axis=1)

    lane = lax.broadcasted_iota(jnp.int32, (1, nb), 1)
    pick = lane == b
    sc = sc_ref[pl.ds(b, 1), :]
    zrow = z_ref[pl.ds(b, 1), :]
    for h in range(GDN_HEADS):
        wcol = jnp.sum(jnp.where(pick, wt_ref[h], 0.0), axis=1, keepdims=True)
        qcol = jnp.sum(jnp.where(pick, qt_ref[h], 0.0), axis=1, keepdims=True)
        kcol = jnp.sum(jnp.where(pick, kt_ref[h], 0.0), axis=1, keepdims=True)
        s_old = s_ref[0, h]
        v_new = u_ref[h, pl.ds(b, 1), :] - jnp.sum(s_old * wcol, axis=0, keepdims=True)
        o = jnp.sum(s_old * qcol, axis=0, keepdims=True) + sc[:, GDN_HEADS + h:GDN_HEADS + h + 1] * v_new
        sout_ref[0, h] = s_old * sc[:, h:h + 1] + kcol * v_new
        o = _rms(o) * on_ref[...] * jax.nn.silu(zrow[:, h * GDN_DV:(h + 1) * GDN_DV])
        o_ref[0, :, h * GDN_DV:(h + 1) * GDN_DV] = o


def _gdn_sample(xx4, z2, small2, state, wts):
    nb = z2.shape[0]
    cst = lambda shape: pl.BlockSpec(shape, lambda b: (0,) * len(shape))
    sspec = pl.BlockSpec((1, GDN_HEADS, GDN_DK, GDN_DV), lambda b: (b, 0, 0, 0))
    return pl.pallas_call(
        _gdn_sample_kernel, grid=(nb,),
        in_specs=[cst(xx4.shape), cst(z2.shape), cst(small2.shape), cst((CONV_W, CONV_DIM)), cst((1, GDN_HEADS)),
                  cst((1, GDN_HEADS)), cst((1, GDN_DV)), sspec],
        out_specs=[pl.BlockSpec((1, 1, GDN_V), lambda b: (b, 0, 0)), sspec],
        out_shape=[jax.ShapeDtypeStruct((nb, 1, GDN_V), F32), jax.ShapeDtypeStruct(state.shape, F32)],
        scratch_shapes=[pltpu.VMEM((GDN_HEADS, GDN_DK, nb), F32)] * 3
        + [pltpu.VMEM((GDN_HEADS, nb, GDN_DV), F32), pltpu.VMEM((nb, LANES), F32)],
        compiler_params=_cparams(("arbitrary",)), name="gdn_sample",
    )(xx4, z2, small2, wts["conv_w"], wts["alog"], wts["dtb"], wts["onorm"], state)


def _mem_prompt_kernel(qm_ref, kv_ref, o_ref):
    qm = qm_ref[0]
    kv = kv_ref[0]
    for h in range(MEM_HEADS):
        sl = slice(h * MEM_HD, (h + 1) * MEM_HD)
        s = _dot_nt(qm[:, sl], kv[:, sl]) * MEM_SCALE
        e = jnp.exp(s - jnp.max(s, axis=-1, keepdims=True))
        p = e / jnp.sum(e, axis=-1, keepdims=True)
        o_ref[0, :, sl] = _dot(p, kv[:, MEM_Q + h * MEM_HD:MEM_Q + (h + 1) * MEM_HD])


def _mem_prompt(qm, mkv, tq=512):
    B, T, _ = qm.shape
    M = mkv.shape[1]
    return pl.pallas_call(
        _mem_prompt_kernel, grid=(B, T // tq),
        in_specs=[pl.BlockSpec((1, tq, MEM_Q), lambda b, t: (b, t, 0)), pl.BlockSpec((1, M, 2 * MEM_Q), lambda b, t: (b, 0, 0))],
        out_specs=pl.BlockSpec((1, tq, MEM_Q), lambda b, t: (b, t, 0)),
        out_shape=jax.ShapeDtypeStruct((B, T, MEM_Q), F32),
        compiler_params=_cparams(("parallel", "parallel")), name="mem_attn_prompt",
    )(qm, mkv)


def _mem_sample_kernel(qm_ref, kv_ref, o_ref):
    q = qm_ref[0].astype(F32)
    for h in range(MEM_HEADS):
        sl = slice(h * MEM_HD, (h + 1) * MEM_HD)
        k = kv_ref[0, :, 0, h, :]
        v = kv_ref[0, :, 1, h, :]
        s = jnp.sum(k * q[:, sl], axis=1, keepdims=True) * MEM_SCALE
        e = jnp.exp(s - jnp.max(s, axis=0, keepdims=True))
        p = e / jnp.sum(e, axis=0, keepdims=True)
        o_ref[0, :, sl] = jnp.sum(v * p, axis=0, keepdims=True)


def _mem_sample(qm3, cache):
    B, M = cache.shape[0], cache.shape[1]
    return pl.pallas_call(
        _mem_sample_kernel, grid=(B,),
        in_specs=[pl.BlockSpec((1, 1, MEM_Q), lambda b: (b, 0, 0)),
                  pl.BlockSpec((1, M, 2, MEM_HEADS, MEM_HD), lambda b: (b, 0, 0, 0, 0))],
        out_specs=pl.BlockSpec((1, 1, MEM_Q), lambda b: (b, 0, 0)),
        out_shape=jax.ShapeDtypeStruct((B, 1, MEM_Q), F32),
        compiler_params=_cparams(("parallel",)), name="mem_attn_sample",
    )(qm3, cache)


def _merge_ffn_kernel(on_ref, og_ref, om_ref, gate_ref, x_ref, wbn_ref, wbg_ref, wbm_ref, wout_ref, gffn_ref, w1_ref, w2_ref, y_ref):
    merged = (gate_ref[:, 0:D_MODEL] * _dot(on_ref[...], wbn_ref[...])
              + gate_ref[:, D_MODEL:2 * D_MODEL] * _dot(og_ref[...], wbg_ref[...])
              + gate_ref[:, 2 * D_MODEL:3 * D_MODEL] * _dot(om_ref[...], wbm_ref[...]))
    h = x_ref[...] + _dot(merged, wout_ref[...])
    f = _dot(_rms(h) * gffn_ref[...], w1_ref[...])
    f = jnp.square(jnp.maximum(f, 0.0))
    y_ref[...] = h + _dot(f, w2_ref[...])


def _merge_ffn(o_nsa, o_gdn, o_mem, gates, x2d, wts, tm):
    n = x2d.shape[0]
    tok = lambda w: pl.BlockSpec((tm, w), lambda i: (i, 0))
    return pl.pallas_call(
        _merge_ffn_kernel, grid=(n // tm,),
        in_specs=[tok(Q_NSA), tok(GDN_V), tok(MEM_Q), tok(N_BRANCH * D_MODEL), tok(D_MODEL),
                  _const_spec((Q_NSA, D_MODEL)), _const_spec((GDN_V, D_MODEL)), _const_spec((MEM_Q, D_MODEL)),
                  _const_spec((D_MODEL, D_MODEL)), _const_spec((1, D_MODEL)), _const_spec((D_MODEL, D_FF)), _const_spec((D_FF, D_MODEL))],
        out_specs=tok(D_MODEL),
        out_shape=jax.ShapeDtypeStruct((n, D_MODEL), F32),
        compiler_params=_cparams(("parallel",)), name="merge_ffn",
    )(o_nsa, o_gdn, o_mem, gates, x2d, wts["w_br_nsa"], wts["w_br_gdn"], wts["w_br_mem"], wts["w_out"], wts["g_ffn"],
      wts["w_ff1"], wts["w_ff2"])


def _prep_weights(g_mix, w_in, nsa_q_norm, nsa_k_norm, cmp_pe, cmp_w1, cmp_w2, gdn_conv_w, gdn_A_log, gdn_dt_bias, gdn_o_norm,
                  g_mem, w_mem_kv, mem_q_norm, mem_k_norm, w_br_nsa, w_br_gdn, w_br_mem, w_out, g_ffn, w_ff1, w_ff2):
    offs = [0]
    for s in IN_SPLITS:
        offs.append(offs[-1] + s)
    wt = w_in.T
    seg = lambda i: wt[offs[i]:offs[i + 1]]
    small = jnp.concatenate([seg(2), seg(4), seg(5)], axis=0)
    small = jnp.pad(small, ((0, SMALL_W - small.shape[0]), (0, 0)))
    kv = seg(1)
    k_of = lambda c: kv[c * KV_ROW:c * KV_ROW + KV_ROW // 2]
    w_tok = jnp.concatenate([seg(0), seg(3), seg(6), seg(7), seg(8), small, k_of(1), k_of(2)], axis=0).T.astype(BF16)
    w_ft = jnp.concatenate([kv, seg(0), jnp.pad(seg(2), ((0, G_ROWS - 3 * NSA_HEADS), (0, 0)))], axis=0).astype(BF16)
    R = CMP_BLOCK // CMP_STRIDE
    w1r = cmp_w1.reshape(2, R, CMP_STRIDE, NSA_HD, CMP_HID)
    wbd = jnp.einsum("krsdf,gh->ksgdhrf", w1r, jnp.eye(NSA_KV_HEADS, dtype=F32))
    wbd = wbd.reshape(2, CMP_STRIDE * NSA_KV_HEADS * NSA_HD, NSA_KV_HEADS * R * CMP_HID).astype(BF16)
    row = lambda v: v.reshape(1, -1)
    return dict(
        g_mix=row(g_mix), w_tok=w_tok, w_ft=w_ft, qn=row(jnp.tile(nsa_q_norm, LANES // NSA_HD)), qn_col=nsa_q_norm.reshape(NSA_HD, 1),
        kn=nsa_k_norm.reshape(3, NSA_HD, 1), kn_row=jnp.tile(nsa_k_norm, (1, LANES // NSA_HD)), mqn=row(mem_q_norm), mkn=row(mem_k_norm), g_mem=row(g_mem),
        w_mem_kv=w_mem_kv.astype(BF16), wbd=wbd,
        pe=jnp.broadcast_to(cmp_pe.reshape(2, 1, CMP_BLOCK * NSA_HD), (2, 8, CMP_BLOCK * NSA_HD)),
        w1=cmp_w1.astype(BF16), w2=cmp_w2.astype(BF16), conv_w=gdn_conv_w,
        alog=row(gdn_A_log), dtb=row(gdn_dt_bias), alog_t=gdn_A_log.reshape(-1, 1), dtb_t=gdn_dt_bias.reshape(-1, 1),
        onorm=row(gdn_o_norm), w_br_nsa=w_br_nsa.astype(BF16), w_br_gdn=w_br_gdn.astype(BF16), w_br_mem=w_br_mem.astype(BF16),
        w_out=w_out.astype(BF16), g_ffn=row(g_ffn), w_ff1=w_ff1.astype(BF16), w_ff2=w_ff2.astype(BF16))


def _rows_5d(kt):
    B, _, N = kt.shape
    return jnp.transpose(kt.reshape(B, 2, NSA_KV_HEADS, NSA_HD, N), (0, 4, 1, 2, 3))


def _feature_major(rows):
    B, N = rows.shape[0], rows.shape[1]
    return jnp.transpose(rows, (0, 2, 3, 4, 1)).reshape(B, KV_ROW, N)


def kernel(x_prompt, x_sample, cache_cmp_kv, cache_sel_kv, cache_win_kv, state_gdn, state_gdn_conv, cache_mem_kv, page_table, mem_prompt, g_mix, w_in, nsa_q_norm, nsa_k_norm, cmp_pe, cmp_w1, cmp_w2, gdn_conv_w, gdn_A_log, gdn_dt_bias, gdn_o_norm, g_mem, w_mem_kv, mem_q_norm, mem_k_norm, w_br_nsa, w_br_gdn, w_br_mem, w_out, g_ffn, w_ff1, w_ff2):
    wts = _prep_weights(g_mix, w_in, nsa_q_norm, nsa_k_norm, cmp_pe, cmp_w1, cmp_w2, gdn_conv_w, gdn_A_log, gdn_dt_bias,
                        gdn_o_norm, g_mem, w_mem_kv, mem_q_norm, mem_k_norm, w_br_nsa, w_br_gdn, w_br_mem, w_out, g_ffn, w_ff1, w_ff2)
    B, T, D = x_prompt.shape
    nb = x_sample.shape[0]
    assert x_sample.shape[1] == 1 and T % 256 == 0 and T >= WINDOW + 128
    n_pages = page_table.shape[1]
    past = n_pages * PAGE_SIZE

    p = _inproj(x_prompt, jnp.arange(T, dtype=jnp.int32), wts, tm=256, decode=False)
    mkv = _memkv(mem_prompt.reshape(-1, D), wts).reshape(B, -1, 2 * MEM_Q)
    ckv = _pcompress(p["kct"], wts)
    o_nsa = _pattn_t(p, ckv)
    o_gdn, p_state, conv_tail = _gdn_prompt(p["qkv"], p["z"], p["small"], wts)
    o_mem = _mem_prompt(p["qm"], mkv)
    y_prompt = _merge_ffn(o_nsa.reshape(-1, Q_NSA), o_gdn.reshape(-1, GDN_V), o_mem.reshape(-1, MEM_Q),
                          p["gates"].reshape(-1, N_BRANCH * D), x_prompt.reshape(-1, D), wts, tm=256).reshape(B, T, D)
    p_cmp, p_sel = _rows_5d(p["kct"]), _rows_5d(p["kst"])
    p_win = _rows_5d(p["kwt"][:, :, T - min(WINDOW, T):])
    p_conv = conv_tail[:, 8 - (CONV_W - 1):, :]
    p_mem_kv = mkv.reshape(B, -1, 2, MEM_HEADS, MEM_HD)

    xs = x_sample.reshape(1, nb, D)
    s = _inproj(xs, jnp.full((nb,), past, jnp.int32), wts, tm=nb, decode=True)
    sq, sqkv, sz, sqm, sgates, ssmall, skct, skst, skwt = (s[k] for k in ("q", "qkv", "z", "qm", "gates", "small", "kct", "kst", "kwt"))
    pool_cmp = _feature_major(cache_cmp_kv)
    pool_sel = _feature_major(cache_sel_kv)
    q3 = sq.reshape(nb, 1, Q_NSA)
    ocmp, idx = _s1(pool_cmp, page_table, q3, wts, q_pos=past)
    n_sel = -(-(past + 1) // SEL_BLOCK)
    n_top = min(SEL_TOP, n_sel)
    idx = idx[:, :NSA_KV_HEADS, :n_top].reshape(nb, NSA_KV_HEADS * n_top)
    blk_per_page = PAGE_SIZE // SEL_BLOCK
    is_new = idx * SEL_BLOCK >= past
    page = jnp.take_along_axis(page_table, jnp.where(is_new, 0, idx // blk_per_page), axis=1)
    meta = jnp.where(is_new, blk_per_page, idx % blk_per_page).astype(jnp.int32)
    tokrow = lambda kt: jnp.transpose(kt[0], (1, 0)).reshape(nb, 1, KV_ROW)
    cwint = _feature_major(cache_win_kv)
    o_nsa_s, swin = _s2(pool_sel, page.astype(jnp.int32), meta, q3, ocmp, ssmall.reshape(nb, 1, SMALL_W), tokrow(skst), tokrow(skwt),
                        skwt[0], cwint, n_top)
    xx4 = jnp.concatenate([jnp.transpose(state_gdn_conv, (1, 0, 2)), sqkv], axis=0)
    o_gdn_s, s_state = _gdn_sample(xx4, sz[0], ssmall[0], state_gdn, wts)
    o_mem_s = _mem_sample(sqm.reshape(nb, 1, MEM_Q), cache_mem_kv)
    y_sample = _merge_ffn(o_nsa_s.reshape(nb, Q_NSA), o_gdn_s.reshape(nb, GDN_V), o_mem_s.reshape(nb, MEM_Q), sgates[0], x_sample.reshape(nb, D),
                          wts, tm=nb).reshape(nb, 1, D)
    s_cmp = jnp.transpose(skct[0], (1, 0)).reshape(nb, 1, 2, NSA_KV_HEADS, NSA_HD)
    s_sel = jnp.transpose(skst[0], (1, 0)).reshape(nb, 1, 2, NSA_KV_HEADS, NSA_HD)
    s_win = _rows_5d(swin)
    s_conv = jnp.transpose(xx4[1:], (1, 0, 2))
    return (y_prompt, y_sample, p_cmp, p_sel, p_win, p_state, p_conv, p_mem_kv, s_cmp, s_sel, s_win, s_state, s_conv)
```
